```python
import math
import jax, jax.numpy as jnp
from jax import lax
import numpy as np

D_MODEL = 1024
BATCH = 8
SEQ = 2048
DEPTH = 1
DEC_BATCH = 128
DEC_SEQ = 1
PAST_LEN = 16384
PAGE_SIZE = 128

EPS = 1e-6
D_LRU = D_MODEL
LRU_BLOCKS = 8
LRU_BW = D_LRU // LRU_BLOCKS
LRU_C = 8.0
CONV_W = 4
GLA_HEADS = 4
GLA_DV = D_MODEL // GLA_HEADS
GLA_DK = GLA_DV // 2
GLA_RANK = 16
GLA_TAU = 16.0
GLA_CHUNK = 64
D_GLA = GLA_HEADS * GLA_DV
D_MIX = D_LRU + D_GLA
IN_SIZES = (D_LRU, D_LRU, GLA_HEADS * GLA_DK, GLA_HEADS * GLA_DK, D_GLA, GLA_RANK, D_GLA)
D_IN = sum(IN_SIZES)
SPLITS = [int(s) for s in np.cumsum(IN_SIZES)[:-1]]

kernel_name = "hymba_rglru_gla_adaln_step"


def _rmsnorm(x, g):
    xf = x.astype(jnp.float32)
    y = xf * lax.rsqrt(jnp.mean(xf * xf, axis=-1, keepdims=True) + EPS)
    return (y * g.astype(jnp.float32)).astype(x.dtype)


def _lin_comb(left, right):
    a1, b1 = left
    a2, b2 = right
    return a1 * a2, a2 * b1 + b2


def _gla_chunked(q, k, v, g, S0):
    Bn, T, H, K = q.shape
    V = v.shape[-1]
    C = min(GLA_CHUNK, T)
    n = -(-T // C)
    pad = n * C - T
    padw = ((0, 0), (0, pad), (0, 0), (0, 0))
    q, k, v, g = (jnp.pad(t, padw) for t in (q, k, v, g))
    def blk(t):
        return t.reshape(Bn, n, C, H, t.shape[-1]).transpose(1, 0, 3, 2, 4)
    q, k, v, g = blk(q), blk(k), blk(v), blk(g)
    b = jnp.cumsum(g, axis=3)
    b_last = b[:, :, :, -1:, :]
    q_in = q * jnp.exp(b)
    k_in = k * jnp.exp(-b)
    k_dec = k * jnp.exp(b_last - b)
    mask = jnp.tril(jnp.ones((C, C), dtype=bool))
    A = jnp.einsum('nbhck,nbhsk->nbhcs', q_in, k_in)
    A = jnp.where(mask, A, 0.0)
    o_intra = jnp.einsum('nbhcs,nbhsv->nbhcv', A, v)

    def step(S, xs):
        qn, kn, vn, dl = xs
        o = jnp.einsum('bhck,bhkv->bhcv', qn, S)
        S = jnp.exp(dl[:, :, 0, :])[..., None] * S + jnp.einsum('bhck,bhcv->bhkv', kn, vn)
        return S, o

    S_fin, o_inter = lax.scan(step, S0, (q_in, k_dec, v, b_last))
    o = (o_intra + o_inter).transpose(1, 0, 3, 2, 4).reshape(Bn, n * C, H, V)[:, :T]
    return o, S_fin


def _layer(x, c, h0, conv0, S0, g_norm, w_ada, b_ada, w_in, conv_w, conv_b,
           w_gate_x, b_gate_x, w_gate_a, b_gate_a, lru_lambda,
           w_gla_g2, b_gla_g2, g_gla_norm, w_out):
    Bn, T, _ = x.shape
    f32 = jnp.float32
    ada = jax.nn.silu(c) @ w_ada + b_ada
    shift, scale, gate = jnp.split(ada, 3, axis=-1)
    hn = _rmsnorm(x, g_norm) * (1.0 + scale[:, None]) + shift[:, None]
    z = hn @ w_in
    xa, za, q, k, v, glr, zg = jnp.split(z, SPLITS, axis=-1)

    xp = jnp.concatenate([conv0.astype(xa.dtype), xa], axis=1)
    xc = conv_b + sum(xp[:, j:j + T] * conv_w[j] for j in range(CONV_W))
    new_conv = xp[:, T:]
    xb = xc.reshape(Bn, T, LRU_BLOCKS, LRU_BW)
    gx = jax.nn.sigmoid(jnp.einsum('btnd,nde->btne', xb, w_gate_x).reshape(Bn, T, D_LRU) + b_gate_x)
    ga = jax.nn.sigmoid(jnp.einsum('btnd,nde->btne', xb, w_gate_a).reshape(Bn, T, D_LRU) + b_gate_a)
    log_a = (LRU_C * ga.astype(f32)) * jax.nn.log_sigmoid(lru_lambda.astype(f32))
    a = jnp.exp(log_a)
    mult = jnp.sqrt(-jnp.expm1(2.0 * log_a))
    u = mult * (gx * xc).astype(f32)
    u = u.at[:, 0].add(a[:, 0] * h0.astype(f32))
    _, h = lax.associative_scan(_lin_comb, (a, u), axis=1)
    y_lru = h.astype(x.dtype) * jax.nn.silu(za)

    qh = q.reshape(Bn, T, GLA_HEADS, GLA_DK).astype(f32) * (GLA_DK ** -0.5)
    kh = k.reshape(Bn, T, GLA_HEADS, GLA_DK).astype(f32)
    vh = v.reshape(Bn, T, GLA_HEADS, GLA_DV).astype(f32)
    log_alpha = jax.nn.log_sigmoid((glr @ w_gla_g2 + b_gla_g2).astype(f32)) / GLA_TAU
    log_alpha = log_alpha.reshape(Bn, T, GLA_HEADS, GLA_DK)
    o, S_new = _gla_chunked(qh, kh, vh, log_alpha, S0.astype(f32))
    o = o * lax.rsqrt(jnp.mean(o * o, axis=-1, keepdims=True) + EPS) * g_gla_norm.astype(f32)
    y_gla = o.reshape(Bn, T, D_GLA).astype(x.dtype) * jax.nn.silu(zg)

    mix = jnp.concatenate([y_lru, y_gla], axis=-1) @ w_out
    x = x + gate[:, None] * mix
    return x, h[:, -1], new_conv, S_new


def setup_inputs(seed: int = 0) -> dict:
    key = jax.random.key(seed)
    ks = jax.random.split(key, 24)
    nrm = lambda k, s, sc: jax.random.normal(k, s, jnp.float32) * sc
    u = jax.random.uniform(ks[14], (DEPTH, D_LRU), jnp.float32, 0.9, 0.999)
    s = u ** (1.0 / LRU_C)
    lru_lambda = jnp.log(s) - jnp.log1p(-s)
    return {
        "x_prompt": nrm(ks[0], (BATCH, SEQ, D_MODEL), 1.0),
        "x_sample": nrm(ks[1], (DEC_BATCH, DEC_SEQ, D_MODEL), 1.0),
        "state_lru_h": nrm(ks[2], (DEPTH, DEC_BATCH, D_LRU), 0.5),
        "state_lru_conv": nrm(ks[3], (DEPTH, DEC_BATCH, CONV_W - 1, D_LRU), 1.0),
        "state_gla": nrm(ks[4], (DEPTH, DEC_BATCH, GLA_HEADS, GLA_DK, GLA_DV), 1.0),
        "c_prompt": nrm(ks[5], (BATCH, D_MODEL), 1.0),
        "c_sample": nrm(ks[6], (DEC_BATCH, D_MODEL), 1.0),
        "g_norm": 1.0 + nrm(ks[7], (DEPTH, D_MODEL), 0.02),
        "w_ada": nrm(ks[8], (DEPTH, D_MODEL, 3 * D_MODEL), 0.5 * D_MODEL ** -0.5),
        "b_ada": nrm(ks[9], (DEPTH, 3 * D_MODEL), 0.02),
        "w_in": nrm(ks[10], (DEPTH, D_MODEL, D_IN), D_MODEL ** -0.5),
        "conv_w": nrm(ks[11], (DEPTH, CONV_W, D_LRU), CONV_W ** -0.5),
        "conv_b": nrm(ks[12], (DEPTH, D_LRU), 0.02),
        "w_gate_x": nrm(ks[13], (DEPTH, LRU_BLOCKS, LRU_BW, LRU_BW), LRU_BW ** -0.5),
        "b_gate_x": nrm(ks[15], (DEPTH, D_LRU), 0.02),
        "w_gate_a": nrm(ks[16], (DEPTH, LRU_BLOCKS, LRU_BW, LRU_BW), LRU_BW ** -0.5),
        "b_gate_a": nrm(ks[17], (DEPTH, D_LRU), 0.02),
        "lru_lambda": lru_lambda,
        "w_gla_g2": nrm(ks[18], (DEPTH, GLA_RANK, GLA_HEADS * GLA_DK), GLA_RANK ** -0.5),
        "b_gla_g2": nrm(ks[19], (DEPTH, GLA_HEADS * GLA_DK), 0.02),
        "g_gla_norm": 1.0 + nrm(ks[20], (DEPTH, GLA_HEADS, GLA_DV), 0.02),
        "w_out": nrm(ks[21], (DEPTH, D_MIX, D_MODEL), D_MIX ** -0.5),
        "g_final": 1.0 + nrm(ks[22], (D_MODEL,), 0.02),
    }


def reference(x_prompt, x_sample, state_lru_h, state_lru_conv, state_gla, c_prompt, c_sample,
              g_norm, w_ada, b_ada, w_in, conv_w, conv_b, w_gate_x, b_gate_x, w_gate_a, b_gate_a,
              lru_lambda, w_gla_g2, b_gla_g2, g_gla_norm, w_out, g_final):
    bp = x_prompt.shape[0]
    yp, ys = x_prompt, x_sample
    hp_l, cp_l, sp_l, hs_l, cs_l, ss_l = [], [], [], [], [], []
    for l in range(DEPTH):
        w = (g_norm[l], w_ada[l], b_ada[l], w_in[l], conv_w[l], conv_b[l],
             w_gate_x[l], b_gate_x[l], w_gate_a[l], b_gate_a[l], lru_lambda[l],
             w_gla_g2[l], b_gla_g2[l], g_gla_norm[l], w_out[l])
        h0 = jnp.zeros((bp, D_LRU), jnp.float32)
        conv0 = jnp.zeros((bp, CONV_W - 1, D_LRU), x_prompt.dtype)
        S0 = jnp.zeros((bp, GLA_HEADS, GLA_DK, GLA_DV), jnp.float32)
        yp, hp, cp, sp = _layer(yp, c_prompt, h0, conv0, S0, *w)
        ys, hs, cs, ss = _layer(ys, c_sample, state_lru_h[l], state_lru_conv[l], state_gla[l], *w)
        hp_l.append(hp); cp_l.append(cp); sp_l.append(sp)
        hs_l.append(hs); cs_l.append(cs); ss_l.append(ss)
    yp = _rmsnorm(yp, g_final)
    ys = _rmsnorm(ys, g_final)
    return (yp, ys, jnp.stack(hp_l), jnp.stack(cp_l), jnp.stack(sp_l),
            jnp.stack(hs_l), jnp.stack(cs_l), jnp.stack(ss_l))
```

```python
import functools

import jax
import jax.numpy as jnp
from jax import lax
from jax.experimental import pallas as pl
from jax.experimental.pallas import tpu as pltpu

EPS = 1e-6
D = 1024
LRU_BLOCKS = 8
LRU_BW = D // LRU_BLOCKS
LRU_C = 8.0
CONV_W = 4
HEADS = 4
DV = D // HEADS
DK = DV // 2
RANK = 16
RANK_PAD = 128
TAU = 16.0
CHUNK = 64
D_QK = HEADS * DK
D_MIX = 2 * D

SUBLANES = 8
TM = 256
VMEM_LIMIT = 56 * 1024 * 1024

_BF = jnp.bfloat16
_F32 = jnp.float32

_NT = (((1,), (1,)), ((), ()))
_TN = (((0,), (0,)), ((), ()))


def _sigmoid(x):
    return 1.0 / (1.0 + jnp.exp(-x))


def _silu(x):
    return x * _sigmoid(x)


def _log_sigmoid(x):
    return jnp.minimum(x, 0.0) - jnp.log1p(jnp.exp(-jnp.abs(x)))


def _rms(x):
    return x * lax.rsqrt(jnp.mean(x * x, axis=-1, keepdims=True) + EPS)


def _dot(a, b):
    return jnp.dot(a, b, preferred_element_type=_F32)


def _group_scan(a3, u3):
    row = lax.broadcasted_iota(jnp.int32, a3.shape, 1)
    for s in (1, 2, 4):
        a_sh = pltpu.roll(a3, s, axis=1)
        u_sh = pltpu.roll(u3, s, axis=1)
        m = row >= s
        u3 = jnp.where(m, a3 * u_sh + u3, u3)
        a3 = jnp.where(m, a3 * a_sh, a3)
    return a3, u3


def _group_cumsum(x3):
    row = lax.broadcasted_iota(jnp.int32, x3.shape, 1)
    for s in (1, 2, 4):
        x3 = jnp.where(row >= s, x3 + pltpu.roll(x3, s, axis=1), x3)
    return x3


def _linear_scan(a, u, h0):
    r, l = a.shape
    g = r // SUBLANES
    a3, u3 = _group_scan(a.reshape(g, SUBLANES, l), u.reshape(g, SUBLANES, l))
    carry = h0
    out = []
    for i in range(g):
        hg = a3[i] * carry + u3[i]
        out.append(hg)
        carry = hg[SUBLANES - 1:SUBLANES]
    return jnp.concatenate(out, axis=0)


def _chunk_cumsum(x):
    r, l = x.shape
    g = r // SUBLANES
    per_chunk = CHUNK // SUBLANES
    x3 = _group_cumsum(x.reshape(g, SUBLANES, l))
    out = []
    carry = None
    for i in range(g):
        xg = x3[i] if i % per_chunk == 0 else x3[i] + carry
        out.append(xg)
        carry = xg[SUBLANES - 1:SUBLANES]
    return jnp.concatenate(out, axis=0)


def _lru_block(xc_n, za_n, pre, bgx, bga, logsig, h0):
    gx = _sigmoid(pre[:, :LRU_BW] + bgx)
    ga = _sigmoid(pre[:, LRU_BW:] + bga)
    log_a = (LRU_C * ga) * logsig
    a = jnp.exp(log_a)
    mult = jnp.sqrt(1.0 - a * a)
    u = mult * (gx * xc_n)
    return a, u


def _ada_kernel(c_ref, w_ref, b_ref, o_ref):
    c = c_ref[...]
    o_ref[...] = _dot(_silu(c).astype(_BF), w_ref[...]) + b_ref[...]


def _ada(c_all, w_ada, b_ada):
    n = c_all.shape[0]
    return pl.pallas_call(
        _ada_kernel,
        out_shape=jax.ShapeDtypeStruct((n, 3 * D), _F32),
        grid=(3,),
        in_specs=[
            pl.BlockSpec((n, D), lambda j: (0, 0)),
            pl.BlockSpec((D, D), lambda j: (0, j)),
            pl.BlockSpec((1, D), lambda j: (0, j)),
        ],
        out_specs=pl.BlockSpec((n, D), lambda j: (0, j)),
        compiler_params=pltpu.CompilerParams(dimension_semantics=("arbitrary",)),
        name="ada",
    )(c_all, w_ada, b_ada)


def _prompt_kernel(x_ref, ada_ref, gnorm_ref, wmain_ref, wglr_ref, convw_ref, convb_ref,
                   wgate_ref, bgx_ref, bga_ref, lam_ref, wg2_ref, bg2_ref, ggla_ref,
                   wout_ref, gfin_ref,
                   y_ref, hlast_ref, convout_ref, sout_ref,
                   xa_s, za_s, q_s, k_s, v_s, zg_s, b_s, mix_s, h_s, st_s, *, tm):
    t = pl.program_id(1)
    nt = pl.num_programs(1)

    @pl.when(t == 0)
    def _():
        xa_s[0:SUBLANES, :] = jnp.zeros((SUBLANES, D), _F32)
        h_s[...] = jnp.zeros_like(h_s)
        st_s[...] = jnp.zeros_like(st_s)

    x = x_ref[0]
    shift = ada_ref[0, 0:1, :]
    scale = ada_ref[0, 1:2, :]
    gate = ada_ref[0, 2:3, :]
    hn = (_rms(x) * gnorm_ref[...]) * (1.0 + scale) + shift
    hnb = hn.astype(_BF)

    xa_s[SUBLANES:SUBLANES + tm, :] = _dot(hnb, wmain_ref[:, 0:D])
    za_s[...] = _dot(hnb, wmain_ref[:, D:2 * D])
    q_s[...] = _dot(hnb, wmain_ref[:, 2 * D:2 * D + D_QK])
    k_s[...] = _dot(hnb, wmain_ref[:, 2 * D + D_QK:3 * D])
    v_s[...] = _dot(hnb, wmain_ref[:, 3 * D:4 * D])
    zg_s[...] = _dot(hnb, wmain_ref[:, 4 * D:5 * D])
    glr = _dot(hnb, wglr_ref[...])
    la = _log_sigmoid(_dot(glr.astype(_BF), wg2_ref[...]) + bg2_ref[...]) / TAU
    b_s[...] = _chunk_cumsum(la)

    xc = convb_ref[...]
    for j in range(CONV_W):
        off = SUBLANES - (CONV_W - 1) + j
        xc = xc + xa_s[off:off + tm, :] * convw_ref[j:j + 1, :]
    xcb = xc.astype(_BF)
    logsig = _log_sigmoid(lam_ref[...])
    for n in range(LRU_BLOCKS):
        ln = slice(n * LRU_BW, (n + 1) * LRU_BW)
        pre = _dot(xcb[:, ln], wgate_ref[n])
        a, u = _lru_block(xc[:, ln], None, pre, bgx_ref[:, ln], bga_ref[:, ln],
                          logsig[:, ln], None)
        h = _linear_scan(a, u, h_s[:, ln])
        h_s[:, ln] = h[tm - 1:tm]
        mix_s[:, ln] = (h * _silu(za_s[:, ln])).astype(_BF)

    @pl.when(t == nt - 1)
    def _():
        hlast_ref[0] = h_s[...]
        convout_ref[0] = xa_s[SUBLANES + tm - (CONV_W - 1):SUBLANES + tm, :]

    xa_s[0:SUBLANES, :] = xa_s[tm:tm + SUBLANES, :]

    tril = (lax.broadcasted_iota(jnp.int32, (CHUNK, CHUNK), 0)
            >= lax.broadcasted_iota(jnp.int32, (CHUNK, CHUNK), 1))

    def chunk_body(c, carry):
        rows = pl.ds(pl.multiple_of(c * CHUNK, CHUNK), CHUNK)
        for h in range(HEADS):
            lk = slice(h * DK, (h + 1) * DK)
            lv = slice(h * DV, (h + 1) * DV)
            b = b_s[rows, lk]
            bl = b[CHUNK - 1:CHUNK]
            q = q_s[rows, lk] * (DK ** -0.5)
            k = k_s[rows, lk]
            v = v_s[rows, lv]
            qi = (q * jnp.exp(b)).astype(_BF)
            ki = (k * jnp.exp(-b)).astype(_BF)
            kd = (k * jnp.exp(bl - b)).astype(_BF)
            vb = v.astype(_BF)
            att = lax.dot_general(qi, ki, _NT, preferred_element_type=_F32)
            att = jnp.where(tril, att, 0.0).astype(_BF)
            st = st_s[h]
            o = _dot(att, vb) + lax.dot_general(qi, st.astype(_BF), _NT,
                                                preferred_element_type=_F32)
            st_s[h] = st * jnp.exp(bl) + _dot(v.T.astype(_BF), kd)
            o = _rms(o) * ggla_ref[:, lv]
            mix_s[rows, D + h * DV:D + (h + 1) * DV] = (o * _silu(zg_s[rows, lv])).astype(_BF)
        return carry

    lax.fori_loop(0, tm // CHUNK, chunk_body, 0)

    @pl.when(t == nt - 1)
    def _():
        for h in range(HEADS):
            sout_ref[0, h] = st_s[h].T

    y = x + gate * _dot(mix_s[...], wout_ref[...])
    y_ref[0] = _rms(y) * gfin_ref[...]


def _const_spec(shape):
    nd = len(shape)
    return pl.BlockSpec(shape, lambda b, t: (0,) * nd, pipeline_mode=pl.Buffered(1))


def _prompt(x, ada3, gnorm, wmain, wglr, convw, convb, wgate, bgx, bga, lam, wg2, bg2, ggla,
            wout, gfin, tm=TM):
    bsz, seq, _ = x.shape
    nt = seq // tm
    kern = functools.partial(_prompt_kernel, tm=tm)
    out_shape = (
        jax.ShapeDtypeStruct((bsz, seq, D), _F32),
        jax.ShapeDtypeStruct((bsz, 1, D), _F32),
        jax.ShapeDtypeStruct((bsz, CONV_W - 1, D), _F32),
        jax.ShapeDtypeStruct((bsz, HEADS, DK, DV), _F32),
    )
    in_specs = [
        pl.BlockSpec((1, tm, D), lambda b, t: (b, t, 0)),
        pl.BlockSpec((1, 3, D), lambda b, t: (b, 0, 0)),
        _const_spec((1, D)),
        _const_spec((D, 5 * D)),
        _const_spec((D, RANK_PAD)),
        _const_spec((CONV_W, D)),
        _const_spec((1, D)),
        _const_spec((LRU_BLOCKS, LRU_BW, 2 * LRU_BW)),
        _const_spec((1, D)),
        _const_spec((1, D)),
        _const_spec((1, D)),
        _const_spec((RANK_PAD, D_QK)),
        _const_spec((1, D_QK)),
        _const_spec((1, D)),
        _const_spec((D_MIX, D)),
        _const_spec((1, D)),
    ]
    out_specs = (
        pl.BlockSpec((1, tm, D), lambda b, t: (b, t, 0)),
        pl.BlockSpec((1, 1, D), lambda b, t: (b, 0, 0)),
        pl.BlockSpec((1, CONV_W - 1, D), lambda b, t: (b, 0, 0)),
        pl.BlockSpec((1, HEADS, DK, DV), lambda b, t: (b, 0, 0, 0)),
    )
    scratch = [
        pltpu.VMEM((tm + SUBLANES, D), _F32),
        pltpu.VMEM((tm, D), _F32),
        pltpu.VMEM((tm, D_QK), _F32),
        pltpu.VMEM((tm, D_QK), _F32),
        pltpu.VMEM((tm, D), _F32),
        pltpu.VMEM((tm, D), _F32),
        pltpu.VMEM((tm, D_QK), _F32),
        pltpu.VMEM((tm, D_MIX), _BF),
        pltpu.VMEM((1, D), _F32),
        pltpu.VMEM((HEADS, DV, DK), _F32),
    ]
    return pl.pallas_call(
        kern,
        out_shape=out_shape,
        grid=(bsz, nt),
        in_specs=in_specs,
        out_specs=out_specs,
        scratch_shapes=scratch,
        compiler_params=pltpu.CompilerParams(
            dimension_semantics=("arbitrary", "arbitrary"),
            vmem_limit_bytes=VMEM_LIMIT),
        name="prompt_layer",
    )(x, ada3, gnorm, wmain, wglr, convw, convb, wgate, bgx, bga, lam, wg2, bg2, ggla, wout, gfin)


def _sample_in_kernel(x_ref, shift_ref, scale_ref, h0_ref, c0_ref, c1_ref, c2_ref,
                      gnorm_ref, wmain_ref, wglr_ref, convw_ref, convb_ref, wgate_ref,
                      bgx_ref, bga_ref, lam_ref, wg2_ref, bg2_ref,
                      hnew_ref, xa_ref, ylru_ref, qs_ref, k_ref, eg_ref, v_ref, zg_ref):
    x = x_ref[...]
    hn = (_rms(x) * gnorm_ref[...]) * (1.0 + scale_ref[...]) + shift_ref[...]
    hnb = hn.astype(_BF)
    xa = _dot(hnb, wmain_ref[:, 0:D])
    za = _dot(hnb, wmain_ref[:, D:2 * D])
    qs_ref[...] = _dot(hnb, wmain_ref[:, 2 * D:2 * D + D_QK]) * (DK ** -0.5)
    k_ref[...] = _dot(hnb, wmain_ref[:, 2 * D + D_QK:3 * D])
    v_ref[...] = _dot(hnb, wmain_ref[:, 3 * D:4 * D])
    zg_ref[...] = _dot(hnb, wmain_ref[:, 4 * D:5 * D])
    glr = _dot(hnb, wglr_ref[...])
    la = _log_sigmoid(_dot(glr.astype(_BF), wg2_ref[...]) + bg2_ref[...]) / TAU
    eg_ref[...] = jnp.exp(la)
    xa_ref[...] = xa

    xc = convb_ref[...]
    for j, cj in enumerate((c0_ref[...], c1_ref[...], c2_ref[...], xa)):
        xc = xc + cj * convw_ref[j:j + 1, :]
    xcb = xc.astype(_BF)
    logsig = _log_sigmoid(lam_ref[...])
    for n in range(LRU_BLOCKS):
        ln = slice(n * LRU_BW, (n + 1) * LRU_BW)
        pre = _dot(xcb[:, ln], wgate_ref[n])
        a, u = _lru_block(xc[:, ln], None, pre, bgx_ref[:, ln], bga_ref[:, ln],
                          logsig[:, ln], None)
        h = a * h0_ref[:, ln] + u
        hnew_ref[:, ln] = h
        ylru_ref[:, ln] = h * _silu(za[:, ln])


def _sample_in(x, shift, scale, h0, c0, c1, c2, gnorm, wmain, wglr, convw, convb, wgate,
               bgx, bga, lam, wg2, bg2):
    n = x.shape[0]
    f = lambda w: jax.ShapeDtypeStruct((n, w), _F32)
    return pl.pallas_call(
        _sample_in_kernel,
        out_shape=(f(D), f(D), f(D), f(D_QK), f(D_QK), f(D_QK), f(D), f(D)),
        compiler_params=pltpu.CompilerParams(vmem_limit_bytes=VMEM_LIMIT),
        name="sample_in",
    )(x, shift, scale, h0, c0, c1, c2, gnorm, wmain, wglr, convw, convb, wgate, bgx, bga,
      lam, wg2, bg2)


def _sample_gla_kernel(qs_ref, k_ref, eg_ref, v_ref, s_ref, o_ref, snew_ref, *, bs):
    eye = (lax.broadcasted_iota(jnp.int32, (DK, DK), 0)
           == lax.broadcasted_iota(jnp.int32, (DK, DK), 1))

    def to_col(row):
        return jnp.sum(jnp.where(eye, jnp.broadcast_to(row, (DK, DK)), 0.0),
                       axis=-1, keepdims=True)

    for j in range(bs):
        for h in range(HEADS):
            lk = slice(h * DK, (h + 1) * DK)
            lv = slice(h * DV, (h + 1) * DV)
            qs = qs_ref[j:j + 1, lk]
            k = k_ref[j:j + 1, lk]
            eg = eg_ref[j:j + 1, lk]
            v = v_ref[j:j + 1, lv]
            s0 = s_ref[j, h]
            qk = jnp.sum(qs * k, axis=-1, keepdims=True)
            qg = jnp.broadcast_to(qs * eg, (SUBLANES, DK)).astype(_BF)
            o_inter = _dot(qg, s0.astype(_BF))[0:1]
            o_ref[j:j + 1, lv] = o_inter + qk * v
            snew_ref[j, h] = to_col(eg) * s0 + to_col(k) * v


def _sample_gla(qs, k, eg, v, s0, bs=8):
    n = qs.shape[0]
    kern = functools.partial(_sample_gla_kernel, bs=bs)
    return pl.pallas_call(
        kern,
        out_shape=(jax.ShapeDtypeStruct((n, D), _F32),
                   jax.ShapeDtypeStruct(s0.shape, _F32)),
        grid=(n // bs,),
        in_specs=[
            pl.BlockSpec((bs, D_QK), lambda i: (i, 0)),
            pl.BlockSpec((bs, D_QK), lambda i: (i, 0)),
            pl.BlockSpec((bs, D_QK), lambda i: (i, 0)),
            pl.BlockSpec((bs, D), lambda i: (i, 0)),
            pl.BlockSpec((bs, HEADS, DK, DV), lambda i: (i, 0, 0, 0)),
        ],
        out_specs=(
            pl.BlockSpec((bs, D), lambda i: (i, 0)),
            pl.BlockSpec((bs, HEADS, DK, DV), lambda i: (i, 0, 0, 0)),
        ),
        compiler_params=pltpu.CompilerParams(
            dimension_semantics=("arbitrary",), vmem_limit_bytes=VMEM_LIMIT),
        name="sample_gla",
    )(qs, k, eg, v, s0)


def _sample_out_kernel(x_ref, gate_ref, o_ref, zg_ref, ylru_ref, ggla_ref, wout_ref, gfin_ref,
                       y_ref):
    parts = [ylru_ref[...].astype(_BF)]
    for h in range(HEADS):
        lv = slice(h * DV, (h + 1) * DV)
        o = _rms(o_ref[:, lv]) * ggla_ref[:, lv]
        parts.append((o * _silu(zg_ref[:, lv])).astype(_BF))
    mix = jnp.concatenate(parts, axis=-1)
    y = x_ref[...] + gate_ref[...] * _dot(mix, wout_ref[...])
    y_ref[...] = _rms(y) * gfin_ref[...]


def _sample_out(x, gate, o, zg, ylru, ggla, wout, gfin):
    return pl.pallas_call(
        _sample_out_kernel,
        out_shape=jax.ShapeDtypeStruct(x.shape, _F32),
        compiler_params=pltpu.CompilerParams(vmem_limit_bytes=VMEM_LIMIT),
        name="sample_out",
    )(x, gate, o, zg, ylru, ggla, wout, gfin)


def kernel(x_prompt, x_sample, state_lru_h, state_lru_conv, state_gla, c_prompt, c_sample,
           g_norm, w_ada, b_ada, w_in, conv_w, conv_b, w_gate_x, b_gate_x, w_gate_a, b_gate_a,
           lru_lambda, w_gla_g2, b_gla_g2, g_gla_norm, w_out, g_final):
    depth = g_norm.shape[0]
    assert depth == 1, "single-layer trunk"
    bp = x_prompt.shape[0]
    l = 0

    w = w_in[l]
    o_glr = 4 * D
    wmain = jnp.concatenate([w[:, :o_glr], w[:, o_glr + RANK:]], axis=1).astype(_BF)
    wglr = jnp.pad(w[:, o_glr:o_glr + RANK], ((0, 0), (0, RANK_PAD - RANK))).astype(_BF)
    wg2 = jnp.pad(w_gla_g2[l], ((0, RANK_PAD - RANK), (0, 0))).astype(_BF)
    wgate = jnp.concatenate([w_gate_x[l], w_gate_a[l]], axis=-1).astype(_BF)
    wout = w_out[l].astype(_BF)
    wada = w_ada[l].astype(_BF)
    row = lambda a: a.reshape(1, -1)
    gnorm, convb = row(g_norm[l]), row(conv_b[l])
    bgx, bga, lam = row(b_gate_x[l]), row(b_gate_a[l]), row(lru_lambda[l])
    bg2, ggla, gfin = row(b_gla_g2[l]), row(g_gla_norm[l]), row(g_final)
    convw = conv_w[l]

    ada = _ada(jnp.concatenate([c_prompt, c_sample], axis=0), wada, row(b_ada[l]))
    ada_p = ada[:bp].reshape(bp, 3, D)
    ada_s = ada[bp:]

    yp, hp, cp, sp = _prompt(x_prompt, ada_p, gnorm, wmain, wglr, convw, convb, wgate, bgx, bga,
                             lam, wg2, bg2, ggla, wout, gfin)

    xs = x_sample[:, 0, :]
    conv0 = state_lru_conv[l]
    hs, xa, ylru, qs, ks, eg, vs, zg = _sample_in(
        xs, ada_s[:, :D], ada_s[:, D:2 * D], state_lru_h[l],
        conv0[:, 0], conv0[:, 1], conv0[:, 2],
        gnorm, wmain, wglr, convw, convb, wgate, bgx, bga, lam, wg2, bg2)
    o, ss = _sample_gla(qs, ks, eg, vs, state_gla[l])
    ys = _sample_out(xs, ada_s[:, 2 * D:], o, zg, ylru, ggla, wout, gfin)
    cs = jnp.stack([conv0[:, 1], conv0[:, 2], xa], axis=1)

    return (yp, ys[:, None, :], hp.reshape(1, bp, D), cp[None], sp[None],
            hs[None], cs[None], ss[None])
```

```python
import functools

import jax
import jax.numpy as jnp
from jax import lax
from jax.experimental import pallas as pl
from jax.experimental.pallas import tpu as pltpu

EPS = 1e-6
D = 1024
LRU_BLOCKS = 8
LRU_BW = D // LRU_BLOCKS
LRU_C = 8.0
CONV_W = 4
HEADS = 4
DV = D // HEADS
DK = DV // 2
RANK = 16
RANK_PAD = 128
TAU = 16.0
CHUNK = 64
D_QK = HEADS * DK
D_MIX = 2 * D
O_GLR = 4 * D
N_GROUPS = 5

SUBLANES = 8
TM = 256
STAGE_ROWS = 256
VMEM_LIMIT = 56 * 1024 * 1024
NEG_LOG2E = -1.4426950408889634

_BF = jnp.bfloat16
_F32 = jnp.float32

_NT = (((1,), (1,)), ((), ()))


def _sigmoid(x):
    return 1.0 / (1.0 + jnp.exp2(x * NEG_LOG2E))


def _silu(x):
    return x * _sigmoid(x)


def _log_sigmoid(x):
    return jnp.minimum(x, 0.0) - jnp.log1p(jnp.exp(-jnp.abs(x)))


def _rms(x):
    return x * lax.rsqrt(jnp.mean(x * x, axis=-1, keepdims=True) + EPS)


def _dot(a, b):
    return jnp.dot(a, b, preferred_element_type=_F32)


def _group_scan(a3, u3):
    row = lax.broadcasted_iota(jnp.int32, a3.shape, 1)
    for s in (1, 2, 4):
        a_sh = pltpu.roll(a3, s, axis=1)
        u_sh = pltpu.roll(u3, s, axis=1)
        m = row >= s
        u3 = jnp.where(m, a3 * u_sh + u3, u3)
        a3 = jnp.where(m, a3 * a_sh, a3)
    return a3, u3


def _group_cumsum(x3):
    row = lax.broadcasted_iota(jnp.int32, x3.shape, 1)
    for s in (1, 2, 4):
        x3 = jnp.where(row >= s, x3 + pltpu.roll(x3, s, axis=1), x3)
    return x3


def _linear_scan(a, u, h0):
    r, l = a.shape
    g = r // SUBLANES
    a3, u3 = _group_scan(a.reshape(g, SUBLANES, l), u.reshape(g, SUBLANES, l))
    carry = h0
    out = []
    for i in range(g):
        hg = a3[i] * carry + u3[i]
        out.append(hg)
        carry = hg[SUBLANES - 1:SUBLANES]
    return jnp.concatenate(out, axis=0)


def _chunk_cumsum(x):
    r, l = x.shape
    g = r // SUBLANES
    per_chunk = CHUNK // SUBLANES
    x3 = _group_cumsum(x.reshape(g, SUBLANES, l))
    out = []
    carry = None
    for i in range(g):
        xg = x3[i] if i % per_chunk == 0 else x3[i] + carry
        out.append(xg)
        carry = xg[SUBLANES - 1:SUBLANES]
    return jnp.concatenate(out, axis=0)


def _lru_block(xc_n, pre, bgx, bga, logsig):
    gx = _sigmoid(pre[:, :LRU_BW] + bgx)
    ga = _sigmoid(pre[:, LRU_BW:] + bga)
    log_a = (LRU_C * ga) * logsig
    a = jnp.exp(log_a)
    om = 1.0 - a * a
    mult = jnp.where(om > 0.0, om * lax.rsqrt(om), 0.0)
    u = mult * (gx * xc_n)
    return a, u


def _ada_kernel(c_ref, w_ref, b_ref, o_ref):
    c = c_ref[...]
    o_ref[...] = _dot(_silu(c).astype(_BF), w_ref[...].astype(_BF)) + b_ref[...]


def _ada(c_all, w_ada, b_ada):
    n = c_all.shape[0]
    return pl.pallas_call(
        _ada_kernel,
        out_shape=jax.ShapeDtypeStruct((n, 3 * D), _F32),
        grid=(3,),
        in_specs=[
            pl.BlockSpec((n, D), lambda j: (0, 0)),
            pl.BlockSpec((D, D), lambda j: (0, j)),
            pl.BlockSpec((1, D), lambda j: (0, j)),
        ],
        out_specs=pl.BlockSpec((n, D), lambda j: (0, j)),
        compiler_params=pltpu.CompilerParams(dimension_semantics=("arbitrary",)),
        name="ada",
    )(c_all, w_ada, b_ada)


def _stage_blocks(n, src_block, store_block, stage, sem):
    def copy(i, slot):
        return pltpu.make_async_copy(src_block(i), stage.at[slot], sem.at[slot])

    copy(0, 0).start()

    def body(i, carry):
        slot = lax.rem(i, 2)

        @pl.when(i + 1 < n)
        def _():
            copy(i + 1, 1 - slot).start()

        copy(i, slot).wait()
        store_block(i, stage[slot].astype(_BF))
        return carry

    lax.fori_loop(0, n, body, 0)


def _prompt_kernel(x_ref, ada_ref, gnorm_ref, win_hbm, wzg_hbm, wout_hbm, wglr_ref, convw_ref,
                   convb_ref, wgate_ref, bgx_ref, bga_ref, lam_ref, wg2_ref, bg2_ref, ggla_ref,
                   gfin_ref,
                   y_ref, hlast_ref, convout_ref, sout_ref,
                   wmain_s, wout_s, wglr_s, wg2_s, wgate_s, stage, sem,
                   xa_s, za_s, q_s, k_s, v_s, zg_s, b_s, mix_s, h_s, st_s, *, tm):
    bi = pl.program_id(0)
    t = pl.program_id(1)
    nt = pl.num_programs(1)
    blocks_per_group = D // STAGE_ROWS

    @pl.when((bi == 0) & (t == 0))
    def _():
        def rows(i):
            return pl.ds(pl.multiple_of(lax.rem(i, blocks_per_group) * STAGE_ROWS, STAGE_ROWS),
                         STAGE_ROWS)

        def win_block(i):
            col = pl.multiple_of((i // blocks_per_group) * D, D)
            return win_hbm.at[rows(i), pl.ds(col, D)]

        def store_main(i, val):
            wmain_s[i // blocks_per_group, rows(i), :] = val

        def store_zg(i, val):
            wmain_s[N_GROUPS - 1, rows(i), :] = val

        def out_rows(i):
            return pl.ds(pl.multiple_of(i * STAGE_ROWS, STAGE_ROWS), STAGE_ROWS)

        def store_out(i, val):
            wout_s[out_rows(i), :] = val

        _stage_blocks((N_GROUPS - 1) * blocks_per_group, win_block, store_main, stage, sem)
        _stage_blocks(blocks_per_group, lambda i: wzg_hbm.at[rows(i), :], store_zg, stage, sem)
        _stage_blocks(D_MIX // STAGE_ROWS, lambda i: wout_hbm.at[out_rows(i), :], store_out,
                      stage, sem)
        wglr_s[...] = wglr_ref[...].astype(_BF)
        wg2_s[...] = wg2_ref[...].astype(_BF)
        wgate_s[...] = wgate_ref[...].astype(_BF)

    @pl.when(t == 0)
    def _():
        xa_s[0:SUBLANES, :] = jnp.zeros((SUBLANES, D), _F32)
        h_s[...] = jnp.zeros_like(h_s)
        st_s[...] = jnp.zeros_like(st_s)

    x = x_ref[0]
    shift = ada_ref[0, 0:1, :]
    scale = ada_ref[0, 1:2, :]
    gate = ada_ref[0, 2:3, :]
    hn = (_rms(x) * gnorm_ref[...]) * (1.0 + scale) + shift
    hnb = hn.astype(_BF)

    xa_s[SUBLANES:SUBLANES + tm, :] = _dot(hnb, wmain_s[0])
    za_s[...] = _dot(hnb, wmain_s[1])
    q_s[...] = _dot(hnb, wmain_s[2, :, 0:D_QK])
    k_s[...] = _dot(hnb, wmain_s[2, :, D_QK:D])
    v_s[...] = _dot(hnb, wmain_s[3])
    zg_s[...] = _dot(hnb, wmain_s[4])
    glr = _dot(hnb, wglr_s[...])
    la = _log_sigmoid(_dot(glr.astype(_BF), wg2_s[...]) + bg2_ref[...]) / TAU
    b_s[...] = _chunk_cumsum(la)

    xc = convb_ref[...]
    for j in range(CONV_W):
        off = SUBLANES - (CONV_W - 1) + j
        xc = xc + xa_s[off:off + tm, :] * convw_ref[j:j + 1, :]
    xcb = xc.astype(_BF)
    logsig = _log_sigmoid(lam_ref[...])
    for n in range(LRU_BLOCKS):
        ln = slice(n * LRU_BW, (n + 1) * LRU_BW)
        pre = _dot(xcb[:, ln], wgate_s[n])
        a, u = _lru_block(xc[:, ln], pre, bgx_ref[:, ln], bga_ref[:, ln], logsig[:, ln])
        h = _linear_scan(a, u, h_s[:, ln])
        h_s[:, ln] = h[tm - 1:tm]
        mix_s[:, ln] = (h * _silu(za_s[:, ln])).astype(_BF)

    @pl.when(t == nt - 1)
    def _():
        hlast_ref[0] = h_s[...]
        convout_ref[0] = xa_s[SUBLANES + tm - (CONV_W - 1):SUBLANES + tm, :]

    xa_s[0:SUBLANES, :] = xa_s[tm:tm + SUBLANES, :]

    tril = (lax.broadcasted_iota(jnp.int32, (CHUNK, CHUNK), 0)
            >= lax.broadcasted_iota(jnp.int32, (CHUNK, CHUNK), 1))

    def chunk_body(c, carry):
        rows = pl.ds(pl.multiple_of(c * CHUNK, CHUNK), CHUNK)
        for h in range(HEADS):
            lk = slice(h * DK, (h + 1) * DK)
            lv = slice(h * DV, (h + 1) * DV)
            b = b_s[rows, lk]
            bl = b[CHUNK - 1:CHUNK]
            q = q_s[rows, lk] * (DK ** -0.5)
            k = k_s[rows, lk]
            v = v_s[rows, lv]
            qi = (q * jnp.exp(b)).astype(_BF)
            ki = (k * jnp.exp(-b)).astype(_BF)
            kd = (k * jnp.exp(bl - b)).astype(_BF)
            vb = v.astype(_BF)
            att = lax.dot_general(qi, ki, _NT, preferred_element_type=_F32)
            att = jnp.where(tril, att, 0.0).astype(_BF)
            st = st_s[h]
            o = _dot(att, vb) + lax.dot_general(qi, st.astype(_BF), _NT,
                                                preferred_element_type=_F32)
            st_s[h] = st * jnp.exp(bl) + _dot(v.T.astype(_BF), kd)
            o = _rms(o) * ggla_ref[:, lv]
            mix_s[rows, D + h * DV:D + (h + 1) * DV] = (o * _silu(zg_s[rows, lv])).astype(_BF)
        return carry

    lax.fori_loop(0, tm // CHUNK, chunk_body, 0)

    @pl.when(t == nt - 1)
    def _():
        for h in range(HEADS):
            sout_ref[0, h] = st_s[h].T

    y = x + gate * _dot(mix_s[...], wout_s[...])
    y_ref[0] = _rms(y) * gfin_ref[...]


def _const_spec(shape):
    nd = len(shape)
    return pl.BlockSpec(shape, lambda b, t: (0,) * nd, pipeline_mode=pl.Buffered(1))


def _prompt(x, ada3, gnorm, w_in2d, w_zg, w_out2d, wglr, convw, convb, wgate, bgx, bga, lam,
            wg2, bg2, ggla, gfin, tm=TM):
    bsz, seq, _ = x.shape
    nt = seq // tm
    kern = functools.partial(_prompt_kernel, tm=tm)
    out_shape = (
        jax.ShapeDtypeStruct((bsz, seq, D), _F32),
        jax.ShapeDtypeStruct((bsz, 1, D), _F32),
        jax.ShapeDtypeStruct((bsz, CONV_W - 1, D), _F32),
        jax.ShapeDtypeStruct((bsz, HEADS, DK, DV), _F32),
    )
    hbm = pl.BlockSpec(memory_space=pl.ANY)
    in_specs = [
        pl.BlockSpec((1, tm, D), lambda b, t: (b, t, 0)),
        pl.BlockSpec((1, 3, D), lambda b, t: (b, 0, 0)),
        _const_spec((1, D)),
        hbm, hbm, hbm,
        _const_spec((D, RANK_PAD)),
        _const_spec((CONV_W, D)),
        _const_spec((1, D)),
        _const_spec((LRU_BLOCKS, LRU_BW, 2 * LRU_BW)),
        _const_spec((1, D)),
        _const_spec((1, D)),
        _const_spec((1, D)),
        _const_spec((RANK_PAD, D_QK)),
        _const_spec((1, D_QK)),
        _const_spec((1, D)),
        _const_spec((1, D)),
    ]
    out_specs = (
        pl.BlockSpec((1, tm, D), lambda b, t: (b, t, 0)),
        pl.BlockSpec((1, 1, D), lambda b, t: (b, 0, 0)),
        pl.BlockSpec((1, CONV_W - 1, D), lambda b, t: (b, 0, 0)),
        pl.BlockSpec((1, HEADS, DK, DV), lambda b, t: (b, 0, 0, 0)),
    )
    scratch = [
        pltpu.VMEM((N_GROUPS, D, D), _BF),
        pltpu.VMEM((D_MIX, D), _BF),
        pltpu.VMEM((D, RANK_PAD), _BF),
        pltpu.VMEM((RANK_PAD, D_QK), _BF),
        pltpu.VMEM((LRU_BLOCKS, LRU_BW, 2 * LRU_BW), _BF),
        pltpu.VMEM((2, STAGE_ROWS, D), _F32),
        pltpu.SemaphoreType.DMA((2,)),
        pltpu.VMEM((tm + SUBLANES, D), _F32),
        pltpu.VMEM((tm, D), _F32),
        pltpu.VMEM((tm, D_QK), _F32),
        pltpu.VMEM((tm, D_QK), _F32),
        pltpu.VMEM((tm, D), _F32),
        pltpu.VMEM((tm, D), _F32),
        pltpu.VMEM((tm, D_QK), _F32),
        pltpu.VMEM((tm, D_MIX), _BF),
        pltpu.VMEM((1, D), _F32),
        pltpu.VMEM((HEADS, DV, DK), _F32),
    ]
    return pl.pallas_call(
        kern,
        out_shape=out_shape,
        grid=(bsz, nt),
        in_specs=in_specs,
        out_specs=out_specs,
        scratch_shapes=scratch,
        compiler_params=pltpu.CompilerParams(
            dimension_semantics=("arbitrary", "arbitrary"),
            vmem_limit_bytes=VMEM_LIMIT),
        name="prompt_layer",
    )(x, ada3, gnorm, w_in2d, w_zg, w_out2d, wglr, convw, convb, wgate, bgx, bga, lam, wg2, bg2,
      ggla, gfin)


def _sample_in_kernel(x_ref, shift_ref, scale_ref, h0_ref, c0_ref, c1_ref, c2_ref,
                      gnorm_ref, win_ref, wzg_ref, wglr_ref, convw_ref, convb_ref, wgate_ref,
                      bgx_ref, bga_ref, lam_ref, wg2_ref, bg2_ref,
                      hnew_ref, xa_ref, ylru_ref, qs_ref, k_ref, eg_ref, v_ref, zg_ref):
    x = x_ref[...]
    hn = (_rms(x) * gnorm_ref[...]) * (1.0 + scale_ref[...]) + shift_ref[...]
    hnb = hn.astype(_BF)

    def proj(c0, width):
        return _dot(hnb, win_ref[:, c0:c0 + width].astype(_BF))

    xa = proj(0, D)
    za = proj(D, D)
    qs_ref[...] = proj(2 * D, D_QK) * (DK ** -0.5)
    k_ref[...] = proj(2 * D + D_QK, D_QK)
    v_ref[...] = proj(3 * D, D)
    zg_ref[...] = _dot(hnb, wzg_ref[...].astype(_BF))
    glr = _dot(hnb, wglr_ref[...].astype(_BF))
    la = _log_sigmoid(_dot(glr.astype(_BF), wg2_ref[...].astype(_BF)) + bg2_ref[...]) / TAU
    eg_ref[...] = jnp.exp(la)
    xa_ref[...] = xa

    xc = convb_ref[...]
    for j, cj in enumerate((c0_ref[...], c1_ref[...], c2_ref[...], xa)):
        xc = xc + cj * convw_ref[j:j + 1, :]
    xcb = xc.astype(_BF)
    logsig = _log_sigmoid(lam_ref[...])
    for n in range(LRU_BLOCKS):
        ln = slice(n * LRU_BW, (n + 1) * LRU_BW)
        pre = _dot(xcb[:, ln], wgate_ref[n].astype(_BF))
        a, u = _lru_block(xc[:, ln], pre, bgx_ref[:, ln], bga_ref[:, ln], logsig[:, ln])
        h = a * h0_ref[:, ln] + u
        hnew_ref[:, ln] = h
        ylru_ref[:, ln] = h * _silu(za[:, ln])


def _sample_in(x, shift, scale, h0, c0, c1, c2, gnorm, w_in2d, w_zg, wglr, convw, convb, wgate,
               bgx, bga, lam, wg2, bg2):
    n = x.shape[0]
    f = lambda w: jax.ShapeDtypeStruct((n, w), _F32)
    return pl.pallas_call(
        _sample_in_kernel,
        out_shape=(f(D), f(D), f(D), f(D_QK), f(D_QK), f(D_QK), f(D), f(D)),
        compiler_params=pltpu.CompilerParams(vmem_limit_bytes=VMEM_LIMIT),
        name="sample_in",
    )(x, shift, scale, h0, c0, c1, c2, gnorm, w_in2d, w_zg, wglr, convw, convb, wgate, bgx, bga,
      lam, wg2, bg2)


def _sample_gla_kernel(qs_ref, k_ref, eg_ref, v_ref, s_ref, o_ref, snew_ref, *, bs):
    eye = (lax.broadcasted_iota(jnp.int32, (DK, DK), 0)
           == lax.broadcasted_iota(jnp.int32, (DK, DK), 1))

    def to_col(row):
        return jnp.sum(jnp.where(eye, jnp.broadcast_to(row, (DK, DK)), 0.0),
                       axis=-1, keepdims=True)

    for j in range(bs):
        for h in range(HEADS):
            lk = slice(h * DK, (h + 1) * DK)
            lv = slice(h * DV, (h + 1) * DV)
            qs = qs_ref[j:j + 1, lk]
            k = k_ref[j:j + 1, lk]
            eg = eg_ref[j:j + 1, lk]
            v = v_ref[j:j + 1, lv]
            s0 = s_ref[j, h]
            qk = jnp.sum(qs * k, axis=-1, keepdims=True)
            qg = jnp.broadcast_to(qs * eg, (SUBLANES, DK)).astype(_BF)
            o_inter = _dot(qg, s0.astype(_BF))[0:1]
            o_ref[j:j + 1, lv] = o_inter + qk * v
            snew_ref[j, h] = to_col(eg) * s0 + to_col(k) * v


def _sample_gla(qs, k, eg, v, s0, bs=8):
    n = qs.shape[0]
    kern = functools.partial(_sample_gla_kernel, bs=bs)
    return pl.pallas_call(
        kern,
        out_shape=(jax.ShapeDtypeStruct((n, D), _F32),
                   jax.ShapeDtypeStruct(s0.shape, _F32)),
        grid=(n // bs,),
        in_specs=[
            pl.BlockSpec((bs, D_QK), lambda i: (i, 0)),
            pl.BlockSpec((bs, D_QK), lambda i: (i, 0)),
            pl.BlockSpec((bs, D_QK), lambda i: (i, 0)),
            pl.BlockSpec((bs, D), lambda i: (i, 0)),
            pl.BlockSpec((bs, HEADS, DK, DV), lambda i: (i, 0, 0, 0)),
        ],
        out_specs=(
            pl.BlockSpec((bs, D), lambda i: (i, 0)),
            pl.BlockSpec((bs, HEADS, DK, DV), lambda i: (i, 0, 0, 0)),
        ),
        compiler_params=pltpu.CompilerParams(
            dimension_semantics=("arbitrary",), vmem_limit_bytes=VMEM_LIMIT),
        name="sample_gla",
    )(qs, k, eg, v, s0)


def _sample_out_kernel(x_ref, gate_ref, o_ref, zg_ref, ylru_ref, ggla_ref, wout_ref, gfin_ref,
                       y_ref):
    parts = [ylru_ref[...].astype(_BF)]
    for h in range(HEADS):
        lv = slice(h * DV, (h + 1) * DV)
        o = _rms(o_ref[:, lv]) * ggla_ref[:, lv]
        parts.append((o * _silu(zg_ref[:, lv])).astype(_BF))
    mix = jnp.concatenate(parts, axis=-1)
    y = x_ref[...] + gate_ref[...] * _dot(mix, wout_ref[...].astype(_BF))
    y_ref[...] = _rms(y) * gfin_ref[...]


def _sample_out(x, gate, o, zg, ylru, ggla, w_out2d, gfin):
    return pl.pallas_call(
        _sample_out_kernel,
        out_shape=jax.ShapeDtypeStruct(x.shape, _F32),
        compiler_params=pltpu.CompilerParams(vmem_limit_bytes=VMEM_LIMIT),
        name="sample_out",
    )(x, gate, o, zg, ylru, ggla, w_out2d, gfin)


def kernel(x_prompt, x_sample, state_lru_h, state_lru_conv, state_gla, c_prompt, c_sample,
           g_norm, w_ada, b_ada, w_in, conv_w, conv_b, w_gate_x, b_gate_x, w_gate_a, b_gate_a,
           lru_lambda, w_gla_g2, b_gla_g2, g_gla_norm, w_out, g_final):
    depth = g_norm.shape[0]
    assert depth == 1, "single-layer trunk"
    bp = x_prompt.shape[0]
    ns = x_sample.shape[0]

    w_in2d = w_in.reshape(D, w_in.shape[-1])
    w_out2d = w_out.reshape(D_MIX, D)
    w_zg = w_in2d[:, O_GLR + RANK:]
    wglr = jnp.pad(w_in2d[:, O_GLR:O_GLR + RANK], ((0, 0), (0, RANK_PAD - RANK)))
    wg2 = jnp.pad(w_gla_g2.reshape(RANK, D_QK), ((0, RANK_PAD - RANK), (0, 0)))
    wgate = jnp.concatenate([w_gate_x.reshape(LRU_BLOCKS, LRU_BW, LRU_BW),
                             w_gate_a.reshape(LRU_BLOCKS, LRU_BW, LRU_BW)], axis=-1)
    row = lambda a: a.reshape(1, -1)
    gnorm, convb = row(g_norm), row(conv_b)
    bgx, bga, lam = row(b_gate_x), row(b_gate_a), row(lru_lambda)
    bg2, ggla, gfin = row(b_gla_g2), row(g_gla_norm), row(g_final)
    convw = conv_w.reshape(CONV_W, D)

    ada = _ada(jnp.concatenate([c_prompt, c_sample], axis=0), w_ada.reshape(D, 3 * D),
               row(b_ada))
    ada_p = ada[:bp].reshape(bp, 3, D)
    ada_s = ada[bp:]

    yp, hp, cp, sp = _prompt(x_prompt, ada_p, gnorm, w_in2d, w_zg, w_out2d, wglr, convw, convb,
                             wgate, bgx, bga, lam, wg2, bg2, ggla, gfin)

    xs = x_sample.reshape(ns, D)
    conv0 = state_lru_conv.reshape(ns, CONV_W - 1, D)
    hs, xa, ylru, qs, ks, eg, vs, zg = _sample_in(
        xs, ada_s[:, :D], ada_s[:, D:2 * D], state_lru_h.reshape(ns, D),
        conv0[:, 0], conv0[:, 1], conv0[:, 2],
        gnorm, w_in2d, w_zg, wglr, convw, convb, wgate, bgx, bga, lam, wg2, bg2)
    o, ss = _sample_gla(qs, ks, eg, vs, state_gla.reshape(ns, HEADS, DK, DV))
    ys = _sample_out(xs, ada_s[:, 2 * D:], o, zg, ylru, ggla, w_out2d, gfin)
    cs = jnp.stack([conv0[:, 1], conv0[:, 2], xa], axis=1)

    return (yp, ys.reshape(ns, 1, D), hp.reshape(1, bp, D), cp[None], sp[None],
            hs[None], cs[None], ss[None])
```

```python
import functools

import jax
import jax.numpy as jnp
from jax import lax
from jax.experimental import pallas as pl
from jax.experimental.pallas import tpu as pltpu

EPS = 1e-6
D = 1024
LRU_BLOCKS = 8
LRU_BW = D // LRU_BLOCKS
LRU_C = 8.0
CONV_W = 4
HEADS = 4
DV = D // HEADS
DK = DV // 2
RANK = 16
RANK_PAD = 128
TAU = 16.0
CHUNK = 64
D_QK = HEADS * DK
D_MIX = 2 * D
O_GLR = 4 * D
N_GROUPS = 5

SUBLANES = 8
TM = 256
STAGE_ROWS = 256
VMEM_LIMIT = 56 * 1024 * 1024
NEG_LOG2E = -1.4426950408889634

_BF = jnp.bfloat16
_F32 = jnp.float32

_NT = (((1,), (1,)), ((), ()))


def _sigmoid(x):
    return 1.0 / (1.0 + jnp.exp2(x * NEG_LOG2E))


def _silu(x):
    return x * _sigmoid(x)


def _log_sigmoid(x):
    return jnp.minimum(x, 0.0) - jnp.log1p(jnp.exp(-jnp.abs(x)))


def _rms(x):
    return x * lax.rsqrt(jnp.mean(x * x, axis=-1, keepdims=True) + EPS)


def _dot(a, b):
    return jnp.dot(a, b, preferred_element_type=_F32)


def _group_scan(a3, u3):
    row = lax.broadcasted_iota(jnp.int32, a3.shape, 1)
    for s in (1, 2, 4):
        a_sh = pltpu.roll(a3, s, axis=1)
        u_sh = pltpu.roll(u3, s, axis=1)
        m = row >= s
        u3 = jnp.where(m, a3 * u_sh + u3, u3)
        a3 = jnp.where(m, a3 * a_sh, a3)
    return a3, u3


def _group_cumsum(x3):
    row = lax.broadcasted_iota(jnp.int32, x3.shape, 1)
    for s in (1, 2, 4):
        x3 = jnp.where(row >= s, x3 + pltpu.roll(x3, s, axis=1), x3)
    return x3


def _linear_scan(a, u, h0):
    r, l = a.shape
    g = r // SUBLANES
    a3, u3 = _group_scan(a.reshape(g, SUBLANES, l), u.reshape(g, SUBLANES, l))
    carry = h0
    out = []
    for i in range(g):
        hg = a3[i] * carry + u3[i]
        out.append(hg)
        carry = hg[SUBLANES - 1:SUBLANES]
    return jnp.concatenate(out, axis=0)


def _chunk_cumsum(x):
    r, l = x.shape
    g = r // SUBLANES
    per_chunk = CHUNK // SUBLANES
    x3 = _group_cumsum(x.reshape(g, SUBLANES, l))
    out = []
    carry = None
    for i in range(g):
        xg = x3[i] if i % per_chunk == 0 else x3[i] + carry
        out.append(xg)
        carry = xg[SUBLANES - 1:SUBLANES]
    return jnp.concatenate(out, axis=0)


def _lru_block(xc_n, pre, bgx, bga, logsig):
    gx = _sigmoid(pre[:, :LRU_BW] + bgx)
    ga = _sigmoid(pre[:, LRU_BW:] + bga)
    log_a = (LRU_C * ga) * logsig
    a = jnp.exp(log_a)
    om = 1.0 - a * a
    mult = jnp.where(om > 0.0, om * lax.rsqrt(om), 0.0)
    u = mult * (gx * xc_n)
    return a, u


def _ada_kernel(c_ref, w_ref, b_ref, o_ref):
    c = c_ref[...]
    o_ref[...] = _dot(_silu(c).astype(_BF), w_ref[...].astype(_BF)) + b_ref[...]


def _ada(c_all, w_ada, b_ada):
    n = c_all.shape[0]
    return pl.pallas_call(
        _ada_kernel,
        out_shape=jax.ShapeDtypeStruct((n, 3 * D), _F32),
        grid=(3,),
        in_specs=[
            pl.BlockSpec((n, D), lambda j: (0, 0)),
            pl.BlockSpec((D, D), lambda j: (0, j)),
            pl.BlockSpec((1, D), lambda j: (0, j)),
        ],
        out_specs=pl.BlockSpec((n, D), lambda j: (0, j)),
        compiler_params=pltpu.CompilerParams(dimension_semantics=("arbitrary",)),
        name="ada",
    )(c_all, w_ada, b_ada)


def _stage_blocks(n, src_block, store_block, stage, sem):
    def copy(i, slot):
        return pltpu.make_async_copy(src_block(i), stage.at[slot], sem.at[slot])

    copy(0, 0).start()

    def body(i, carry):
        slot = lax.rem(i, 2)

        @pl.when(i + 1 < n)
        def _():
            copy(i + 1, 1 - slot).start()

        copy(i, slot).wait()
        store_block(i, stage[slot].astype(_BF))
        return carry

    lax.fori_loop(0, n, body, 0)


def _prompt_kernel(x_ref, ada_ref, gnorm_ref, win_hbm, wzg_hbm, wout_hbm, wglr_ref, convw_ref,
                   convb_ref, wgate_ref, bgx_ref, bga_ref, lam_ref, wg2_ref, bg2_ref, ggla_ref,
                   gfin_ref,
                   y_ref, hlast_ref, convout_ref, sout_ref,
                   wmain_s, wout_s, wglr_s, wg2_s, wgate_s, stage, sem,
                   xa_s, za_s, q_s, k_s, v_s, zg_s, b_s, mix_s, h_s, st_s, *, tm):
    bi = pl.program_id(0)
    t = pl.program_id(1)
    nt = pl.num_programs(1)
    blocks_per_group = D // STAGE_ROWS

    @pl.when((bi == 0) & (t == 0))
    def _():
        def rows(i):
            return pl.ds(pl.multiple_of(lax.rem(i, blocks_per_group) * STAGE_ROWS, STAGE_ROWS),
                         STAGE_ROWS)

        def win_block(i):
            col = pl.multiple_of((i // blocks_per_group) * D, D)
            return win_hbm.at[rows(i), pl.ds(col, D)]

        def store_main(i, val):
            wmain_s[i // blocks_per_group, rows(i), :] = val

        def store_zg(i, val):
            wmain_s[N_GROUPS - 1, rows(i), :] = val

        def out_rows(i):
            return pl.ds(pl.multiple_of(i * STAGE_ROWS, STAGE_ROWS), STAGE_ROWS)

        def store_out(i, val):
            wout_s[out_rows(i), :] = val

        _stage_blocks((N_GROUPS - 1) * blocks_per_group, win_block, store_main, stage, sem)
        _stage_blocks(blocks_per_group, lambda i: wzg_hbm.at[rows(i), :], store_zg, stage, sem)
        _stage_blocks(D_MIX // STAGE_ROWS, lambda i: wout_hbm.at[out_rows(i), :], store_out,
                      stage, sem)
        wglr_s[...] = wglr_ref[...].astype(_BF)
        wg2_s[...] = wg2_ref[...].astype(_BF)
        wgate_s[...] = wgate_ref[...].astype(_BF)

    @pl.when(t == 0)
    def _():
        xa_s[0:SUBLANES, :] = jnp.zeros((SUBLANES, D), _F32)
        h_s[...] = jnp.zeros_like(h_s)
        st_s[...] = jnp.zeros_like(st_s)

    x = x_ref[0]
    shift = ada_ref[0, 0:1, :]
    scale = ada_ref[0, 1:2, :]
    gate = ada_ref[0, 2:3, :]
    hn = (_rms(x) * gnorm_ref[...]) * (1.0 + scale) + shift
    hnb = hn.astype(_BF)

    xa_s[SUBLANES:SUBLANES + tm, :] = _dot(hnb, wmain_s[0])
    za_s[...] = _dot(hnb, wmain_s[1])
    q_s[...] = _dot(hnb, wmain_s[2, :, 0:D_QK])
    k_s[...] = _dot(hnb, wmain_s[2, :, D_QK:D])
    v_s[...] = _dot(hnb, wmain_s[3])
    zg_s[...] = _dot(hnb, wmain_s[4])
    glr = _dot(hnb, wglr_s[...])
    la = _log_sigmoid(_dot(glr.astype(_BF), wg2_s[...]) + bg2_ref[...]) / TAU
    b_s[...] = _chunk_cumsum(la)

    xc = convb_ref[...]
    for j in range(CONV_W):
        off = SUBLANES - (CONV_W - 1) + j
        xc = xc + xa_s[off:off + tm, :] * convw_ref[j:j + 1, :]
    xcb = xc.astype(_BF)
    logsig = _log_sigmoid(lam_ref[...])
    for n in range(LRU_BLOCKS):
        ln = slice(n * LRU_BW, (n + 1) * LRU_BW)
        pre = _dot(xcb[:, ln], wgate_s[n])
        a, u = _lru_block(xc[:, ln], pre, bgx_ref[:, ln], bga_ref[:, ln], logsig[:, ln])
        h = _linear_scan(a, u, h_s[:, ln])
        h_s[:, ln] = h[tm - 1:tm]
        mix_s[:, ln] = (h * _silu(za_s[:, ln])).astype(_BF)

    xa_s[0:SUBLANES, :] = xa_s[tm:tm + SUBLANES, :]

    tril = (lax.broadcasted_iota(jnp.int32, (CHUNK, CHUNK), 0)
            >= lax.broadcasted_iota(jnp.int32, (CHUNK, CHUNK), 1))

    def chunk_body(c, carry):
        rows = slice(c * CHUNK, (c + 1) * CHUNK)
        for h in range(HEADS):
            lk = slice(h * DK, (h + 1) * DK)
            lv = slice(h * DV, (h + 1) * DV)
            b = b_s[rows, lk]
            bl = b[CHUNK - 1:CHUNK]
            q = q_s[rows, lk] * (DK ** -0.5)
            k = k_s[rows, lk]
            v = v_s[rows, lv]
            qi = (q * jnp.exp(b)).astype(_BF)
            ki = (k * jnp.exp(-b)).astype(_BF)
            kd = (k * jnp.exp(bl - b)).astype(_BF)
            vb = v.astype(_BF)
            att = lax.dot_general(qi, ki, _NT, preferred_element_type=_F32)
            att = jnp.where(tril, att, 0.0).astype(_BF)
            st = st_s[h]
            o = _dot(att, vb) + lax.dot_general(qi, st.astype(_BF), _NT,
                                                preferred_element_type=_F32)
            st_s[h] = st * jnp.exp(bl) + _dot(v.T.astype(_BF), kd)
            o = _rms(o) * ggla_ref[:, lv]
            mix_s[rows, D + h * DV:D + (h + 1) * DV] = (o * _silu(zg_s[rows, lv])).astype(_BF)
        return carry

    for c in range(tm // CHUNK):
        chunk_body(c, 0)

    y = x + gate * _dot(mix_s[...], wout_s[...])
    y_ref[0] = _rms(y) * gfin_ref[...]

    @pl.when(t == nt - 1)
    def _():
        hlast_ref[0] = h_s[...]
        convout_ref[0] = xa_s[SUBLANES + tm - (CONV_W - 1):SUBLANES + tm, :]
        for h in range(HEADS):
            sout_ref[0, h] = st_s[h].T


def _const_spec(shape):
    nd = len(shape)
    return pl.BlockSpec(shape, lambda b, t: (0,) * nd, pipeline_mode=pl.Buffered(1))


def _prompt(x, ada3, gnorm, w_in2d, w_zg, w_out2d, wglr, convw, convb, wgate, bgx, bga, lam,
            wg2, bg2, ggla, gfin, tm=TM):
    bsz, seq, _ = x.shape
    nt = seq // tm
    kern = functools.partial(_prompt_kernel, tm=tm)
    out_shape = (
        jax.ShapeDtypeStruct((bsz, seq, D), _F32),
        jax.ShapeDtypeStruct((bsz, 1, D), _F32),
        jax.ShapeDtypeStruct((bsz, CONV_W - 1, D), _F32),
        jax.ShapeDtypeStruct((bsz, HEADS, DK, DV), _F32),
    )
    hbm = pl.BlockSpec(memory_space=pl.ANY)
    in_specs = [
        pl.BlockSpec((1, tm, D), lambda b, t: (b, t, 0)),
        pl.BlockSpec((1, 3, D), lambda b, t: (b, 0, 0)),
        _const_spec((1, D)),
        hbm, hbm, hbm,
        _const_spec((D, RANK_PAD)),
        _const_spec((CONV_W, D)),
        _const_spec((1, D)),
        _const_spec((LRU_BLOCKS, LRU_BW, 2 * LRU_BW)),
        _const_spec((1, D)),
        _const_spec((1, D)),
        _const_spec((1, D)),
        _const_spec((RANK_PAD, D_QK)),
        _const_spec((1, D_QK)),
        _const_spec((1, D)),
        _const_spec((1, D)),
    ]
    out_specs = (
        pl.BlockSpec((1, tm, D), lambda b, t: (b, t, 0)),
        pl.BlockSpec((1, 1, D), lambda b, t: (b, 0, 0)),
        pl.BlockSpec((1, CONV_W - 1, D), lambda b, t: (b, 0, 0)),
        pl.BlockSpec((1, HEADS, DK, DV), lambda b, t: (b, 0, 0, 0)),
    )
    scratch = [
        pltpu.VMEM((N_GROUPS, D, D), _BF),
        pltpu.VMEM((D_MIX, D), _BF),
        pltpu.VMEM((D, RANK_PAD), _BF),
        pltpu.VMEM((RANK_PAD, D_QK), _BF),
        pltpu.VMEM((LRU_BLOCKS, LRU_BW, 2 * LRU_BW), _BF),
        pltpu.VMEM((2, STAGE_ROWS, D), _F32),
        pltpu.SemaphoreType.DMA((2,)),
        pltpu.VMEM((tm + SUBLANES, D), _F32),
        pltpu.VMEM((tm, D), _F32),
        pltpu.VMEM((tm, D_QK), _F32),
        pltpu.VMEM((tm, D_QK), _F32),
        pltpu.VMEM((tm, D), _F32),
        pltpu.VMEM((tm, D), _F32),
        pltpu.VMEM((tm, D_QK), _F32),
        pltpu.VMEM((tm, D_MIX), _BF),
        pltpu.VMEM((1, D), _F32),
        pltpu.VMEM((HEADS, DV, DK), _F32),
    ]
    return pl.pallas_call(
        kern,
        out_shape=out_shape,
        grid=(bsz, nt),
        in_specs=in_specs,
        out_specs=out_specs,
        scratch_shapes=scratch,
        compiler_params=pltpu.CompilerParams(
            dimension_semantics=("arbitrary", "arbitrary"),
            vmem_limit_bytes=VMEM_LIMIT),
        name="prompt_layer",
    )(x, ada3, gnorm, w_in2d, w_zg, w_out2d, wglr, convw, convb, wgate, bgx, bga, lam, wg2, bg2,
      ggla, gfin)


def _sample_in_kernel(x_ref, shift_ref, scale_ref, h0_ref, c0_ref, c1_ref, c2_ref,
                      gnorm_ref, win_ref, wzg_ref, wglr_ref, convw_ref, convb_ref, wgate_ref,
                      bgx_ref, bga_ref, lam_ref, wg2_ref, bg2_ref,
                      hnew_ref, xa_ref, ylru_ref, qs_ref, k_ref, eg_ref, v_ref, zg_ref):
    x = x_ref[...]
    hn = (_rms(x) * gnorm_ref[...]) * (1.0 + scale_ref[...]) + shift_ref[...]
    hnb = hn.astype(_BF)

    def proj(c0, width):
        return _dot(hnb, win_ref[:, c0:c0 + width].astype(_BF))

    xa = proj(0, D)
    za = proj(D, D)
    qs_ref[...] = proj(2 * D, D_QK) * (DK ** -0.5)
    k_ref[...] = proj(2 * D + D_QK, D_QK)
    v_ref[...] = proj(3 * D, D)
    zg_ref[...] = _dot(hnb, wzg_ref[...].astype(_BF))
    glr = _dot(hnb, wglr_ref[...].astype(_BF))
    la = _log_sigmoid(_dot(glr.astype(_BF), wg2_ref[...].astype(_BF)) + bg2_ref[...]) / TAU
    eg_ref[...] = jnp.exp(la)
    xa_ref[...] = xa

    xc = convb_ref[...]
    for j, cj in enumerate((c0_ref[...], c1_ref[...], c2_ref[...], xa)):
        xc = xc + cj * convw_ref[j:j + 1, :]
    xcb = xc.astype(_BF)
    logsig = _log_sigmoid(lam_ref[...])
    for n in range(LRU_BLOCKS):
        ln = slice(n * LRU_BW, (n + 1) * LRU_BW)
        pre = _dot(xcb[:, ln], wgate_ref[n].astype(_BF))
        a, u = _lru_block(xc[:, ln], pre, bgx_ref[:, ln], bga_ref[:, ln], logsig[:, ln])
        h = a * h0_ref[:, ln] + u
        hnew_ref[:, ln] = h
        ylru_ref[:, ln] = h * _silu(za[:, ln])


def _sample_in(x, shift, scale, h0, c0, c1, c2, gnorm, w_in2d, w_zg, wglr, convw, convb, wgate,
               bgx, bga, lam, wg2, bg2):
    n = x.shape[0]
    f = lambda w: jax.ShapeDtypeStruct((n, w), _F32)
    return pl.pallas_call(
        _sample_in_kernel,
        out_shape=(f(D), f(D), f(D), f(D_QK), f(D_QK), f(D_QK), f(D), f(D)),
        compiler_params=pltpu.CompilerParams(vmem_limit_bytes=VMEM_LIMIT),
        name="sample_in",
    )(x, shift, scale, h0, c0, c1, c2, gnorm, w_in2d, w_zg, wglr, convw, convb, wgate, bgx, bga,
      lam, wg2, bg2)


def _sample_gla_kernel(qs_ref, k_ref, eg_ref, v_ref, s_ref, o_ref, snew_ref, *, bs):
    eye = (lax.broadcasted_iota(jnp.int32, (DK, DK), 0)
           == lax.broadcasted_iota(jnp.int32, (DK, DK), 1))

    def to_col(row):
        return jnp.sum(jnp.where(eye, jnp.broadcast_to(row, (DK, DK)), 0.0),
                       axis=-1, keepdims=True)

    for j in range(bs):
        for h in range(HEADS):
            lk = slice(h * DK, (h + 1) * DK)
            lv = slice(h * DV, (h + 1) * DV)
            qs = qs_ref[j:j + 1, lk]
            k = k_ref[j:j + 1, lk]
            eg = eg_ref[j:j + 1, lk]
            v = v_ref[j:j + 1, lv]
            s0 = s_ref[j, h]
            qk = jnp.sum(qs * k, axis=-1, keepdims=True)
            qg = jnp.broadcast_to(qs * eg, (SUBLANES, DK)).astype(_BF)
            o_inter = _dot(qg, s0.astype(_BF))[0:1]
            o_ref[j:j + 1, lv] = o_inter + qk * v
            snew_ref[j, h] = to_col(eg) * s0 + to_col(k) * v


def _sample_gla(qs, k, eg, v, s0, bs=8):
    n = qs.shape[0]
    kern = functools.partial(_sample_gla_kernel, bs=bs)
    return pl.pallas_call(
        kern,
        out_shape=(jax.ShapeDtypeStruct((n, D), _F32),
                   jax.ShapeDtypeStruct(s0.shape, _F32)),
        grid=(n // bs,),
        in_specs=[
            pl.BlockSpec((bs, D_QK), lambda i: (i, 0)),
            pl.BlockSpec((bs, D_QK), lambda i: (i, 0)),
            pl.BlockSpec((bs, D_QK), lambda i: (i, 0)),
            pl.BlockSpec((bs, D), lambda i: (i, 0)),
            pl.BlockSpec((bs, HEADS, DK, DV), lambda i: (i, 0, 0, 0)),
        ],
        out_specs=(
            pl.BlockSpec((bs, D), lambda i: (i, 0)),
            pl.BlockSpec((bs, HEADS, DK, DV), lambda i: (i, 0, 0, 0)),
        ),
        compiler_params=pltpu.CompilerParams(
            dimension_semantics=("arbitrary",), vmem_limit_bytes=VMEM_LIMIT),
        name="sample_gla",
    )(qs, k, eg, v, s0)


def _sample_out_kernel(x_ref, gate_ref, o_ref, zg_ref, ylru_ref, ggla_ref, wout_ref, gfin_ref,
                       y_ref):
    parts = [ylru_ref[...].astype(_BF)]
    for h in range(HEADS):
        lv = slice(h * DV, (h + 1) * DV)
        o = _rms(o_ref[:, lv]) * ggla_ref[:, lv]
        parts.append((o * _silu(zg_ref[:, lv])).astype(_BF))
    mix = jnp.concatenate(parts, axis=-1)
    y = x_ref[...] + gate_ref[...] * _dot(mix, wout_ref[...].astype(_BF))
    y_ref[...] = _rms(y) * gfin_ref[...]


def _sample_out(x, gate, o, zg, ylru, ggla, w_out2d, gfin):
    return pl.pallas_call(
        _sample_out_kernel,
        out_shape=jax.ShapeDtypeStruct(x.shape, _F32),
        compiler_params=pltpu.CompilerParams(vmem_limit_bytes=VMEM_LIMIT),
        name="sample_out",
    )(x, gate, o, zg, ylru, ggla, w_out2d, gfin)


def kernel(x_prompt, x_sample, state_lru_h, state_lru_conv, state_gla, c_prompt, c_sample,
           g_norm, w_ada, b_ada, w_in, conv_w, conv_b, w_gate_x, b_gate_x, w_gate_a, b_gate_a,
           lru_lambda, w_gla_g2, b_gla_g2, g_gla_norm, w_out, g_final):
    depth = g_norm.shape[0]
    assert depth == 1, "single-layer trunk"
    bp = x_prompt.shape[0]
    ns = x_sample.shape[0]

    w_in2d = w_in.reshape(D, w_in.shape[-1])
    w_out2d = w_out.reshape(D_MIX, D)
    w_zg = w_in2d[:, O_GLR + RANK:]
    wglr = jnp.pad(w_in2d[:, O_GLR:O_GLR + RANK], ((0, 0), (0, RANK_PAD - RANK)))
    wg2 = jnp.pad(w_gla_g2.reshape(RANK, D_QK), ((0, RANK_PAD - RANK), (0, 0)))
    wgate = jnp.concatenate([w_gate_x.reshape(LRU_BLOCKS, LRU_BW, LRU_BW),
                             w_gate_a.reshape(LRU_BLOCKS, LRU_BW, LRU_BW)], axis=-1)
    row = lambda a: a.reshape(1, -1)
    gnorm, convb = row(g_norm), row(conv_b)
    bgx, bga, lam = row(b_gate_x), row(b_gate_a), row(lru_lambda)
    bg2, ggla, gfin = row(b_gla_g2), row(g_gla_norm), row(g_final)
    convw = conv_w.reshape(CONV_W, D)

    ada = _ada(jnp.concatenate([c_prompt, c_sample], axis=0), w_ada.reshape(D, 3 * D),
               row(b_ada))
    ada_p = ada[:bp].reshape(bp, 3, D)
    ada_s = ada[bp:]

    yp, hp, cp, sp = _prompt(x_prompt, ada_p, gnorm, w_in2d, w_zg, w_out2d, wglr, convw, convb,
                             wgate, bgx, bga, lam, wg2, bg2, ggla, gfin)

    xs = x_sample.reshape(ns, D)
    conv0 = state_lru_conv.reshape(ns, CONV_W - 1, D)
    hs, xa, ylru, qs, ks, eg, vs, zg = _sample_in(
        xs, ada_s[:, :D], ada_s[:, D:2 * D], state_lru_h.reshape(ns, D),
        conv0[:, 0], conv0[:, 1], conv0[:, 2],
        gnorm, w_in2d, w_zg, wglr, convw, convb, wgate, bgx, bga, lam, wg2, bg2)
    o, ss = _sample_gla(qs, ks, eg, vs, state_gla.reshape(ns, HEADS, DK, DV))
    ys = _sample_out(xs, ada_s[:, 2 * D:], o, zg, ylru, ggla, w_out2d, gfin)
    cs = jnp.stack([conv0[:, 1], conv0[:, 2], xa], axis=1)

    return (yp, ys.reshape(ns, 1, D), hp.reshape(1, bp, D), cp[None], sp[None],
            hs[None], cs[None], ss[None])
```

```python
import functools

import jax
import jax.numpy as jnp
from jax import lax
from jax.experimental import pallas as pl
from jax.experimental.pallas import tpu as pltpu

EPS = 1e-6
D = 1024
LRU_BLOCKS = 8
LRU_BW = D // LRU_BLOCKS
LRU_C = 8.0
CONV_W = 4
HEADS = 4
DV = D // HEADS
DK = DV // 2
RANK = 16
RANK_PAD = 128
TAU = 16.0
CHUNK = 64
D_QK = HEADS * DK
D_MIX = 2 * D
O_GLR = 4 * D
N_GROUPS = 5

SUBLANES = 8
TM = 256
STAGE_ROWS = 256
VMEM_LIMIT = 56 * 1024 * 1024
NEG_LOG2E = -1.4426950408889634

_BF = jnp.bfloat16
_F32 = jnp.float32

_NT = (((1,), (1,)), ((), ()))


def _sigmoid(x):
    return 1.0 / (1.0 + jnp.exp2(x * NEG_LOG2E))


def _silu(x):
    return x * _sigmoid(x)


def _log_sigmoid(x):
    return jnp.minimum(x, 0.0) - jnp.log1p(jnp.exp(-jnp.abs(x)))


def _rms(x):
    return x * lax.rsqrt(jnp.mean(x * x, axis=-1, keepdims=True) + EPS)


def _dot(a, b):
    return jnp.dot(a, b, preferred_element_type=_F32)


def _group_scan(a3, u3):
    row = lax.broadcasted_iota(jnp.int32, a3.shape, 1)
    for s in (1, 2, 4):
        a_sh = pltpu.roll(a3, s, axis=1)
        u_sh = pltpu.roll(u3, s, axis=1)
        m = row >= s
        u3 = jnp.where(m, a3 * u_sh + u3, u3)
        a3 = jnp.where(m, a3 * a_sh, a3)
    return a3, u3


def _group_cumsum(x3):
    row = lax.broadcasted_iota(jnp.int32, x3.shape, 1)
    for s in (1, 2, 4):
        x3 = jnp.where(row >= s, x3 + pltpu.roll(x3, s, axis=1), x3)
    return x3


def _linear_scan(a, u, h0):
    r, l = a.shape
    g = r // SUBLANES
    a3, u3 = _group_scan(a.reshape(g, SUBLANES, l), u.reshape(g, SUBLANES, l))
    carry = h0
    out = []
    for i in range(g):
        hg = a3[i] * carry + u3[i]
        out.append(hg)
        carry = hg[SUBLANES - 1:SUBLANES]
    return jnp.concatenate(out, axis=0)


def _chunk_cumsum(x):
    r, l = x.shape
    g = r // SUBLANES
    per_chunk = CHUNK // SUBLANES
    x3 = _group_cumsum(x.reshape(g, SUBLANES, l))
    out = []
    carry = None
    for i in range(g):
        xg = x3[i] if i % per_chunk == 0 else x3[i] + carry
        out.append(xg)
        carry = xg[SUBLANES - 1:SUBLANES]
    return jnp.concatenate(out, axis=0)


def _lru_block(xc_n, pre, bgx, bga, logsig):
    gx = _sigmoid(pre[:, :LRU_BW] + bgx)
    ga = _sigmoid(pre[:, LRU_BW:] + bga)
    log_a = (LRU_C * ga) * logsig
    a = jnp.exp(log_a)
    om = 1.0 - a * a
    mult = jnp.where(om > 0.0, om * lax.rsqrt(om), 0.0)
    u = mult * (gx * xc_n)
    return a, u


def _ada_kernel(c_ref, w_ref, b_ref, o_ref):
    c = c_ref[...]
    o_ref[...] = _dot(_silu(c).astype(_BF), w_ref[...].astype(_BF)) + b_ref[...]


def _ada(c_all, w_ada, b_ada):
    n = c_all.shape[0]
    return pl.pallas_call(
        _ada_kernel,
        out_shape=jax.ShapeDtypeStruct((n, 3 * D), _F32),
        grid=(3,),
        in_specs=[
            pl.BlockSpec((n, D), lambda j: (0, 0)),
            pl.BlockSpec((D, D), lambda j: (0, j)),
            pl.BlockSpec((1, D), lambda j: (0, j)),
        ],
        out_specs=pl.BlockSpec((n, D), lambda j: (0, j)),
        compiler_params=pltpu.CompilerParams(dimension_semantics=("arbitrary",)),
        name="ada",
    )(c_all, w_ada, b_ada)


def _stage_blocks(n, src_block, store_block, stage, sem):
    def copy(i, slot):
        return pltpu.make_async_copy(src_block(i), stage.at[slot], sem.at[slot])

    copy(0, 0).start()

    def body(i, carry):
        slot = lax.rem(i, 2)

        @pl.when(i + 1 < n)
        def _():
            copy(i + 1, 1 - slot).start()

        copy(i, slot).wait()
        store_block(i, stage[slot].astype(_BF))
        return carry

    lax.fori_loop(0, n, body, 0)


def _prompt_kernel(xn_ref, xp_ref, adan_ref, adap_ref, gnorm_ref, win_hbm, wzg_hbm, wout_hbm,
                   wglr_ref, convw_ref, convb_ref, wgate_ref, bgx_ref, bga_ref, lam_ref, wg2_ref,
                   bg2_ref, ggla_ref, gfin_ref,
                   y_ref, hlast_ref, convout_ref, sout_ref,
                   wmain_s, wout_s, wglr_s, wg2_s, wgate_s, stage, sem,
                   hn_s, xa_s, za_s, xc_s, pre_s, q_s, k_s, v_s, zg_s, b_s, mix_s, yacc_s, ctail_s,
                   h_s, st_s, *, tm, nt, n_tiles):
    i = pl.program_id(0)
    t = lax.rem(jnp.minimum(i, n_tiles - 1), nt)
    t_next = lax.rem(jnp.minimum(i + 1, n_tiles - 1), nt)
    slot = lax.rem(i, 2)
    blocks_per_group = D // STAGE_ROWS
    half = D // 2
    quarter = D // 4

    def front_norm(x_ref, ada_ref):
        x = x_ref[0]
        hn = (_rms(x) * gnorm_ref[...]) * (1.0 + ada_ref[0, 1:2, :]) + ada_ref[0, 0:1, :]
        hn_s[...] = hn.astype(_BF)

    def front_xa():
        xa_s[SUBLANES:SUBLANES + tm, :] = _dot(hn_s[...], wmain_s[0])

    def front_za():
        za_s[...] = _dot(hn_s[...], wmain_s[1])

    def front_conv():
        xc = convb_ref[...]
        for j in range(CONV_W):
            off = SUBLANES - (CONV_W - 1) + j
            xc = xc + xa_s[off:off + tm, :] * convw_ref[j:j + 1, :]
        xc_s[...] = xc

    def front_gates():
        for n in range(LRU_BLOCKS):
            pre_s[:, 2 * n * LRU_BW:2 * (n + 1) * LRU_BW] = _dot(
                xc_s[:, n * LRU_BW:(n + 1) * LRU_BW].astype(_BF), wgate_s[n])

    @pl.when(i == 0)
    def _():
        def rows(j):
            return pl.ds(pl.multiple_of(lax.rem(j, blocks_per_group) * STAGE_ROWS, STAGE_ROWS),
                         STAGE_ROWS)

        def win_block(j):
            col = pl.multiple_of((j // blocks_per_group) * D, D)
            return win_hbm.at[rows(j), pl.ds(col, D)]

        def store_main(j, val):
            wmain_s[j // blocks_per_group, rows(j), :] = val

        def store_zg(j, val):
            wmain_s[N_GROUPS - 1, rows(j), :] = val

        def out_rows(j):
            return pl.ds(pl.multiple_of(j * STAGE_ROWS, STAGE_ROWS), STAGE_ROWS)

        def store_out(j, val):
            wout_s[out_rows(j), :] = val

        _stage_blocks((N_GROUPS - 1) * blocks_per_group, win_block, store_main, stage, sem)
        _stage_blocks(blocks_per_group, lambda j: wzg_hbm.at[rows(j), :], store_zg, stage, sem)
        _stage_blocks(D_MIX // STAGE_ROWS, lambda j: wout_hbm.at[out_rows(j), :], store_out,
                      stage, sem)
        wglr_s[...] = wglr_ref[...].astype(_BF)
        wg2_s[...] = wg2_ref[...].astype(_BF)
        wgate_s[...] = wgate_ref[...].astype(_BF)
        mix_s[...] = jnp.zeros_like(mix_s)
        xa_s[0:SUBLANES, :] = jnp.zeros((SUBLANES, D), _F32)
        front_norm(xp_ref, adap_ref)
        front_xa()
        front_za()
        front_conv()
        front_gates()

    @pl.when(t == 0)
    def _():
        h_s[...] = jnp.zeros_like(h_s)
        st_s[...] = jnp.zeros_like(st_s)

    def out_slab(j):
        def run():
            cols = slice(j * quarter, (j + 1) * quarter)
            o = _dot(mix_s[1 - slot], wout_s[:, cols])
            yacc_s[:, cols] = xp_ref[0, :, cols] + adap_ref[0, 2:3, cols] * o
        return run

    def out_norm():
        y_ref[0] = _rms(yacc_s[...]) * gfin_ref[...]

    def proj(dst, dst_cols, group, w_cols):
        def run():
            dst[:, dst_cols] = _dot(hn_s[...], wmain_s[group, :, w_cols])
        return run

    def decay():
        glr = _dot(hn_s[...], wglr_s[...])
        la = _log_sigmoid(_dot(glr.astype(_BF), wg2_s[...]) + bg2_ref[...]) / TAU
        b_s[...] = _chunk_cumsum(la)

    lo, hi, full = slice(0, half), slice(half, D), slice(0, D_QK)
    tasks = [
        (out_slab(0), proj(q_s, full, 2, lo)),
        (out_slab(1), proj(k_s, full, 2, hi)),
        (out_slab(2), decay),
        (out_slab(3), out_norm),
        (proj(v_s, lo, 3, lo),),
        (proj(v_s, hi, 3, hi),),
        (proj(zg_s, lo, 4, lo),),
        (proj(zg_s, hi, 4, hi),),
    ]

    logsig = _log_sigmoid(lam_ref[...])
    for n in range(LRU_BLOCKS):
        ln = slice(n * LRU_BW, (n + 1) * LRU_BW)
        a, u = _lru_block(xc_s[:, ln], pre_s[:, 2 * n * LRU_BW:2 * (n + 1) * LRU_BW],
                          bgx_ref[:, ln], bga_ref[:, ln], logsig[:, ln])
        h = _linear_scan(a, u, h_s[:, ln])
        h_s[:, ln] = h[tm - 1:tm]
        mix_s[slot, :, ln] = (h * _silu(za_s[:, ln])).astype(_BF)
        for task in tasks[n]:
            task()

    tril = (lax.broadcasted_iota(jnp.int32, (CHUNK, CHUNK), 0)
            >= lax.broadcasted_iota(jnp.int32, (CHUNK, CHUNK), 1))

    def gla_chunk(c):
        rows = slice(c * CHUNK, (c + 1) * CHUNK)
        for h in range(HEADS):
            lk = slice(h * DK, (h + 1) * DK)
            lv = slice(h * DV, (h + 1) * DV)
            b = b_s[rows, lk]
            bl = b[CHUNK - 1:CHUNK]
            q = q_s[rows, lk] * (DK ** -0.5)
            k = k_s[rows, lk]
            v = v_s[rows, lv]
            qi = (q * jnp.exp(b)).astype(_BF)
            ki = (k * jnp.exp(-b)).astype(_BF)
            kd = (k * jnp.exp(bl - b)).astype(_BF)
            vb = v.astype(_BF)
            att = lax.dot_general(qi, ki, _NT, preferred_element_type=_F32)
            att = jnp.where(tril, att, 0.0).astype(_BF)
            st = st_s[h]
            o = _dot(att, vb) + lax.dot_general(qi, st.astype(_BF), _NT,
                                                preferred_element_type=_F32)
            st_s[h] = st * jnp.exp(bl) + _dot(v.T.astype(_BF), kd)
            o = _rms(o) * ggla_ref[:, lv]
            mix_s[slot, rows, D + h * DV:D + (h + 1) * DV] = (
                o * _silu(zg_s[rows, lv])).astype(_BF)

    def next_tail():
        tail = xa_s[tm:tm + SUBLANES, :]
        ctail_s[...] = tail
        xa_s[0:SUBLANES, :] = jnp.where(t_next != 0, tail, 0.0)

    next_tail()
    front_norm(xn_ref, adan_ref)
    gla_chunk(0)
    front_xa()
    gla_chunk(1)
    front_za()
    front_conv()
    gla_chunk(2)
    front_gates()
    gla_chunk(3)

    @pl.when((t == nt - 1) & (i < n_tiles))
    def _():
        hlast_ref[0] = h_s[...]
        convout_ref[0] = ctail_s[SUBLANES - (CONV_W - 1):SUBLANES, :]
        for h in range(HEADS):
            sout_ref[0, h] = st_s[h].T


def _const_spec(shape):
    nd = len(shape)
    return pl.BlockSpec(shape, lambda i: (0,) * nd, pipeline_mode=pl.Buffered(1))


def _prompt(x, ada3, gnorm, w_in2d, w_zg, w_out2d, wglr, convw, convb, wgate, bgx, bga, lam,
            wg2, bg2, ggla, gfin, tm=TM):
    bsz, seq, _ = x.shape
    nt = seq // tm
    n_tiles = bsz * nt
    assert tm // CHUNK == 4, "the step body spreads the next tile's input side over 4 GLA chunks"
    kern = functools.partial(_prompt_kernel, tm=tm, nt=nt, n_tiles=n_tiles)
    out_shape = (
        jax.ShapeDtypeStruct((bsz, seq, D), _F32),
        jax.ShapeDtypeStruct((bsz, 1, D), _F32),
        jax.ShapeDtypeStruct((bsz, CONV_W - 1, D), _F32),
        jax.ShapeDtypeStruct((bsz, HEADS, DK, DV), _F32),
    )
    cur = lambda i: jnp.minimum(i, n_tiles - 1)
    prev = lambda i: jnp.maximum(i - 1, 0)
    nxt = lambda i: jnp.minimum(i + 1, n_tiles - 1)
    tile = lambda f: (lambda i: (f(i) // nt, lax.rem(f(i), nt), 0))
    seq_of = lambda f: (lambda i: (f(i) // nt, 0, 0))
    hbm = pl.BlockSpec(memory_space=pl.ANY)
    in_specs = [
        pl.BlockSpec((1, tm, D), tile(nxt)),
        pl.BlockSpec((1, tm, D), tile(prev)),
        pl.BlockSpec((1, 3, D), seq_of(nxt)),
        pl.BlockSpec((1, 3, D), seq_of(prev)),
        _const_spec((1, D)),
        hbm, hbm, hbm,
        _const_spec((D, RANK_PAD)),
        _const_spec((CONV_W, D)),
        _const_spec((1, D)),
        _const_spec((LRU_BLOCKS, LRU_BW, 2 * LRU_BW)),
        _const_spec((1, D)),
        _const_spec((1, D)),
        _const_spec((1, D)),
        _const_spec((RANK_PAD, D_QK)),
        _const_spec((1, D_QK)),
        _const_spec((1, D)),
        _const_spec((1, D)),
    ]
    out_specs = (
        pl.BlockSpec((1, tm, D), tile(prev)),
        pl.BlockSpec((1, 1, D), seq_of(cur)),
        pl.BlockSpec((1, CONV_W - 1, D), seq_of(cur)),
        pl.BlockSpec((1, HEADS, DK, DV), lambda i: (cur(i) // nt, 0, 0, 0)),
    )
    scratch = [
        pltpu.VMEM((N_GROUPS, D, D), _BF),
        pltpu.VMEM((D_MIX, D), _BF),
        pltpu.VMEM((D, RANK_PAD), _BF),
        pltpu.VMEM((RANK_PAD, D_QK), _BF),
        pltpu.VMEM((LRU_BLOCKS, LRU_BW, 2 * LRU_BW), _BF),
        pltpu.VMEM((2, STAGE_ROWS, D), _F32),
        pltpu.SemaphoreType.DMA((2,)),
        pltpu.VMEM((tm, D), _BF),
        pltpu.VMEM((tm + SUBLANES, D), _F32),
        pltpu.VMEM((tm, D), _F32),
        pltpu.VMEM((tm, D), _F32),
        pltpu.VMEM((tm, 2 * D), _F32),
        pltpu.VMEM((tm, D_QK), _F32),
        pltpu.VMEM((tm, D_QK), _F32),
        pltpu.VMEM((tm, D), _F32),
        pltpu.VMEM((tm, D), _F32),
        pltpu.VMEM((tm, D_QK), _F32),
        pltpu.VMEM((2, tm, D_MIX), _BF),
        pltpu.VMEM((tm, D), _F32),
        pltpu.VMEM((SUBLANES, D), _F32),
        pltpu.VMEM((1, D), _F32),
        pltpu.VMEM((HEADS, DV, DK), _F32),
    ]
    return pl.pallas_call(
        kern,
        out_shape=out_shape,
        grid=(n_tiles + 1,),
        in_specs=in_specs,
        out_specs=out_specs,
        scratch_shapes=scratch,
        compiler_params=pltpu.CompilerParams(
            dimension_semantics=("arbitrary",),
            vmem_limit_bytes=VMEM_LIMIT),
        name="prompt_layer",
    )(x, x, ada3, ada3, gnorm, w_in2d, w_zg, w_out2d, wglr, convw, convb, wgate, bgx, bga, lam,
      wg2, bg2, ggla, gfin)


def _sample_in_kernel(x_ref, shift_ref, scale_ref, h0_ref, c0_ref, c1_ref, c2_ref,
                      gnorm_ref, win_ref, wzg_ref, wglr_ref, convw_ref, convb_ref, wgate_ref,
                      bgx_ref, bga_ref, lam_ref, wg2_ref, bg2_ref,
                      hnew_ref, xa_ref, ylru_ref, qs_ref, k_ref, eg_ref, v_ref, zg_ref):
    x = x_ref[...]
    hn = (_rms(x) * gnorm_ref[...]) * (1.0 + scale_ref[...]) + shift_ref[...]
    hnb = hn.astype(_BF)

    def proj(c0, width):
        return _dot(hnb, win_ref[:, c0:c0 + width].astype(_BF))

    xa = proj(0, D)
    za = proj(D, D)
    qs_ref[...] = proj(2 * D, D_QK) * (DK ** -0.5)
    k_ref[...] = proj(2 * D + D_QK, D_QK)
    v_ref[...] = proj(3 * D, D)
    zg_ref[...] = _dot(hnb, wzg_ref[...].astype(_BF))
    glr = _dot(hnb, wglr_ref[...].astype(_BF))
    la = _log_sigmoid(_dot(glr.astype(_BF), wg2_ref[...].astype(_BF)) + bg2_ref[...]) / TAU
    eg_ref[...] = jnp.exp(la)
    xa_ref[...] = xa

    xc = convb_ref[...]
    for j, cj in enumerate((c0_ref[...], c1_ref[...], c2_ref[...], xa)):
        xc = xc + cj * convw_ref[j:j + 1, :]
    xcb = xc.astype(_BF)
    logsig = _log_sigmoid(lam_ref[...])
    for n in range(LRU_BLOCKS):
        ln = slice(n * LRU_BW, (n + 1) * LRU_BW)
        pre = _dot(xcb[:, ln], wgate_ref[n].astype(_BF))
        a, u = _lru_block(xc[:, ln], pre, bgx_ref[:, ln], bga_ref[:, ln], logsig[:, ln])
        h = a * h0_ref[:, ln] + u
        hnew_ref[:, ln] = h
        ylru_ref[:, ln] = h * _silu(za[:, ln])


def _sample_in(x, shift, scale, h0, c0, c1, c2, gnorm, w_in2d, w_zg, wglr, convw, convb, wgate,
               bgx, bga, lam, wg2, bg2):
    n = x.shape[0]
    f = lambda w: jax.ShapeDtypeStruct((n, w), _F32)
    return pl.pallas_call(
        _sample_in_kernel,
        out_shape=(f(D), f(D), f(D), f(D_QK), f(D_QK), f(D_QK), f(D), f(D)),
        compiler_params=pltpu.CompilerParams(vmem_limit_bytes=VMEM_LIMIT),
        name="sample_in",
    )(x, shift, scale, h0, c0, c1, c2, gnorm, w_in2d, w_zg, wglr, convw, convb, wgate, bgx, bga,
      lam, wg2, bg2)


def _sample_gla_kernel(qs_ref, k_ref, eg_ref, v_ref, s_ref, o_ref, snew_ref, *, bs):
    eye = (lax.broadcasted_iota(jnp.int32, (DK, DK), 0)
           == lax.broadcasted_iota(jnp.int32, (DK, DK), 1))

    def to_col(row):
        return jnp.sum(jnp.where(eye, jnp.broadcast_to(row, (DK, DK)), 0.0),
                       axis=-1, keepdims=True)

    for j in range(bs):
        for h in range(HEADS):
            lk = slice(h * DK, (h + 1) * DK)
            lv = slice(h * DV, (h + 1) * DV)
            qs = qs_ref[j:j + 1, lk]
            k = k_ref[j:j + 1, lk]
            eg = eg_ref[j:j + 1, lk]
            v = v_ref[j:j + 1, lv]
            s0 = s_ref[j, h]
            qk = jnp.sum(qs * k, axis=-1, keepdims=True)
            qg = jnp.broadcast_to(qs * eg, (SUBLANES, DK)).astype(_BF)
            o_inter = _dot(qg, s0.astype(_BF))[0:1]
            o_ref[j:j + 1, lv] = o_inter + qk * v
            snew_ref[j, h] = to_col(eg) * s0 + to_col(k) * v


def _sample_gla(qs, k, eg, v, s0, bs=8):
    n = qs.shape[0]
    kern = functools.partial(_sample_gla_kernel, bs=bs)
    return pl.pallas_call(
        kern,
        out_shape=(jax.ShapeDtypeStruct((n, D), _F32),
                   jax.ShapeDtypeStruct(s0.shape, _F32)),
        grid=(n // bs,),
        in_specs=[
            pl.BlockSpec((bs, D_QK), lambda i: (i, 0)),
            pl.BlockSpec((bs, D_QK), lambda i: (i, 0)),
            pl.BlockSpec((bs, D_QK), lambda i: (i, 0)),
            pl.BlockSpec((bs, D), lambda i: (i, 0)),
            pl.BlockSpec((bs, HEADS, DK, DV), lambda i: (i, 0, 0, 0)),
        ],
        out_specs=(
            pl.BlockSpec((bs, D), lambda i: (i, 0)),
            pl.BlockSpec((bs, HEADS, DK, DV), lambda i: (i, 0, 0, 0)),
        ),
        compiler_params=pltpu.CompilerParams(
            dimension_semantics=("arbitrary",), vmem_limit_bytes=VMEM_LIMIT),
        name="sample_gla",
    )(qs, k, eg, v, s0)


def _sample_out_kernel(x_ref, gate_ref, o_ref, zg_ref, ylru_ref, ggla_ref, wout_ref, gfin_ref,
                       y_ref):
    parts = [ylru_ref[...].astype(_BF)]
    for h in range(HEADS):
        lv = slice(h * DV, (h + 1) * DV)
        o = _rms(o_ref[:, lv]) * ggla_ref[:, lv]
        parts.append((o * _silu(zg_ref[:, lv])).astype(_BF))
    mix = jnp.concatenate(parts, axis=-1)
    y = x_ref[...] + gate_ref[...] * _dot(mix, wout_ref[...].astype(_BF))
    y_ref[...] = _rms(y) * gfin_ref[...]


def _sample_out(x, gate, o, zg, ylru, ggla, w_out2d, gfin):
    return pl.pallas_call(
        _sample_out_kernel,
        out_shape=jax.ShapeDtypeStruct(x.shape, _F32),
        compiler_params=pltpu.CompilerParams(vmem_limit_bytes=VMEM_LIMIT),
        name="sample_out",
    )(x, gate, o, zg, ylru, ggla, w_out2d, gfin)


def kernel(x_prompt, x_sample, state_lru_h, state_lru_conv, state_gla, c_prompt, c_sample,
           g_norm, w_ada, b_ada, w_in, conv_w, conv_b, w_gate_x, b_gate_x, w_gate_a, b_gate_a,
           lru_lambda, w_gla_g2, b_gla_g2, g_gla_norm, w_out, g_final):
    depth = g_norm.shape[0]
    assert depth == 1, "single-layer trunk"
    bp = x_prompt.shape[0]
    ns = x_sample.shape[0]

    w_in2d = w_in.reshape(D, w_in.shape[-1])
    w_out2d = w_out.reshape(D_MIX, D)
    w_zg = w_in2d[:, O_GLR + RANK:]
    wglr = jnp.pad(w_in2d[:, O_GLR:O_GLR + RANK], ((0, 0), (0, RANK_PAD - RANK)))
    wg2 = jnp.pad(w_gla_g2.reshape(RANK, D_QK), ((0, RANK_PAD - RANK), (0, 0)))
    wgate = jnp.concatenate([w_gate_x.reshape(LRU_BLOCKS, LRU_BW, LRU_BW),
                             w_gate_a.reshape(LRU_BLOCKS, LRU_BW, LRU_BW)], axis=-1)
    row = lambda a: a.reshape(1, -1)
    gnorm, convb = row(g_norm), row(conv_b)
    bgx, bga, lam = row(b_gate_x), row(b_gate_a), row(lru_lambda)
    bg2, ggla, gfin = row(b_gla_g2), row(g_gla_norm), row(g_final)
    convw = conv_w.reshape(CONV_W, D)

    ada = _ada(jnp.concatenate([c_prompt, c_sample], axis=0), w_ada.reshape(D, 3 * D),
               row(b_ada))
    ada_p = ada[:bp].reshape(bp, 3, D)
    ada_s = ada[bp:]

    yp, hp, cp, sp = _prompt(x_prompt, ada_p, gnorm, w_in2d, w_zg, w_out2d, wglr, convw, convb,
                             wgate, bgx, bga, lam, wg2, bg2, ggla, gfin)

    xs = x_sample.reshape(ns, D)
    conv0 = state_lru_conv.reshape(ns, CONV_W - 1, D)
    hs, xa, ylru, qs, ks, eg, vs, zg = _sample_in(
        xs, ada_s[:, :D], ada_s[:, D:2 * D], state_lru_h.reshape(ns, D),
        conv0[:, 0], conv0[:, 1], conv0[:, 2],
        gnorm, w_in2d, w_zg, wglr, convw, convb, wgate, bgx, bga, lam, wg2, bg2)
    o, ss = _sample_gla(qs, ks, eg, vs, state_gla.reshape(ns, HEADS, DK, DV))
    ys = _sample_out(xs, ada_s[:, 2 * D:], o, zg, ylru, ggla, w_out2d, gfin)
    cs = jnp.stack([conv0[:, 1], conv0[:, 2], xa], axis=1)

    return (yp, ys.reshape(ns, 1, D), hp.reshape(1, bp, D), cp[None], sp[None],
            hs[None], cs[None], ss[None])
```

```python
import functools

import jax
import jax.numpy as jnp
from jax import lax
from jax.experimental import pallas as pl
from jax.experimental.pallas import tpu as pltpu

EPS = 1e-6
D = 1024
LRU_BLOCKS = 8
LRU_BW = D // LRU_BLOCKS
LRU_C = 8.0
CONV_W = 4
HEADS = 4
DV = D // HEADS
DK = DV // 2
RANK = 16
RANK_PAD = 128
TAU = 16.0
CHUNK = 64
D_QK = HEADS * DK
D_MIX = 2 * D
O_GLR = 4 * D
N_GROUPS = 5

SUBLANES = 8
TM = 256
STAGE_ROWS = 256
VMEM_LIMIT = 56 * 1024 * 1024
NEG_LOG2E = -1.4426950408889634

_BF = jnp.bfloat16
_F32 = jnp.float32

_NT = (((1,), (1,)), ((), ()))


def _sigmoid(x):
    return 1.0 / (1.0 + jnp.exp2(x * NEG_LOG2E))


def _silu(x):
    return x * _sigmoid(x)


def _log_sigmoid(x):
    return jnp.minimum(x, 0.0) - jnp.log1p(jnp.exp(-jnp.abs(x)))


def _rms(x):
    return x * lax.rsqrt(jnp.mean(x * x, axis=-1, keepdims=True) + EPS)


def _dot(a, b):
    return jnp.dot(a, b, preferred_element_type=_F32)


def _group_scan(a3, u3):
    row = lax.broadcasted_iota(jnp.int32, a3.shape, 1)
    for s in (1, 2, 4):
        a_sh = pltpu.roll(a3, s, axis=1)
        u_sh = pltpu.roll(u3, s, axis=1)
        m = row >= s
        u3 = jnp.where(m, a3 * u_sh + u3, u3)
        a3 = jnp.where(m, a3 * a_sh, a3)
    return a3, u3


def _group_cumsum(x3):
    row = lax.broadcasted_iota(jnp.int32, x3.shape, 1)
    for s in (1, 2, 4):
        x3 = jnp.where(row >= s, x3 + pltpu.roll(x3, s, axis=1), x3)
    return x3


def _linear_scan(a, u, h0):
    r, l = a.shape
    g = r // SUBLANES
    a3, u3 = _group_scan(a.reshape(g, SUBLANES, l), u.reshape(g, SUBLANES, l))
    carry = h0
    out = []
    for i in range(g):
        hg = a3[i] * carry + u3[i]
        out.append(hg)
        carry = hg[SUBLANES - 1:SUBLANES]
    return jnp.concatenate(out, axis=0)


def _chunk_cumsum(x):
    r, l = x.shape
    g = r // SUBLANES
    per_chunk = CHUNK // SUBLANES
    x3 = _group_cumsum(x.reshape(g, SUBLANES, l))
    out = []
    carry = None
    for i in range(g):
        xg = x3[i] if i % per_chunk == 0 else x3[i] + carry
        out.append(xg)
        carry = xg[SUBLANES - 1:SUBLANES]
    return jnp.concatenate(out, axis=0)


def _lru_block(xc_n, pre, bgx, bga, logsig):
    gx = _sigmoid(pre[:, :LRU_BW] + bgx)
    ga = _sigmoid(pre[:, LRU_BW:] + bga)
    log_a = (LRU_C * ga) * logsig
    a = jnp.exp(log_a)
    om = 1.0 - a * a
    mult = jnp.where(om > 0.0, om * lax.rsqrt(om), 0.0)
    u = mult * (gx * xc_n)
    return a, u


def _gla_token_step(qs, k, eg, v, s0):
    eye = (lax.broadcasted_iota(jnp.int32, (DK, DK), 0)
           == lax.broadcasted_iota(jnp.int32, (DK, DK), 1))

    def to_col(row):
        return jnp.sum(jnp.where(eye, jnp.broadcast_to(row, (DK, DK)), 0.0),
                       axis=-1, keepdims=True)

    qk = jnp.sum(qs * k, axis=-1, keepdims=True)
    qg = jnp.broadcast_to(qs * eg, (SUBLANES, DK)).astype(_BF)
    o = _dot(qg, s0.astype(_BF))[0:1] + qk * v
    return o, to_col(eg) * s0 + to_col(k) * v


def _ada_kernel(c_ref, w_ref, b_ref, o_ref):
    c = c_ref[...]
    o_ref[...] = _dot(_silu(c).astype(_BF), w_ref[...].astype(_BF)) + b_ref[...]


def _ada(c_all, w_ada, b_ada):
    n = c_all.shape[0]
    return pl.pallas_call(
        _ada_kernel,
        out_shape=jax.ShapeDtypeStruct((n, 3 * D), _F32),
        grid=(3,),
        in_specs=[
            pl.BlockSpec((n, D), lambda j: (0, 0)),
            pl.BlockSpec((D, D), lambda j: (0, j)),
            pl.BlockSpec((1, D), lambda j: (0, j)),
        ],
        out_specs=pl.BlockSpec((n, D), lambda j: (0, j)),
        compiler_params=pltpu.CompilerParams(dimension_semantics=("arbitrary",)),
        name="ada",
    )(c_all, w_ada, b_ada)


def _stage_blocks(n, src_block, store_block, stage, sem):
    def copy(i, slot):
        return pltpu.make_async_copy(src_block(i), stage.at[slot], sem.at[slot])

    copy(0, 0).start()

    def body(i, carry):
        slot = lax.rem(i, 2)

        @pl.when(i + 1 < n)
        def _():
            copy(i + 1, 1 - slot).start()

        copy(i, slot).wait()
        store_block(i, stage[slot].astype(_BF))
        return carry

    lax.fori_loop(0, n, body, 0)


def _prompt_kernel(xn_ref, xp_ref, adan_ref, adap_ref, gnorm_ref, win_hbm, wzg_hbm, wout_hbm,
                   wglr_ref, convw_ref, convb_ref, wgate_ref, bgx_ref, bga_ref, lam_ref, wg2_ref,
                   bg2_ref, ggla_ref, gfin_ref, sqs_ref, sk_ref, seg_ref, sv_ref, ss0_ref,
                   y_ref, hlast_ref, convout_ref, sout_ref, so_ref, ssnew_ref,
                   wmain_s, wout_s, wglr_s, wg2_s, wgate_s, stage, sem,
                   hn_s, xa_s, za_s, xc_s, pre_s, q_s, k_s, v_s, zg_s, b_s, mix_s, yacc_s, ctail_s,
                   h_s, st_s, *, tm, nt, n_tiles, s_per_step, s_blocks):
    i = pl.program_id(0)
    t = lax.rem(jnp.minimum(i, n_tiles - 1), nt)
    t_next = lax.rem(jnp.minimum(i + 1, n_tiles - 1), nt)
    slot = lax.rem(i, 2)
    blocks_per_group = D // STAGE_ROWS
    half = D // 2
    quarter = D // 4

    def front_norm(x_ref, ada_ref):
        x = x_ref[0]
        hn = (_rms(x) * gnorm_ref[...]) * (1.0 + ada_ref[0, 1:2, :]) + ada_ref[0, 0:1, :]
        hn_s[...] = hn.astype(_BF)

    def front_xa():
        xa_s[SUBLANES:SUBLANES + tm, :] = _dot(hn_s[...], wmain_s[0])

    def front_za():
        za_s[...] = _dot(hn_s[...], wmain_s[1])

    def front_conv():
        xc = convb_ref[...]
        for j in range(CONV_W):
            off = SUBLANES - (CONV_W - 1) + j
            xc = xc + xa_s[off:off + tm, :] * convw_ref[j:j + 1, :]
        xc_s[...] = xc

    def front_gates():
        for n in range(LRU_BLOCKS):
            pre_s[:, 2 * n * LRU_BW:2 * (n + 1) * LRU_BW] = _dot(
                xc_s[:, n * LRU_BW:(n + 1) * LRU_BW].astype(_BF), wgate_s[n])

    @pl.when(i == 0)
    def _():
        def rows(j):
            return pl.ds(pl.multiple_of(lax.rem(j, blocks_per_group) * STAGE_ROWS, STAGE_ROWS),
                         STAGE_ROWS)

        def win_block(j):
            col = pl.multiple_of((j // blocks_per_group) * D, D)
            return win_hbm.at[rows(j), pl.ds(col, D)]

        def store_main(j, val):
            wmain_s[j // blocks_per_group, rows(j), :] = val

        def store_zg(j, val):
            wmain_s[N_GROUPS - 1, rows(j), :] = val

        def out_rows(j):
            return pl.ds(pl.multiple_of(j * STAGE_ROWS, STAGE_ROWS), STAGE_ROWS)

        def store_out(j, val):
            wout_s[out_rows(j), :] = val

        _stage_blocks((N_GROUPS - 1) * blocks_per_group, win_block, store_main, stage, sem)
        _stage_blocks(blocks_per_group, lambda j: wzg_hbm.at[rows(j), :], store_zg, stage, sem)
        _stage_blocks(D_MIX // STAGE_ROWS, lambda j: wout_hbm.at[out_rows(j), :], store_out,
                      stage, sem)
        wglr_s[...] = wglr_ref[...].astype(_BF)
        wg2_s[...] = wg2_ref[...].astype(_BF)
        wgate_s[...] = wgate_ref[...].astype(_BF)
        mix_s[...] = jnp.zeros_like(mix_s)
        xa_s[0:SUBLANES, :] = jnp.zeros((SUBLANES, D), _F32)
        front_norm(xp_ref, adap_ref)
        front_xa()
        front_za()
        front_conv()
        front_gates()

    @pl.when(t == 0)
    def _():
        h_s[...] = jnp.zeros_like(h_s)
        st_s[...] = jnp.zeros_like(st_s)

    for j in range(s_per_step):
        srow = slice(j, j + 1)
        for h in range(HEADS):
            lk = slice(h * DK, (h + 1) * DK)
            lv = slice(h * DV, (h + 1) * DV)
            o, snew = _gla_token_step(sqs_ref[0, srow, lk], sk_ref[0, srow, lk],
                                      seg_ref[0, srow, lk], sv_ref[0, srow, lv], ss0_ref[j, h])
            so_ref[0, srow, lv] = o
            ssnew_ref[j, h] = snew

    def out_slab(j):
        def run():
            cols = slice(j * quarter, (j + 1) * quarter)
            o = _dot(mix_s[1 - slot], wout_s[:, cols])
            yacc_s[:, cols] = xp_ref[0, :, cols] + adap_ref[0, 2:3, cols] * o
        return run

    def out_norm():
        y_ref[0] = _rms(yacc_s[...]) * gfin_ref[...]

    def proj(dst, dst_cols, group, w_cols):
        def run():
            dst[:, dst_cols] = _dot(hn_s[...], wmain_s[group, :, w_cols])
        return run

    def decay():
        glr = _dot(hn_s[...], wglr_s[...])
        la = _log_sigmoid(_dot(glr.astype(_BF), wg2_s[...]) + bg2_ref[...]) / TAU
        b_s[...] = _chunk_cumsum(la)

    lo, hi, full = slice(0, half), slice(half, D), slice(0, D_QK)
    tasks = [
        (out_slab(0), proj(q_s, full, 2, lo)),
        (out_slab(1), proj(k_s, full, 2, hi)),
        (out_slab(2), decay),
        (out_slab(3), out_norm),
        (proj(v_s, lo, 3, lo),),
        (proj(v_s, hi, 3, hi),),
        (proj(zg_s, lo, 4, lo),),
        (proj(zg_s, hi, 4, hi),),
    ]

    logsig = _log_sigmoid(lam_ref[...])
    for n in range(LRU_BLOCKS):
        ln = slice(n * LRU_BW, (n + 1) * LRU_BW)
        a, u = _lru_block(xc_s[:, ln], pre_s[:, 2 * n * LRU_BW:2 * (n + 1) * LRU_BW],
                          bgx_ref[:, ln], bga_ref[:, ln], logsig[:, ln])
        h = _linear_scan(a, u, h_s[:, ln])
        h_s[:, ln] = h[tm - 1:tm]
        mix_s[slot, :, ln] = (h * _silu(za_s[:, ln])).astype(_BF)
        for task in tasks[n]:
            task()

    tril = (lax.broadcasted_iota(jnp.int32, (CHUNK, CHUNK), 0)
            >= lax.broadcasted_iota(jnp.int32, (CHUNK, CHUNK), 1))

    def gla_chunk(c):
        rows = slice(c * CHUNK, (c + 1) * CHUNK)
        for h in range(HEADS):
            lk = slice(h * DK, (h + 1) * DK)
            lv = slice(h * DV, (h + 1) * DV)
            b = b_s[rows, lk]
            bl = b[CHUNK - 1:CHUNK]
            q = q_s[rows, lk] * (DK ** -0.5)
            k = k_s[rows, lk]
            v = v_s[rows, lv]
            qi = (q * jnp.exp(b)).astype(_BF)
            ki = (k * jnp.exp(-b)).astype(_BF)
            kd = (k * jnp.exp(bl - b)).astype(_BF)
            vb = v.astype(_BF)
            att = lax.dot_general(qi, ki, _NT, preferred_element_type=_F32)
            att = jnp.where(tril, att, 0.0).astype(_BF)
            st = st_s[h]
            o = _dot(att, vb) + lax.dot_general(qi, st.astype(_BF), _NT,
                                                preferred_element_type=_F32)
            st_s[h] = st * jnp.exp(bl) + _dot(v.T.astype(_BF), kd)
            o = _rms(o) * ggla_ref[:, lv]
            mix_s[slot, rows, D + h * DV:D + (h + 1) * DV] = (
                o * _silu(zg_s[rows, lv])).astype(_BF)

    def next_tail():
        tail = xa_s[tm:tm + SUBLANES, :]
        ctail_s[...] = tail
        xa_s[0:SUBLANES, :] = jnp.where(t_next != 0, tail, 0.0)

    next_tail()
    front_norm(xn_ref, adan_ref)
    gla_chunk(0)
    front_xa()
    gla_chunk(1)
    front_za()
    front_conv()
    gla_chunk(2)
    front_gates()
    gla_chunk(3)

    @pl.when((t == nt - 1) & (i < n_tiles))
    def _():
        hlast_ref[0] = h_s[...]
        convout_ref[0] = ctail_s[SUBLANES - (CONV_W - 1):SUBLANES, :]
        for h in range(HEADS):
            sout_ref[0, h] = st_s[h].T


def _const_spec(shape):
    nd = len(shape)
    return pl.BlockSpec(shape, lambda i: (0,) * nd, pipeline_mode=pl.Buffered(1))


def _prompt(x, ada3, gnorm, w_in2d, w_zg, w_out2d, wglr, convw, convb, wgate, bgx, bga, lam,
            wg2, bg2, ggla, gfin, sqs, sk, seg, sv, ss0, tm=TM):
    bsz, seq, _ = x.shape
    nt = seq // tm
    n_tiles = bsz * nt
    ns = ss0.shape[0]
    s_per_step = -(-ns // n_tiles)
    s_blocks = ns // s_per_step
    assert tm // CHUNK == 4, "the step body spreads the next tile's input side over 4 GLA chunks"
    assert s_blocks * s_per_step == ns and s_blocks <= n_tiles + 1
    kern = functools.partial(_prompt_kernel, tm=tm, nt=nt, n_tiles=n_tiles,
                             s_per_step=s_per_step, s_blocks=s_blocks)
    out_shape = (
        jax.ShapeDtypeStruct((bsz, seq, D), _F32),
        jax.ShapeDtypeStruct((bsz, 1, D), _F32),
        jax.ShapeDtypeStruct((bsz, CONV_W - 1, D), _F32),
        jax.ShapeDtypeStruct((bsz, HEADS, DK, DV), _F32),
        jax.ShapeDtypeStruct((s_blocks, s_per_step, D), _F32),
        jax.ShapeDtypeStruct(ss0.shape, _F32),
    )
    by_step = lambda a: a.reshape(s_blocks, s_per_step, a.shape[-1])
    sqs, sk, seg, sv = by_step(sqs), by_step(sk), by_step(seg), by_step(sv)
    s_rows = lambda i: (jnp.minimum(i, s_blocks - 1), 0, 0)
    s_block = lambda i: (jnp.minimum(i, s_blocks - 1), 0, 0, 0)
    cur = lambda i: jnp.minimum(i, n_tiles - 1)
    prev = lambda i: jnp.maximum(i - 1, 0)
    nxt = lambda i: jnp.minimum(i + 1, n_tiles - 1)
    tile = lambda f: (lambda i: (f(i) // nt, lax.rem(f(i), nt), 0))
    seq_of = lambda f: (lambda i: (f(i) // nt, 0, 0))
    hbm = pl.BlockSpec(memory_space=pl.ANY)
    in_specs = [
        pl.BlockSpec((1, tm, D), tile(nxt)),
        pl.BlockSpec((1, tm, D), tile(prev)),
        pl.BlockSpec((1, 3, D), seq_of(nxt)),
        pl.BlockSpec((1, 3, D), seq_of(prev)),
        _const_spec((1, D)),
        hbm, hbm, hbm,
        _const_spec((D, RANK_PAD)),
        _const_spec((CONV_W, D)),
        _const_spec((1, D)),
        _const_spec((LRU_BLOCKS, LRU_BW, 2 * LRU_BW)),
        _const_spec((1, D)),
        _const_spec((1, D)),
        _const_spec((1, D)),
        _const_spec((RANK_PAD, D_QK)),
        _const_spec((1, D_QK)),
        _const_spec((1, D)),
        _const_spec((1, D)),
        pl.BlockSpec((1, s_per_step, D_QK), s_rows),
        pl.BlockSpec((1, s_per_step, D_QK), s_rows),
        pl.BlockSpec((1, s_per_step, D_QK), s_rows),
        pl.BlockSpec((1, s_per_step, D), s_rows),
        pl.BlockSpec((s_per_step, HEADS, DK, DV), s_block),
    ]
    out_specs = (
        pl.BlockSpec((1, tm, D), tile(prev)),
        pl.BlockSpec((1, 1, D), seq_of(cur)),
        pl.BlockSpec((1, CONV_W - 1, D), seq_of(cur)),
        pl.BlockSpec((1, HEADS, DK, DV), lambda i: (cur(i) // nt, 0, 0, 0)),
        pl.BlockSpec((1, s_per_step, D), s_rows),
        pl.BlockSpec((s_per_step, HEADS, DK, DV), s_block),
    )
    scratch = [
        pltpu.VMEM((N_GROUPS, D, D), _BF),
        pltpu.VMEM((D_MIX, D), _BF),
        pltpu.VMEM((D, RANK_PAD), _BF),
        pltpu.VMEM((RANK_PAD, D_QK), _BF),
        pltpu.VMEM((LRU_BLOCKS, LRU_BW, 2 * LRU_BW), _BF),
        pltpu.VMEM((2, STAGE_ROWS, D), _F32),
        pltpu.SemaphoreType.DMA((2,)),
        pltpu.VMEM((tm, D), _BF),
        pltpu.VMEM((tm + SUBLANES, D), _F32),
        pltpu.VMEM((tm, D), _F32),
        pltpu.VMEM((tm, D), _F32),
        pltpu.VMEM((tm, 2 * D), _F32),
        pltpu.VMEM((tm, D_QK), _F32),
        pltpu.VMEM((tm, D_QK), _F32),
        pltpu.VMEM((tm, D), _F32),
        pltpu.VMEM((tm, D), _F32),
        pltpu.VMEM((tm, D_QK), _F32),
        pltpu.VMEM((2, tm, D_MIX), _BF),
        pltpu.VMEM((tm, D), _F32),
        pltpu.VMEM((SUBLANES, D), _F32),
        pltpu.VMEM((1, D), _F32),
        pltpu.VMEM((HEADS, DV, DK), _F32),
    ]
    return pl.pallas_call(
        kern,
        out_shape=out_shape,
        grid=(n_tiles + 1,),
        in_specs=in_specs,
        out_specs=out_specs,
        scratch_shapes=scratch,
        compiler_params=pltpu.CompilerParams(
            dimension_semantics=("arbitrary",),
            vmem_limit_bytes=VMEM_LIMIT),
        name="prompt_layer",
    )(x, x, ada3, ada3, gnorm, w_in2d, w_zg, w_out2d, wglr, convw, convb, wgate, bgx, bga, lam,
      wg2, bg2, ggla, gfin, sqs, sk, seg, sv, ss0)


def _sample_in_kernel(x_ref, shift_ref, scale_ref, h0_ref, c0_ref, c1_ref, c2_ref,
                      gnorm_ref, win_ref, wzg_ref, wglr_ref, convw_ref, convb_ref, wgate_ref,
                      bgx_ref, bga_ref, lam_ref, wg2_ref, bg2_ref,
                      hnew_ref, xa_ref, ylru_ref, qs_ref, k_ref, eg_ref, v_ref, zg_ref):
    x = x_ref[...]
    hn = (_rms(x) * gnorm_ref[...]) * (1.0 + scale_ref[...]) + shift_ref[...]
    hnb = hn.astype(_BF)

    def proj(c0, width):
        return _dot(hnb, win_ref[:, c0:c0 + width].astype(_BF))

    xa = proj(0, D)
    za = proj(D, D)
    qs_ref[...] = proj(2 * D, D_QK) * (DK ** -0.5)
    k_ref[...] = proj(2 * D + D_QK, D_QK)
    v_ref[...] = proj(3 * D, D)
    zg_ref[...] = _dot(hnb, wzg_ref[...].astype(_BF))
    glr = _dot(hnb, wglr_ref[...].astype(_BF))
    la = _log_sigmoid(_dot(glr.astype(_BF), wg2_ref[...].astype(_BF)) + bg2_ref[...]) / TAU
    eg_ref[...] = jnp.exp(la)
    xa_ref[...] = xa

    xc = convb_ref[...]
    for j, cj in enumerate((c0_ref[...], c1_ref[...], c2_ref[...], xa)):
        xc = xc + cj * convw_ref[j:j + 1, :]
    xcb = xc.astype(_BF)
    logsig = _log_sigmoid(lam_ref[...])
    for n in range(LRU_BLOCKS):
        ln = slice(n * LRU_BW, (n + 1) * LRU_BW)
        pre = _dot(xcb[:, ln], wgate_ref[n].astype(_BF))
        a, u = _lru_block(xc[:, ln], pre, bgx_ref[:, ln], bga_ref[:, ln], logsig[:, ln])
        h = a * h0_ref[:, ln] + u
        hnew_ref[:, ln] = h
        ylru_ref[:, ln] = h * _silu(za[:, ln])


def _sample_in(x, shift, scale, h0, c0, c1, c2, gnorm, w_in2d, w_zg, wglr, convw, convb, wgate,
               bgx, bga, lam, wg2, bg2):
    n = x.shape[0]
    f = lambda w: jax.ShapeDtypeStruct((n, w), _F32)
    return pl.pallas_call(
        _sample_in_kernel,
        out_shape=(f(D), f(D), f(D), f(D_QK), f(D_QK), f(D_QK), f(D), f(D)),
        compiler_params=pltpu.CompilerParams(vmem_limit_bytes=VMEM_LIMIT),
        name="sample_in",
    )(x, shift, scale, h0, c0, c1, c2, gnorm, w_in2d, w_zg, wglr, convw, convb, wgate, bgx, bga,
      lam, wg2, bg2)


def _sample_out_kernel(x_ref, gate_ref, o_ref, zg_ref, ylru_ref, ggla_ref, wout_ref, gfin_ref,
                       y_ref):
    parts = [ylru_ref[...].astype(_BF)]
    for h in range(HEADS):
        lv = slice(h * DV, (h + 1) * DV)
        o = _rms(o_ref[:, lv]) * ggla_ref[:, lv]
        parts.append((o * _silu(zg_ref[:, lv])).astype(_BF))
    mix = jnp.concatenate(parts, axis=-1)
    y = x_ref[...] + gate_ref[...] * _dot(mix, wout_ref[...].astype(_BF))
    y_ref[...] = _rms(y) * gfin_ref[...]


def _sample_out(x, gate, o, zg, ylru, ggla, w_out2d, gfin):
    return pl.pallas_call(
        _sample_out_kernel,
        out_shape=jax.ShapeDtypeStruct(x.shape, _F32),
        compiler_params=pltpu.CompilerParams(vmem_limit_bytes=VMEM_LIMIT),
        name="sample_out",
    )(x, gate, o, zg, ylru, ggla, w_out2d, gfin)


def kernel(x_prompt, x_sample, state_lru_h, state_lru_conv, state_gla, c_prompt, c_sample,
           g_norm, w_ada, b_ada, w_in, conv_w, conv_b, w_gate_x, b_gate_x, w_gate_a, b_gate_a,
           lru_lambda, w_gla_g2, b_gla_g2, g_gla_norm, w_out, g_final):
    depth = g_norm.shape[0]
    assert depth == 1, "single-layer trunk"
    bp = x_prompt.shape[0]
    ns = x_sample.shape[0]

    w_in2d = w_in.reshape(D, w_in.shape[-1])
    w_out2d = w_out.reshape(D_MIX, D)
    w_zg = w_in2d[:, O_GLR + RANK:]
    wglr = jnp.pad(w_in2d[:, O_GLR:O_GLR + RANK], ((0, 0), (0, RANK_PAD - RANK)))
    wg2 = jnp.pad(w_gla_g2.reshape(RANK, D_QK), ((0, RANK_PAD - RANK), (0, 0)))
    wgate = jnp.concatenate([w_gate_x.reshape(LRU_BLOCKS, LRU_BW, LRU_BW),
                             w_gate_a.reshape(LRU_BLOCKS, LRU_BW, LRU_BW)], axis=-1)
    row = lambda a: a.reshape(1, -1)
    gnorm, convb = row(g_norm), row(conv_b)
    bgx, bga, lam = row(b_gate_x), row(b_gate_a), row(lru_lambda)
    bg2, ggla, gfin = row(b_gla_g2), row(g_gla_norm), row(g_final)
    convw = conv_w.reshape(CONV_W, D)

    ada = _ada(jnp.concatenate([c_prompt, c_sample], axis=0), w_ada.reshape(D, 3 * D),
               row(b_ada))
    ada_p = ada[:bp].reshape(bp, 3, D)
    ada_s = ada[bp:]

    xs = x_sample.reshape(ns, D)
    conv0 = state_lru_conv.reshape(ns, CONV_W - 1, D)
    hs, xa, ylru, qs, ks, eg, vs, zg = _sample_in(
        xs, ada_s[:, :D], ada_s[:, D:2 * D], state_lru_h.reshape(ns, D),
        conv0[:, 0], conv0[:, 1], conv0[:, 2],
        gnorm, w_in2d, w_zg, wglr, convw, convb, wgate, bgx, bga, lam, wg2, bg2)

    yp, hp, cp, sp, o, ss = _prompt(x_prompt, ada_p, gnorm, w_in2d, w_zg, w_out2d, wglr, convw,
                                    convb, wgate, bgx, bga, lam, wg2, bg2, ggla, gfin,
                                    qs, ks, eg, vs, state_gla.reshape(ns, HEADS, DK, DV))

    ys = _sample_out(xs, ada_s[:, 2 * D:], o.reshape(ns, D), zg, ylru, ggla, w_out2d, gfin)
    cs = jnp.stack([conv0[:, 1], conv0[:, 2], xa], axis=1)

    return (yp, ys.reshape(ns, 1, D), hp.reshape(1, bp, D), cp[None], sp[None],
            hs[None], cs[None], ss[None])
```

```python
import functools

import jax
import jax.numpy as jnp
from jax import lax
from jax.experimental import pallas as pl
from jax.experimental.pallas import tpu as pltpu

EPS = 1e-6
D = 1024
LRU_BLOCKS = 8
LRU_BW = D // LRU_BLOCKS
LRU_C = 8.0
CONV_W = 4
HEADS = 4
DV = D // HEADS
DK = DV // 2
RANK = 16
RANK_PAD = 128
TAU = 16.0
CHUNK = 64
D_QK = HEADS * DK
D_MIX = 2 * D
O_GLR = 4 * D
N_GROUPS = 5

SUBLANES = 8
TM = 256
STAGE_ROWS = 256
VMEM_LIMIT = 56 * 1024 * 1024
NEG_LOG2E = -1.4426950408889634

_BF = jnp.bfloat16
_F32 = jnp.float32

_NT = (((1,), (1,)), ((), ()))


def _sigmoid(x):
    return 1.0 / (1.0 + jnp.exp2(x * NEG_LOG2E))


def _silu(x):
    return x * _sigmoid(x)


def _log_sigmoid(x):
    return jnp.minimum(x, 0.0) - jnp.log1p(jnp.exp(-jnp.abs(x)))


def _rms(x):
    return x * lax.rsqrt(jnp.mean(x * x, axis=-1, keepdims=True) + EPS)


def _dot(a, b):
    return jnp.dot(a, b, preferred_element_type=_F32)


def _group_scan(a3, u3):
    row = lax.broadcasted_iota(jnp.int32, a3.shape, 1)
    for s in (1, 2, 4):
        a_sh = pltpu.roll(a3, s, axis=1)
        u_sh = pltpu.roll(u3, s, axis=1)
        m = row >= s
        u3 = jnp.where(m, a3 * u_sh + u3, u3)
        a3 = jnp.where(m, a3 * a_sh, a3)
    return a3, u3


def _group_cumsum(x3):
    row = lax.broadcasted_iota(jnp.int32, x3.shape, 1)
    for s in (1, 2, 4):
        x3 = jnp.where(row >= s, x3 + pltpu.roll(x3, s, axis=1), x3)
    return x3


def _linear_scan(a, u, h0):
    r, l = a.shape
    g = r // SUBLANES
    a3, u3 = _group_scan(a.reshape(g, SUBLANES, l), u.reshape(g, SUBLANES, l))
    carry = h0
    out = []
    for i in range(g):
        hg = a3[i] * carry + u3[i]
        out.append(hg)
        carry = hg[SUBLANES - 1:SUBLANES]
    return jnp.concatenate(out, axis=0)


def _chunk_cumsum(x):
    r, l = x.shape
    g = r // SUBLANES
    per_chunk = CHUNK // SUBLANES
    x3 = _group_cumsum(x.reshape(g, SUBLANES, l))
    out = []
    carry = None
    for i in range(g):
        xg = x3[i] if i % per_chunk == 0 else x3[i] + carry
        out.append(xg)
        carry = xg[SUBLANES - 1:SUBLANES]
    return jnp.concatenate(out, axis=0)


def _lru_block(xc_n, pre, bgx, bga, logsig):
    gx = _sigmoid(pre[:, :LRU_BW] + bgx)
    ga = _sigmoid(pre[:, LRU_BW:] + bga)
    log_a = (LRU_C * ga) * logsig
    a = jnp.exp(log_a)
    om = 1.0 - a * a
    mult = jnp.where(om > 0.0, om * lax.rsqrt(om), 0.0)
    u = mult * (gx * xc_n)
    return a, u


def _gla_token_step(qs, k, eg, v, s0):
    eye = (lax.broadcasted_iota(jnp.int32, (DK, DK), 0)
           == lax.broadcasted_iota(jnp.int32, (DK, DK), 1))

    def to_col(row):
        return jnp.sum(jnp.where(eye, jnp.broadcast_to(row, (DK, DK)), 0.0),
                       axis=-1, keepdims=True)

    qk = jnp.sum(qs * k, axis=-1, keepdims=True)
    qg = jnp.broadcast_to(qs * eg, (SUBLANES, DK)).astype(_BF)
    o = _dot(qg, s0.astype(_BF))[0:1] + qk * v
    return o, to_col(eg) * s0 + to_col(k) * v


def _ada_kernel(c_ref, w_ref, b_ref, o_ref):
    c = c_ref[...]
    o_ref[...] = _dot(_silu(c).astype(_BF), w_ref[...].astype(_BF)) + b_ref[...]


def _ada(c_all, w_ada, b_ada):
    n = c_all.shape[0]
    return pl.pallas_call(
        _ada_kernel,
        out_shape=jax.ShapeDtypeStruct((n, 3 * D), _F32),
        grid=(3,),
        in_specs=[
            pl.BlockSpec((n, D), lambda j: (0, 0)),
            pl.BlockSpec((D, D), lambda j: (0, j)),
            pl.BlockSpec((1, D), lambda j: (0, j)),
        ],
        out_specs=pl.BlockSpec((n, D), lambda j: (0, j)),
        compiler_params=pltpu.CompilerParams(dimension_semantics=("arbitrary",)),
        name="ada",
    )(c_all, w_ada, b_ada)


def _stage_blocks(n, src_block, store_block, stage, sem):
    def copy(i, slot):
        return pltpu.make_async_copy(src_block(i), stage.at[slot], sem.at[slot])

    copy(0, 0).start()

    def body(i, carry):
        slot = lax.rem(i, 2)

        @pl.when(i + 1 < n)
        def _():
            copy(i + 1, 1 - slot).start()

        copy(i, slot).wait()
        store_block(i, stage[slot].astype(_BF))
        return carry

    lax.fori_loop(0, n, body, 0)


def _stage_in_proj(wt_hbm, wmain_s, stage, sem):
    per_group = D // STAGE_ROWS

    def block(g, c):
        row0 = g * D + (g // (N_GROUPS - 1)) * RANK + c * STAGE_ROWS
        return wt_hbm.at[pl.ds(pl.multiple_of(row0, SUBLANES), STAGE_ROWS), :]

    def copy(g, c):
        return pltpu.make_async_copy(block(g, c), stage.at[c % 2], sem.at[c % 2])

    copy(0, 0).start()

    def body(g, carry):
        for c in range(per_group):
            if c + 1 < per_group:
                copy(g, c + 1).start()
            else:
                @pl.when(g + 1 < N_GROUPS)
                def _():
                    copy(g + 1, 0).start()
            copy(g, c).wait()
            wmain_s[g, :, c * STAGE_ROWS:(c + 1) * STAGE_ROWS] = stage[c % 2].T.astype(_BF)
        return carry

    lax.fori_loop(0, N_GROUPS, body, 0)


def _prompt_kernel(xn_ref, xp_ref, adan_ref, adap_ref, gnorm_ref, wt_hbm, wout_hbm,
                   wglr_ref, convw_ref, convb_ref, wgate_ref, bgx_ref, bga_ref, lam_ref, wg2_ref,
                   bg2_ref, ggla_ref, gfin_ref, sqs_ref, sk_ref, seg_ref, sv_ref, ss0_ref,
                   y_ref, hlast_ref, convout_ref, sout_ref, so_ref, ssnew_ref,
                   wmain_s, wout_s, wglr_s, wg2_s, wgate_s, stage, sem,
                   hn_s, xa_s, za_s, xc_s, pre_s, q_s, k_s, v_s, zg_s, b_s, mix_s, yacc_s, ctail_s,
                   h_s, st_s, *, tm, nt, n_tiles, s_per_step, s_blocks):
    i = pl.program_id(0)
    t = lax.rem(jnp.minimum(i, n_tiles - 1), nt)
    t_next = lax.rem(jnp.minimum(i + 1, n_tiles - 1), nt)
    slot = lax.rem(i, 2)
    half = D // 2
    quarter = D // 4

    def front_norm(x_ref, ada_ref):
        x = x_ref[0]
        hn = (_rms(x) * gnorm_ref[...]) * (1.0 + ada_ref[0, 1:2, :]) + ada_ref[0, 0:1, :]
        hn_s[...] = hn.astype(_BF)

    def front_xa():
        xa_s[SUBLANES:SUBLANES + tm, :] = _dot(hn_s[...], wmain_s[0])

    def front_za():
        za_s[...] = _dot(hn_s[...], wmain_s[1])

    def front_conv():
        xc = convb_ref[...]
        for j in range(CONV_W):
            off = SUBLANES - (CONV_W - 1) + j
            xc = xc + xa_s[off:off + tm, :] * convw_ref[j:j + 1, :]
        xc_s[...] = xc

    def front_gates():
        for n in range(LRU_BLOCKS):
            pre_s[:, 2 * n * LRU_BW:2 * (n + 1) * LRU_BW] = _dot(
                xc_s[:, n * LRU_BW:(n + 1) * LRU_BW].astype(_BF), wgate_s[n])

    @pl.when(i == 0)
    def _():
        def out_rows(j):
            return pl.ds(pl.multiple_of(j * STAGE_ROWS, STAGE_ROWS), STAGE_ROWS)

        def store_out(j, val):
            wout_s[out_rows(j), :] = val

        _stage_in_proj(wt_hbm, wmain_s, stage, sem)
        _stage_blocks(D_MIX // STAGE_ROWS, lambda j: wout_hbm.at[out_rows(j), :], store_out,
                      stage, sem)
        wglr_s[...] = wglr_ref[...].astype(_BF)
        wg2_s[...] = wg2_ref[...].astype(_BF)
        wgate_s[...] = wgate_ref[...].astype(_BF)
        mix_s[...] = jnp.zeros_like(mix_s)
        xa_s[0:SUBLANES, :] = jnp.zeros((SUBLANES, D), _F32)
        front_norm(xp_ref, adap_ref)
        front_xa()
        front_za()
        front_conv()
        front_gates()

    @pl.when(t == 0)
    def _():
        h_s[...] = jnp.zeros_like(h_s)
        st_s[...] = jnp.zeros_like(st_s)

    for j in range(s_per_step):
        srow = slice(j, j + 1)
        for h in range(HEADS):
            lk = slice(h * DK, (h + 1) * DK)
            lv = slice(h * DV, (h + 1) * DV)
            o, snew = _gla_token_step(sqs_ref[0, srow, lk], sk_ref[0, srow, lk],
                                      seg_ref[0, srow, lk], sv_ref[0, srow, lv], ss0_ref[j, h])
            so_ref[0, srow, lv] = o
            ssnew_ref[j, h] = snew

    def out_slab(j):
        def run():
            cols = slice(j * quarter, (j + 1) * quarter)
            o = _dot(mix_s[1 - slot], wout_s[:, cols])
            yacc_s[:, cols] = xp_ref[0, :, cols] + adap_ref[0, 2:3, cols] * o
        return run

    def out_norm():
        y_ref[0] = _rms(yacc_s[...]) * gfin_ref[...]

    def proj(dst, dst_cols, group, w_cols):
        def run():
            dst[:, dst_cols] = _dot(hn_s[...], wmain_s[group, :, w_cols])
        return run

    def decay():
        glr = _dot(hn_s[...], wglr_s[...])
        la = _log_sigmoid(_dot(glr.astype(_BF), wg2_s[...]) + bg2_ref[...]) / TAU
        b_s[...] = _chunk_cumsum(la)

    lo, hi, full = slice(0, half), slice(half, D), slice(0, D_QK)
    tasks = [
        (out_slab(0), proj(q_s, full, 2, lo)),
        (out_slab(1), proj(k_s, full, 2, hi)),
        (out_slab(2), decay),
        (out_slab(3), out_norm),
        (proj(v_s, lo, 3, lo),),
        (proj(v_s, hi, 3, hi),),
        (proj(zg_s, lo, 4, lo),),
        (proj(zg_s, hi, 4, hi),),
    ]

    logsig = _log_sigmoid(lam_ref[...])
    for n in range(LRU_BLOCKS):
        ln = slice(n * LRU_BW, (n + 1) * LRU_BW)
        a, u = _lru_block(xc_s[:, ln], pre_s[:, 2 * n * LRU_BW:2 * (n + 1) * LRU_BW],
                          bgx_ref[:, ln], bga_ref[:, ln], logsig[:, ln])
        h = _linear_scan(a, u, h_s[:, ln])
        h_s[:, ln] = h[tm - 1:tm]
        mix_s[slot, :, ln] = (h * _silu(za_s[:, ln])).astype(_BF)
        for task in tasks[n]:
            task()

    tril = (lax.broadcasted_iota(jnp.int32, (CHUNK, CHUNK), 0)
            >= lax.broadcasted_iota(jnp.int32, (CHUNK, CHUNK), 1))

    def gla_chunk(c):
        rows = slice(c * CHUNK, (c + 1) * CHUNK)
        for h in range(HEADS):
            lk = slice(h * DK, (h + 1) * DK)
            lv = slice(h * DV, (h + 1) * DV)
            b = b_s[rows, lk]
            bl = b[CHUNK - 1:CHUNK]
            q = q_s[rows, lk] * (DK ** -0.5)
            k = k_s[rows, lk]
            v = v_s[rows, lv]
            qi = (q * jnp.exp(b)).astype(_BF)
            ki = (k * jnp.exp(-b)).astype(_BF)
            kd = (k * jnp.exp(bl - b)).astype(_BF)
            vb = v.astype(_BF)
            att = lax.dot_general(qi, ki, _NT, preferred_element_type=_F32)
            att = jnp.where(tril, att, 0.0).astype(_BF)
            st = st_s[h]
            o = _dot(att, vb) + lax.dot_general(qi, st.astype(_BF), _NT,
                                                preferred_element_type=_F32)
            st_s[h] = st * jnp.exp(bl) + _dot(v.T.astype(_BF), kd)
            o = _rms(o) * ggla_ref[:, lv]
            mix_s[slot, rows, D + h * DV:D + (h + 1) * DV] = (
                o * _silu(zg_s[rows, lv])).astype(_BF)

    def next_tail():
        tail = xa_s[tm:tm + SUBLANES, :]
        ctail_s[...] = tail
        xa_s[0:SUBLANES, :] = jnp.where(t_next != 0, tail, 0.0)

    next_tail()
    front_norm(xn_ref, adan_ref)
    gla_chunk(0)
    front_xa()
    gla_chunk(1)
    front_za()
    front_conv()
    gla_chunk(2)
    front_gates()
    gla_chunk(3)

    @pl.when((t == nt - 1) & (i < n_tiles))
    def _():
        hlast_ref[0] = h_s[...]
        convout_ref[0] = ctail_s[SUBLANES - (CONV_W - 1):SUBLANES, :]
        for h in range(HEADS):
            sout_ref[0, h] = st_s[h].T


def _const_spec(shape):
    nd = len(shape)
    return pl.BlockSpec(shape, lambda i: (0,) * nd, pipeline_mode=pl.Buffered(1))


def _prompt(x, ada3, gnorm, w_in_t, w_out2d, wglr, convw, convb, wgate, bgx, bga, lam,
            wg2, bg2, ggla, gfin, sqs, sk, seg, sv, ss0, tm=TM):
    bsz, seq, _ = x.shape
    nt = seq // tm
    n_tiles = bsz * nt
    ns = ss0.shape[0]
    s_per_step = -(-ns // n_tiles)
    s_blocks = ns // s_per_step
    assert tm // CHUNK == 4, "the step body spreads the next tile's input side over 4 GLA chunks"
    assert s_blocks * s_per_step == ns and s_blocks <= n_tiles + 1
    kern = functools.partial(_prompt_kernel, tm=tm, nt=nt, n_tiles=n_tiles,
                             s_per_step=s_per_step, s_blocks=s_blocks)
    out_shape = (
        jax.ShapeDtypeStruct((bsz, seq, D), _F32),
        jax.ShapeDtypeStruct((bsz, 1, D), _F32),
        jax.ShapeDtypeStruct((bsz, CONV_W - 1, D), _F32),
        jax.ShapeDtypeStruct((bsz, HEADS, DK, DV), _F32),
        jax.ShapeDtypeStruct((s_blocks, s_per_step, D), _F32),
        jax.ShapeDtypeStruct(ss0.shape, _F32),
    )
    by_step = lambda a: a.reshape(s_blocks, s_per_step, a.shape[-1])
    sqs, sk, seg, sv = by_step(sqs), by_step(sk), by_step(seg), by_step(sv)
    s_rows = lambda i: (jnp.minimum(i, s_blocks - 1), 0, 0)
    s_block = lambda i: (jnp.minimum(i, s_blocks - 1), 0, 0, 0)
    cur = lambda i: jnp.minimum(i, n_tiles - 1)
    prev = lambda i: jnp.maximum(i - 1, 0)
    nxt = lambda i: jnp.minimum(i + 1, n_tiles - 1)
    tile = lambda f: (lambda i: (f(i) // nt, lax.rem(f(i), nt), 0))
    seq_of = lambda f: (lambda i: (f(i) // nt, 0, 0))
    hbm = pl.BlockSpec(memory_space=pl.ANY)
    in_specs = [
        pl.BlockSpec((1, tm, D), tile(nxt)),
        pl.BlockSpec((1, tm, D), tile(prev)),
        pl.BlockSpec((1, 3, D), seq_of(nxt)),
        pl.BlockSpec((1, 3, D), seq_of(prev)),
        _const_spec((1, D)),
        hbm, hbm,
        _const_spec((D, RANK_PAD)),
        _const_spec((CONV_W, D)),
        _const_spec((1, D)),
        _const_spec((LRU_BLOCKS, LRU_BW, 2 * LRU_BW)),
        _const_spec((1, D)),
        _const_spec((1, D)),
        _const_spec((1, D)),
        _const_spec((RANK_PAD, D_QK)),
        _const_spec((1, D_QK)),
        _const_spec((1, D)),
        _const_spec((1, D)),
        pl.BlockSpec((1, s_per_step, D_QK), s_rows),
        pl.BlockSpec((1, s_per_step, D_QK), s_rows),
        pl.BlockSpec((1, s_per_step, D_QK), s_rows),
        pl.BlockSpec((1, s_per_step, D), s_rows),
        pl.BlockSpec((s_per_step, HEADS, DK, DV), s_block),
    ]
    out_specs = (
        pl.BlockSpec((1, tm, D), tile(prev)),
        pl.BlockSpec((1, 1, D), seq_of(cur)),
        pl.BlockSpec((1, CONV_W - 1, D), seq_of(cur)),
        pl.BlockSpec((1, HEADS, DK, DV), lambda i: (cur(i) // nt, 0, 0, 0)),
        pl.BlockSpec((1, s_per_step, D), s_rows),
        pl.BlockSpec((s_per_step, HEADS, DK, DV), s_block),
    )
    scratch = [
        pltpu.VMEM((N_GROUPS, D, D), _BF),
        pltpu.VMEM((D_MIX, D), _BF),
        pltpu.VMEM((D, RANK_PAD), _BF),
        pltpu.VMEM((RANK_PAD, D_QK), _BF),
        pltpu.VMEM((LRU_BLOCKS, LRU_BW, 2 * LRU_BW), _BF),
        pltpu.VMEM((2, STAGE_ROWS, D), _F32),
        pltpu.SemaphoreType.DMA((2,)),
        pltpu.VMEM((tm, D), _BF),
        pltpu.VMEM((tm + SUBLANES, D), _F32),
        pltpu.VMEM((tm, D), _F32),
        pltpu.VMEM((tm, D), _F32),
        pltpu.VMEM((tm, 2 * D), _F32),
        pltpu.VMEM((tm, D_QK), _F32),
        pltpu.VMEM((tm, D_QK), _F32),
        pltpu.VMEM((tm, D), _F32),
        pltpu.VMEM((tm, D), _F32),
        pltpu.VMEM((tm, D_QK), _F32),
        pltpu.VMEM((2, tm, D_MIX), _BF),
        pltpu.VMEM((tm, D), _F32),
        pltpu.VMEM((SUBLANES, D), _F32),
        pltpu.VMEM((1, D), _F32),
        pltpu.VMEM((HEADS, DV, DK), _F32),
    ]
    return pl.pallas_call(
        kern,
        out_shape=out_shape,
        grid=(n_tiles + 1,),
        in_specs=in_specs,
        out_specs=out_specs,
        scratch_shapes=scratch,
        compiler_params=pltpu.CompilerParams(
            dimension_semantics=("arbitrary",),
            vmem_limit_bytes=VMEM_LIMIT),
        name="prompt_layer",
    )(x, x, ada3, ada3, gnorm, w_in_t, w_out2d, wglr, convw, convb, wgate, bgx, bga, lam,
      wg2, bg2, ggla, gfin, sqs, sk, seg, sv, ss0)


def _sample_in_kernel(x_ref, shift_ref, scale_ref, h0_ref, c0_ref, c1_ref, c2_ref,
                      gnorm_ref, wt_ref, wglr_ref, convw_ref, convb_ref, wgate_ref,
                      bgx_ref, bga_ref, lam_ref, wg2_ref, bg2_ref,
                      hnew_ref, xa_ref, ylru_ref, qs_ref, k_ref, eg_ref, v_ref, zg_ref):
    x = x_ref[...]
    hn = (_rms(x) * gnorm_ref[...]) * (1.0 + scale_ref[...]) + shift_ref[...]
    hnb = hn.astype(_BF)

    def proj(c0, width):
        return lax.dot_general(hnb, wt_ref[c0:c0 + width, :].astype(_BF), _NT,
                               preferred_element_type=_F32)

    xa = proj(0, D)
    za = proj(D, D)
    qs_ref[...] = proj(2 * D, D_QK) * (DK ** -0.5)
    k_ref[...] = proj(2 * D + D_QK, D_QK)
    v_ref[...] = proj(3 * D, D)
    zg_ref[...] = proj(O_GLR + RANK, D)
    glr = _dot(hnb, wglr_ref[...].astype(_BF))
    la = _log_sigmoid(_dot(glr.astype(_BF), wg2_ref[...].astype(_BF)) + bg2_ref[...]) / TAU
    eg_ref[...] = jnp.exp(la)
    xa_ref[...] = xa

    xc = convb_ref[...]
    for j, cj in enumerate((c0_ref[...], c1_ref[...], c2_ref[...], xa)):
        xc = xc + cj * convw_ref[j:j + 1, :]
    xcb = xc.astype(_BF)
    logsig = _log_sigmoid(lam_ref[...])
    for n in range(LRU_BLOCKS):
        ln = slice(n * LRU_BW, (n + 1) * LRU_BW)
        pre = _dot(xcb[:, ln], wgate_ref[n].astype(_BF))
        a, u = _lru_block(xc[:, ln], pre, bgx_ref[:, ln], bga_ref[:, ln], logsig[:, ln])
        h = a * h0_ref[:, ln] + u
        hnew_ref[:, ln] = h
        ylru_ref[:, ln] = h * _silu(za[:, ln])


def _sample_in(x, shift, scale, h0, c0, c1, c2, gnorm, w_in_t, wglr, convw, convb, wgate,
               bgx, bga, lam, wg2, bg2):
    n = x.shape[0]
    f = lambda w: jax.ShapeDtypeStruct((n, w), _F32)
    return pl.pallas_call(
        _sample_in_kernel,
        out_shape=(f(D), f(D), f(D), f(D_QK), f(D_QK), f(D_QK), f(D), f(D)),
        compiler_params=pltpu.CompilerParams(vmem_limit_bytes=VMEM_LIMIT),
        name="sample_in",
    )(x, shift, scale, h0, c0, c1, c2, gnorm, w_in_t, wglr, convw, convb, wgate, bgx, bga,
      lam, wg2, bg2)


def _sample_out_kernel(x_ref, gate_ref, o_ref, zg_ref, ylru_ref, ggla_ref, wout_ref, gfin_ref,
                       y_ref):
    parts = [ylru_ref[...].astype(_BF)]
    for h in range(HEADS):
        lv = slice(h * DV, (h + 1) * DV)
        o = _rms(o_ref[:, lv]) * ggla_ref[:, lv]
        parts.append((o * _silu(zg_ref[:, lv])).astype(_BF))
    mix = jnp.concatenate(parts, axis=-1)
    y = x_ref[...] + gate_ref[...] * _dot(mix, wout_ref[...].astype(_BF))
    y_ref[...] = _rms(y) * gfin_ref[...]


def _sample_out(x, gate, o, zg, ylru, ggla, w_out2d, gfin):
    return pl.pallas_call(
        _sample_out_kernel,
        out_shape=jax.ShapeDtypeStruct(x.shape, _F32),
        compiler_params=pltpu.CompilerParams(vmem_limit_bytes=VMEM_LIMIT),
        name="sample_out",
    )(x, gate, o, zg, ylru, ggla, w_out2d, gfin)


def kernel(x_prompt, x_sample, state_lru_h, state_lru_conv, state_gla, c_prompt, c_sample,
           g_norm, w_ada, b_ada, w_in, conv_w, conv_b, w_gate_x, b_gate_x, w_gate_a, b_gate_a,
           lru_lambda, w_gla_g2, b_gla_g2, g_gla_norm, w_out, g_final):
    depth = g_norm.shape[0]
    assert depth == 1, "single-layer trunk"
    bp = x_prompt.shape[0]
    ns = x_sample.shape[0]

    w_in_t = jnp.swapaxes(w_in, 1, 2).reshape(w_in.shape[-1], D)
    w_out2d = w_out.reshape(D_MIX, D)
    wglr = jnp.pad(w_in_t[O_GLR:O_GLR + RANK, :].T, ((0, 0), (0, RANK_PAD - RANK)))
    wg2 = jnp.pad(w_gla_g2.reshape(RANK, D_QK), ((0, RANK_PAD - RANK), (0, 0)))
    wgate = jnp.concatenate([w_gate_x.reshape(LRU_BLOCKS, LRU_BW, LRU_BW),
                             w_gate_a.reshape(LRU_BLOCKS, LRU_BW, LRU_BW)], axis=-1)
    row = lambda a: a.reshape(1, -1)
    gnorm, convb = row(g_norm), row(conv_b)
    bgx, bga, lam = row(b_gate_x), row(b_gate_a), row(lru_lambda)
    bg2, ggla, gfin = row(b_gla_g2), row(g_gla_norm), row(g_final)
    convw = conv_w.reshape(CONV_W, D)

    ada = _ada(jnp.concatenate([c_prompt, c_sample], axis=0), w_ada.reshape(D, 3 * D),
               row(b_ada))
    ada_p = ada[:bp].reshape(bp, 3, D)
    ada_s = ada[bp:]

    xs = x_sample.reshape(ns, D)
    conv0 = state_lru_conv.reshape(ns, CONV_W - 1, D)
    hs, xa, ylru, qs, ks, eg, vs, zg = _sample_in(
        xs, ada_s[:, :D], ada_s[:, D:2 * D], state_lru_h.reshape(ns, D),
        conv0[:, 0], conv0[:, 1], conv0[:, 2],
        gnorm, w_in_t, wglr, convw, convb, wgate, bgx, bga, lam, wg2, bg2)

    yp, hp, cp, sp, o, ss = _prompt(x_prompt, ada_p, gnorm, w_in_t, w_out2d, wglr, convw,
                                    convb, wgate, bgx, bga, lam, wg2, bg2, ggla, gfin,
                                    qs, ks, eg, vs, state_gla.reshape(ns, HEADS, DK, DV))

    ys = _sample_out(xs, ada_s[:, 2 * D:], o.reshape(ns, D), zg, ylru, ggla, w_out2d, gfin)
    cs = jnp.stack([conv0[:, 1], conv0[:, 2], xa], axis=1)

    return (yp, ys.reshape(ns, 1, D), hp.reshape(1, bp, D), cp[None], sp[None],
            hs[None], cs[None], ss[None])
```

```python
import functools

import jax
import jax.numpy as jnp
from jax import lax
from jax.experimental import pallas as pl
from jax.experimental.pallas import tpu as pltpu

EPS = 1e-6
D = 1024
LRU_BLOCKS = 8
LRU_BW = D // LRU_BLOCKS
LRU_C = 8.0
CONV_W = 4
HEADS = 4
DV = D // HEADS
DK = DV // 2
RANK = 16
RANK_PAD = 128
TAU = 16.0
CHUNK = 64
D_QK = HEADS * DK
D_MIX = 2 * D
O_GLR = 4 * D
N_GROUPS = 5

SUBLANES = 8
TM = 256
STAGE_ROWS = 256
VMEM_LIMIT = 56 * 1024 * 1024
NEG_LOG2E = -1.4426950408889634

_BF = jnp.bfloat16
_F32 = jnp.float32

_NT = (((1,), (1,)), ((), ()))


def _sigmoid(x):
    return 1.0 / (1.0 + jnp.exp2(x * NEG_LOG2E))


def _silu(x):
    return x * _sigmoid(x)


def _log_sigmoid(x):
    return jnp.minimum(x, 0.0) - jnp.log(1.0 + jnp.exp2(jnp.abs(x) * NEG_LOG2E))


def _rms(x):
    return x * lax.rsqrt(jnp.mean(x * x, axis=-1, keepdims=True) + EPS)


def _dot(a, b):
    return jnp.dot(a, b, preferred_element_type=_F32)


def _group_scan(a3, u3):
    row = lax.broadcasted_iota(jnp.int32, a3.shape, 1)
    for s in (1, 2, 4):
        a_sh = pltpu.roll(a3, s, axis=1)
        u_sh = pltpu.roll(u3, s, axis=1)
        m = row >= s
        u3 = jnp.where(m, a3 * u_sh + u3, u3)
        a3 = jnp.where(m, a3 * a_sh, a3)
    return a3, u3


def _group_cumsum(x3):
    row = lax.broadcasted_iota(jnp.int32, x3.shape, 1)
    for s in (1, 2, 4):
        x3 = jnp.where(row >= s, x3 + pltpu.roll(x3, s, axis=1), x3)
    return x3


def _linear_scan(a, u, h0):
    r, l = a.shape
    g = r // SUBLANES
    a3, u3 = _group_scan(a.reshape(g, SUBLANES, l), u.reshape(g, SUBLANES, l))
    carry = h0
    out = []
    for i in range(g):
        hg = a3[i] * carry + u3[i]
        out.append(hg)
        carry = hg[SUBLANES - 1:SUBLANES]
    return jnp.concatenate(out, axis=0)


def _chunk_cumsum(x):
    r, l = x.shape
    g = r // SUBLANES
    per_chunk = CHUNK // SUBLANES
    x3 = _group_cumsum(x.reshape(g, SUBLANES, l))
    out = []
    carry = None
    for i in range(g):
        xg = x3[i] if i % per_chunk == 0 else x3[i] + carry
        out.append(xg)
        carry = xg[SUBLANES - 1:SUBLANES]
    return jnp.concatenate(out, axis=0)


def _lru_block(xc_n, pre, bgx, bga, logsig):
    gx = _sigmoid(pre[:, :LRU_BW] + bgx)
    ga = _sigmoid(pre[:, LRU_BW:] + bga)
    log_a = (LRU_C * ga) * logsig
    a = jnp.exp(log_a)
    om = 1.0 - a * a
    mult = jnp.where(om > 0.0, om * lax.rsqrt(om), 0.0)
    u = mult * (gx * xc_n)
    return a, u


def _gla_token_step(qs, k, eg, v, s0):
    eye = (lax.broadcasted_iota(jnp.int32, (DK, DK), 0)
           == lax.broadcasted_iota(jnp.int32, (DK, DK), 1))

    def to_col(row):
        return jnp.sum(jnp.where(eye, jnp.broadcast_to(row, (DK, DK)), 0.0),
                       axis=-1, keepdims=True)

    qk = jnp.sum(qs * k, axis=-1, keepdims=True)
    qg = jnp.broadcast_to(qs * eg, (SUBLANES, DK)).astype(_BF)
    o = _dot(qg, s0.astype(_BF))[0:1] + qk * v
    return o, to_col(eg) * s0 + to_col(k) * v


def _ada_kernel(c_ref, w_ref, b_ref, o_ref):
    c = c_ref[...]
    o_ref[...] = _dot(_silu(c).astype(_BF), w_ref[...].astype(_BF)) + b_ref[...]


def _ada(c_all, w_ada, b_ada):
    n = c_all.shape[0]
    return pl.pallas_call(
        _ada_kernel,
        out_shape=jax.ShapeDtypeStruct((n, 3 * D), _F32),
        grid=(3,),
        in_specs=[
            pl.BlockSpec((n, D), lambda j: (0, 0)),
            pl.BlockSpec((D, D), lambda j: (0, j)),
            pl.BlockSpec((1, D), lambda j: (0, j)),
        ],
        out_specs=pl.BlockSpec((n, D), lambda j: (0, j)),
        compiler_params=pltpu.CompilerParams(dimension_semantics=("arbitrary",)),
        name="ada",
    )(c_all, w_ada, b_ada)


def _stage_blocks(n, src_block, store_block, stage, sem):
    def copy(i, slot):
        return pltpu.make_async_copy(src_block(i), stage.at[slot], sem.at[slot])

    copy(0, 0).start()

    def body(i, carry):
        slot = lax.rem(i, 2)

        @pl.when(i + 1 < n)
        def _():
            copy(i + 1, 1 - slot).start()

        copy(i, slot).wait()
        store_block(i, stage[slot].astype(_BF))
        return carry

    lax.fori_loop(0, n, body, 0)


def _stage_in_proj(wt_hbm, wmain_s, stage, sem):
    per_group = D // STAGE_ROWS

    def block(g, c):
        row0 = g * D + (g // (N_GROUPS - 1)) * RANK + c * STAGE_ROWS
        return wt_hbm.at[pl.ds(pl.multiple_of(row0, SUBLANES), STAGE_ROWS), :]

    def copy(g, c):
        return pltpu.make_async_copy(block(g, c), stage.at[c % 2], sem.at[c % 2])

    copy(0, 0).start()

    def body(g, carry):
        for c in range(per_group):
            if c + 1 < per_group:
                copy(g, c + 1).start()
            else:
                @pl.when(g + 1 < N_GROUPS)
                def _():
                    copy(g + 1, 0).start()
            copy(g, c).wait()
            wmain_s[g, :, c * STAGE_ROWS:(c + 1) * STAGE_ROWS] = stage[c % 2].T.astype(_BF)
        return carry

    lax.fori_loop(0, N_GROUPS, body, 0)


def _prompt_kernel(xn_ref, xp_ref, adan_ref, adap_ref, gnorm_ref, wt_hbm, wout_hbm,
                   wglr_ref, convw_ref, convb_ref, wgate_ref, bgx_ref, bga_ref, lam_ref, wg2_ref,
                   bg2_ref, ggla_ref, gfin_ref, sqs_ref, sk_ref, seg_ref, sv_ref, ss0_ref,
                   y_ref, hlast_ref, convout_ref, sout_ref, so_ref, ssnew_ref,
                   wmain_s, wout_s, wglr_s, wg2_s, wgate_s, stage, sem,
                   hn_s, xa_s, za_s, xc_s, pre_s, q_s, k_s, v_s, zg_s, b_s, mix_s, yacc_s, ctail_s,
                   h_s, st_s, *, tm, nt, n_tiles, s_per_step, s_blocks):
    i = pl.program_id(0)
    t = lax.rem(jnp.minimum(i, n_tiles - 1), nt)
    t_next = lax.rem(jnp.minimum(i + 1, n_tiles - 1), nt)
    slot = lax.rem(i, 2)
    half = D // 2
    quarter = D // 4

    def front_norm(x_ref, ada_ref):
        x = x_ref[0]
        hn = (_rms(x) * gnorm_ref[...]) * (1.0 + ada_ref[0, 1:2, :]) + ada_ref[0, 0:1, :]
        hn_s[...] = hn.astype(_BF)

    def front_xa():
        xa_s[SUBLANES:SUBLANES + tm, :] = _dot(hn_s[...], wmain_s[0])

    def front_za():
        za_s[...] = _dot(hn_s[...], wmain_s[1])

    def front_conv():
        xe = xa_s[...]
        xc = convb_ref[...]
        for j in range(CONV_W):
            lag = CONV_W - 1 - j
            xj = xe if lag == 0 else pltpu.roll(xe, lag, axis=0)
            xc = xc + xj[SUBLANES:, :] * convw_ref[j:j + 1, :]
        xc_s[...] = xc

    def front_gates():
        for n in range(LRU_BLOCKS):
            pre_s[:, 2 * n * LRU_BW:2 * (n + 1) * LRU_BW] = _dot(
                xc_s[:, n * LRU_BW:(n + 1) * LRU_BW].astype(_BF), wgate_s[n])

    @pl.when(i == 0)
    def _():
        def out_rows(j):
            return pl.ds(pl.multiple_of(j * STAGE_ROWS, STAGE_ROWS), STAGE_ROWS)

        def store_out(j, val):
            wout_s[out_rows(j), :] = val

        _stage_in_proj(wt_hbm, wmain_s, stage, sem)
        _stage_blocks(D_MIX // STAGE_ROWS, lambda j: wout_hbm.at[out_rows(j), :], store_out,
                      stage, sem)
        wglr_s[...] = wglr_ref[...].astype(_BF)
        wg2_s[...] = wg2_ref[...].astype(_BF)
        wgate_s[...] = wgate_ref[...].astype(_BF)
        mix_s[...] = jnp.zeros_like(mix_s)
        xa_s[0:SUBLANES, :] = jnp.zeros((SUBLANES, D), _F32)
        front_norm(xp_ref, adap_ref)
        front_xa()
        front_za()
        front_conv()
        front_gates()

    @pl.when(t == 0)
    def _():
        h_s[...] = jnp.zeros_like(h_s)
        st_s[...] = jnp.zeros_like(st_s)

    for j in range(s_per_step):
        srow = slice(j, j + 1)
        for h in range(HEADS):
            lk = slice(h * DK, (h + 1) * DK)
            lv = slice(h * DV, (h + 1) * DV)
            o, snew = _gla_token_step(sqs_ref[0, srow, lk], sk_ref[0, srow, lk],
                                      seg_ref[0, srow, lk], sv_ref[0, srow, lv], ss0_ref[j, h])
            so_ref[0, srow, lv] = o
            ssnew_ref[j, h] = snew

    def out_slab(j):
        def run():
            cols = slice(j * quarter, (j + 1) * quarter)
            o = _dot(mix_s[1 - slot], wout_s[:, cols])
            yacc_s[:, cols] = xp_ref[0, :, cols] + adap_ref[0, 2:3, cols] * o
        return run

    def out_norm():
        y_ref[0] = _rms(yacc_s[...]) * gfin_ref[...]

    def proj(dst, dst_cols, group, w_cols):
        def run():
            dst[:, dst_cols] = _dot(hn_s[...], wmain_s[group, :, w_cols])
        return run

    def decay():
        glr = lax.dot_general(hn_s[...], wglr_s[...], _NT,
                              preferred_element_type=_F32)
        la = _log_sigmoid(_dot(glr.astype(_BF), wg2_s[...]) + bg2_ref[...]) / TAU
        b_s[...] = _chunk_cumsum(la)

    lo, hi, full = slice(0, half), slice(half, D), slice(0, D_QK)
    tasks = [
        (out_slab(0), proj(q_s, full, 2, lo)),
        (out_slab(1), proj(k_s, full, 2, hi)),
        (out_slab(2), decay),
        (out_slab(3), out_norm),
        (proj(v_s, lo, 3, lo),),
        (proj(v_s, hi, 3, hi),),
        (proj(zg_s, lo, 4, lo),),
        (proj(zg_s, hi, 4, hi),),
    ]

    logsig = _log_sigmoid(lam_ref[...])
    for n in range(LRU_BLOCKS):
        ln = slice(n * LRU_BW, (n + 1) * LRU_BW)
        a, u = _lru_block(xc_s[:, ln], pre_s[:, 2 * n * LRU_BW:2 * (n + 1) * LRU_BW],
                          bgx_ref[:, ln], bga_ref[:, ln], logsig[:, ln])
        h = _linear_scan(a, u, h_s[:, ln])
        h_s[:, ln] = h[tm - 1:tm]
        mix_s[slot, :, ln] = (h * _silu(za_s[:, ln])).astype(_BF)
        for task in tasks[n]:
            task()

    tril = (lax.broadcasted_iota(jnp.int32, (CHUNK, CHUNK), 0)
            >= lax.broadcasted_iota(jnp.int32, (CHUNK, CHUNK), 1))

    def gla_chunk(c):
        rows = slice(c * CHUNK, (c + 1) * CHUNK)
        for h in range(HEADS):
            lk = slice(h * DK, (h + 1) * DK)
            lv = slice(h * DV, (h + 1) * DV)
            b = b_s[rows, lk]
            bl = b[CHUNK - 1:CHUNK]
            q = q_s[rows, lk] * (DK ** -0.5)
            k = k_s[rows, lk]
            v = v_s[rows, lv]
            qi = (q * jnp.exp(b)).astype(_BF)
            ki = (k * jnp.exp(-b)).astype(_BF)
            kd = (k * jnp.exp(bl - b)).astype(_BF)
            vb = v.astype(_BF)
            att = lax.dot_general(qi, ki, _NT, preferred_element_type=_F32)
            att = jnp.where(tril, att, 0.0).astype(_BF)
            st = st_s[h]
            o = _dot(att, vb) + lax.dot_general(qi, st.astype(_BF), _NT,
                                                preferred_element_type=_F32)
            st_s[h] = st * jnp.exp(bl) + _dot(v.T.astype(_BF), kd)
            o = _rms(o) * ggla_ref[:, lv]
            mix_s[slot, rows, D + h * DV:D + (h + 1) * DV] = (
                o * _silu(zg_s[rows, lv])).astype(_BF)

    def next_tail():
        tail = xa_s[tm:tm + SUBLANES, :]
        ctail_s[...] = tail
        xa_s[0:SUBLANES, :] = jnp.where(t_next != 0, tail, 0.0)

    next_tail()
    front_norm(xn_ref, adan_ref)
    gla_chunk(0)
    front_xa()
    gla_chunk(1)
    front_za()
    front_conv()
    gla_chunk(2)
    front_gates()
    gla_chunk(3)

    @pl.when((t == nt - 1) & (i < n_tiles))
    def _():
        hlast_ref[0] = h_s[...]
        convout_ref[0] = ctail_s[SUBLANES - (CONV_W - 1):SUBLANES, :]
        for h in range(HEADS):
            sout_ref[0, h] = st_s[h].T


def _const_spec(shape):
    nd = len(shape)
    return pl.BlockSpec(shape, lambda i: (0,) * nd, pipeline_mode=pl.Buffered(1))


def _prompt(x, ada3, gnorm, w_in_t, w_out2d, wglr, convw, convb, wgate, bgx, bga, lam,
            wg2, bg2, ggla, gfin, sqs, sk, seg, sv, ss0, tm=TM):
    bsz, seq, _ = x.shape
    nt = seq // tm
    n_tiles = bsz * nt
    ns = ss0.shape[0]
    s_per_step = -(-ns // n_tiles)
    s_blocks = ns // s_per_step
    assert tm // CHUNK == 4, "the step body spreads the next tile's input side over 4 GLA chunks"
    assert s_blocks * s_per_step == ns and s_blocks <= n_tiles + 1
    kern = functools.partial(_prompt_kernel, tm=tm, nt=nt, n_tiles=n_tiles,
                             s_per_step=s_per_step, s_blocks=s_blocks)
    out_shape = (
        jax.ShapeDtypeStruct((bsz, seq, D), _F32),
        jax.ShapeDtypeStruct((bsz, 1, D), _F32),
        jax.ShapeDtypeStruct((bsz, CONV_W - 1, D), _F32),
        jax.ShapeDtypeStruct((bsz, HEADS, DK, DV), _F32),
        jax.ShapeDtypeStruct((s_blocks, s_per_step, D), _F32),
        jax.ShapeDtypeStruct(ss0.shape, _F32),
    )
    by_step = lambda a: a.reshape(s_blocks, s_per_step, a.shape[-1])
    sqs, sk, seg, sv = by_step(sqs), by_step(sk), by_step(seg), by_step(sv)
    s_rows = lambda i: (jnp.minimum(i, s_blocks - 1), 0, 0)
    s_block = lambda i: (jnp.minimum(i, s_blocks - 1), 0, 0, 0)
    cur = lambda i: jnp.minimum(i, n_tiles - 1)
    prev = lambda i: jnp.maximum(i - 1, 0)
    nxt = lambda i: jnp.minimum(i + 1, n_tiles - 1)
    tile = lambda f: (lambda i: (f(i) // nt, lax.rem(f(i), nt), 0))
    seq_of = lambda f: (lambda i: (f(i) // nt, 0, 0))
    hbm = pl.BlockSpec(memory_space=pl.ANY)
    in_specs = [
        pl.BlockSpec((1, tm, D), tile(nxt)),
        pl.BlockSpec((1, tm, D), tile(prev)),
        pl.BlockSpec((1, 3, D), seq_of(nxt)),
        pl.BlockSpec((1, 3, D), seq_of(prev)),
        _const_spec((1, D)),
        hbm, hbm,
        _const_spec((RANK_PAD, D)),
        _const_spec((CONV_W, D)),
        _const_spec((1, D)),
        _const_spec((LRU_BLOCKS, LRU_BW, 2 * LRU_BW)),
        _const_spec((1, D)),
        _const_spec((1, D)),
        _const_spec((1, D)),
        _const_spec((RANK_PAD, D_QK)),
        _const_spec((1, D_QK)),
        _const_spec((1, D)),
        _const_spec((1, D)),
        pl.BlockSpec((1, s_per_step, D_QK), s_rows),
        pl.BlockSpec((1, s_per_step, D_QK), s_rows),
        pl.BlockSpec((1, s_per_step, D_QK), s_rows),
        pl.BlockSpec((1, s_per_step, D), s_rows),
        pl.BlockSpec((s_per_step, HEADS, DK, DV), s_block),
    ]
    out_specs = (
        pl.BlockSpec((1, tm, D), tile(prev)),
        pl.BlockSpec((1, 1, D), seq_of(cur)),
        pl.BlockSpec((1, CONV_W - 1, D), seq_of(cur)),
        pl.BlockSpec((1, HEADS, DK, DV), lambda i: (cur(i) // nt, 0, 0, 0)),
        pl.BlockSpec((1, s_per_step, D), s_rows),
        pl.BlockSpec((s_per_step, HEADS, DK, DV), s_block),
    )
    scratch = [
        pltpu.VMEM((N_GROUPS, D, D), _BF),
        pltpu.VMEM((D_MIX, D), _BF),
        pltpu.VMEM((RANK_PAD, D), _BF),
        pltpu.VMEM((RANK_PAD, D_QK), _BF),
        pltpu.VMEM((LRU_BLOCKS, LRU_BW, 2 * LRU_BW), _BF),
        pltpu.VMEM((2, STAGE_ROWS, D), _F32),
        pltpu.SemaphoreType.DMA((2,)),
        pltpu.VMEM((tm, D), _BF),
        pltpu.VMEM((tm + SUBLANES, D), _F32),
        pltpu.VMEM((tm, D), _F32),
        pltpu.VMEM((tm, D), _F32),
        pltpu.VMEM((tm, 2 * D), _F32),
        pltpu.VMEM((tm, D_QK), _F32),
        pltpu.VMEM((tm, D_QK), _F32),
        pltpu.VMEM((tm, D), _F32),
        pltpu.VMEM((tm, D), _F32),
        pltpu.VMEM((tm, D_QK), _F32),
        pltpu.VMEM((2, tm, D_MIX), _BF),
        pltpu.VMEM((tm, D), _F32),
        pltpu.VMEM((SUBLANES, D), _F32),
        pltpu.VMEM((1, D), _F32),
        pltpu.VMEM((HEADS, DV, DK), _F32),
    ]
    return pl.pallas_call(
        kern,
        out_shape=out_shape,
        grid=(n_tiles + 1,),
        in_specs=in_specs,
        out_specs=out_specs,
        scratch_shapes=scratch,
        compiler_params=pltpu.CompilerParams(
            dimension_semantics=("arbitrary",),
            vmem_limit_bytes=VMEM_LIMIT),
        name="prompt_layer",
    )(x, x, ada3, ada3, gnorm, w_in_t, w_out2d, wglr, convw, convb, wgate, bgx, bga, lam,
      wg2, bg2, ggla, gfin, sqs, sk, seg, sv, ss0)


def _sample_in_kernel(x_ref, shift_ref, scale_ref, h0_ref, c0_ref, c1_ref, c2_ref,
                      gnorm_ref, wt_ref, wglr_ref, convw_ref, convb_ref, wgate_ref,
                      bgx_ref, bga_ref, lam_ref, wg2_ref, bg2_ref,
                      hnew_ref, xa_ref, ylru_ref, qs_ref, k_ref, eg_ref, v_ref, zg_ref):
    x = x_ref[...]
    hn = (_rms(x) * gnorm_ref[...]) * (1.0 + scale_ref[...]) + shift_ref[...]
    hnb = hn.astype(_BF)

    def proj(c0, width):
        return lax.dot_general(hnb, wt_ref[c0:c0 + width, :].astype(_BF), _NT,
                               preferred_element_type=_F32)

    xa = proj(0, D)
    za = proj(D, D)
    qs_ref[...] = proj(2 * D, D_QK) * (DK ** -0.5)
    k_ref[...] = proj(2 * D + D_QK, D_QK)
    v_ref[...] = proj(3 * D, D)
    zg_ref[...] = proj(O_GLR + RANK, D)
    glr = lax.dot_general(hnb, wglr_ref[...].astype(_BF), _NT, preferred_element_type=_F32)
    la = _log_sigmoid(_dot(glr.astype(_BF), wg2_ref[...].astype(_BF)) + bg2_ref[...]) / TAU
    eg_ref[...] = jnp.exp(la)
    xa_ref[...] = xa

    xc = convb_ref[...]
    for j, cj in enumerate((c0_ref[...], c1_ref[...], c2_ref[...], xa)):
        xc = xc + cj * convw_ref[j:j + 1, :]
    xcb = xc.astype(_BF)
    logsig = _log_sigmoid(lam_ref[...])
    for n in range(LRU_BLOCKS):
        ln = slice(n * LRU_BW, (n + 1) * LRU_BW)
        pre = _dot(xcb[:, ln], wgate_ref[n].astype(_BF))
        a, u = _lru_block(xc[:, ln], pre, bgx_ref[:, ln], bga_ref[:, ln], logsig[:, ln])
        h = a * h0_ref[:, ln] + u
        hnew_ref[:, ln] = h
        ylru_ref[:, ln] = h * _silu(za[:, ln])


def _sample_in(x, shift, scale, h0, c0, c1, c2, gnorm, w_in_t, wglr, convw, convb, wgate,
               bgx, bga, lam, wg2, bg2):
    n = x.shape[0]
    f = lambda w: jax.ShapeDtypeStruct((n, w), _F32)
    return pl.pallas_call(
        _sample_in_kernel,
        out_shape=(f(D), f(D), f(D), f(D_QK), f(D_QK), f(D_QK), f(D), f(D)),
        compiler_params=pltpu.CompilerParams(vmem_limit_bytes=VMEM_LIMIT),
        name="sample_in",
    )(x, shift, scale, h0, c0, c1, c2, gnorm, w_in_t, wglr, convw, convb, wgate, bgx, bga,
      lam, wg2, bg2)


def _sample_out_kernel(x_ref, gate_ref, o_ref, zg_ref, ylru_ref, ggla_ref, wout_ref, gfin_ref,
                       y_ref):
    parts = [ylru_ref[...].astype(_BF)]
    for h in range(HEADS):
        lv = slice(h * DV, (h + 1) * DV)
        o = _rms(o_ref[:, lv]) * ggla_ref[:, lv]
        parts.append((o * _silu(zg_ref[:, lv])).astype(_BF))
    mix = jnp.concatenate(parts, axis=-1)
    y = x_ref[...] + gate_ref[...] * _dot(mix, wout_ref[...].astype(_BF))
    y_ref[...] = _rms(y) * gfin_ref[...]


def _sample_out(x, gate, o, zg, ylru, ggla, w_out2d, gfin):
    return pl.pallas_call(
        _sample_out_kernel,
        out_shape=jax.ShapeDtypeStruct(x.shape, _F32),
        compiler_params=pltpu.CompilerParams(vmem_limit_bytes=VMEM_LIMIT),
        name="sample_out",
    )(x, gate, o, zg, ylru, ggla, w_out2d, gfin)


def kernel(x_prompt, x_sample, state_lru_h, state_lru_conv, state_gla, c_prompt, c_sample,
           g_norm, w_ada, b_ada, w_in, conv_w, conv_b, w_gate_x, b_gate_x, w_gate_a, b_gate_a,
           lru_lambda, w_gla_g2, b_gla_g2, g_gla_norm, w_out, g_final):
    depth = g_norm.shape[0]
    assert depth == 1, "single-layer trunk"
    bp = x_prompt.shape[0]
    ns = x_sample.shape[0]

    w_in_t = jnp.swapaxes(w_in, 1, 2).reshape(w_in.shape[-1], D)
    w_out2d = w_out.reshape(D_MIX, D)
    wglr = jnp.pad(w_in_t[O_GLR:O_GLR + RANK, :], ((0, RANK_PAD - RANK), (0, 0)))
    wg2 = jnp.pad(w_gla_g2.reshape(RANK, D_QK), ((0, RANK_PAD - RANK), (0, 0)))
    wgate = jnp.concatenate([w_gate_x.reshape(LRU_BLOCKS, LRU_BW, LRU_BW),
                             w_gate_a.reshape(LRU_BLOCKS, LRU_BW, LRU_BW)], axis=-1)
    row = lambda a: a.reshape(1, -1)
    gnorm, convb = row(g_norm), row(conv_b)
    bgx, bga, lam = row(b_gate_x), row(b_gate_a), row(lru_lambda)
    bg2, ggla, gfin = row(b_gla_g2), row(g_gla_norm), row(g_final)
    convw = conv_w.reshape(CONV_W, D)

    ada = _ada(jnp.concatenate([c_prompt, c_sample], axis=0), w_ada.reshape(D, 3 * D),
               row(b_ada))
    ada_p = ada[:bp].reshape(bp, 3, D)
    ada_s = ada[bp:]

    xs = x_sample.reshape(ns, D)
    conv0 = state_lru_conv.reshape(ns, CONV_W - 1, D)
    hs, xa, ylru, qs, ks, eg, vs, zg = _sample_in(
        xs, ada_s[:, :D], ada_s[:, D:2 * D], state_lru_h.reshape(ns, D),
        conv0[:, 0], conv0[:, 1], conv0[:, 2],
        gnorm, w_in_t, wglr, convw, convb, wgate, bgx, bga, lam, wg2, bg2)

    yp, hp, cp, sp, o, ss = _prompt(x_prompt, ada_p, gnorm, w_in_t, w_out2d, wglr, convw,
                                    convb, wgate, bgx, bga, lam, wg2, bg2, ggla, gfin,
                                    qs, ks, eg, vs, state_gla.reshape(ns, HEADS, DK, DV))

    ys = _sample_out(xs, ada_s[:, 2 * D:], o.reshape(ns, D), zg, ylru, ggla, w_out2d, gfin)
    cs = jnp.stack([conv0[:, 1], conv0[:, 2], xa], axis=1)

    return (yp, ys.reshape(ns, 1, D), hp.reshape(1, bp, D), cp[None], sp[None],
            hs[None], cs[None], ss[None])
```

```python
import functools

import jax
import jax.numpy as jnp
from jax import lax
from jax.experimental import pallas as pl
from jax.experimental.pallas import tpu as pltpu

EPS = 1e-6
D = 1024
LRU_BLOCKS = 8
LRU_BW = D // LRU_BLOCKS
LRU_C = 8.0
CONV_W = 4
HEADS = 4
DV = D // HEADS
DK = DV // 2
RANK = 16
RANK_PAD = 128
TAU = 16.0
CHUNK = 64
D_QK = HEADS * DK
D_MIX = 2 * D
O_GLR = 4 * D
N_GROUPS = 5

SUBLANES = 8
TM = 256
STAGE_ROWS = 256
VMEM_LIMIT = 56 * 1024 * 1024
NEG_LOG2E = -1.4426950408889634

_BF = jnp.bfloat16
_F32 = jnp.float32

_NT = (((1,), (1,)), ((), ()))


def _sigmoid(x):
    return 1.0 / (1.0 + jnp.exp2(x * NEG_LOG2E))


def _silu(x):
    return x * _sigmoid(x)


def _log_sigmoid(x):
    return jnp.minimum(x, 0.0) - jnp.log(1.0 + jnp.exp2(jnp.abs(x) * NEG_LOG2E))


def _rms(x):
    return x * lax.rsqrt(jnp.mean(x * x, axis=-1, keepdims=True) + EPS)


def _dot(a, b):
    return jnp.dot(a, b, preferred_element_type=_F32)


def _group_scan(a3, u3):
    row = lax.broadcasted_iota(jnp.int32, a3.shape, 1)
    for s in (1, 2, 4):
        a_sh = pltpu.roll(a3, s, axis=1)
        u_sh = pltpu.roll(u3, s, axis=1)
        m = row >= s
        u3 = jnp.where(m, a3 * u_sh + u3, u3)
        a3 = jnp.where(m, a3 * a_sh, a3)
    return a3, u3


def _group_cumsum(x3):
    row = lax.broadcasted_iota(jnp.int32, x3.shape, 1)
    for s in (1, 2, 4):
        x3 = jnp.where(row >= s, x3 + pltpu.roll(x3, s, axis=1), x3)
    return x3


def _linear_scan(a, u, h0):
    r, l = a.shape
    g = r // SUBLANES
    a3, u3 = _group_scan(a.reshape(g, SUBLANES, l), u.reshape(g, SUBLANES, l))
    carry = h0
    out = []
    for i in range(g):
        hg = a3[i] * carry + u3[i]
        out.append(hg)
        carry = hg[SUBLANES - 1:SUBLANES]
    return jnp.concatenate(out, axis=0)


def _chunk_cumsum(x):
    r, l = x.shape
    g = r // SUBLANES
    per_chunk = CHUNK // SUBLANES
    x3 = _group_cumsum(x.reshape(g, SUBLANES, l))
    out = []
    carry = None
    for i in range(g):
        xg = x3[i] if i % per_chunk == 0 else x3[i] + carry
        out.append(xg)
        carry = xg[SUBLANES - 1:SUBLANES]
    return jnp.concatenate(out, axis=0)


def _lru_block(xc_n, pre, bgx, bga, logsig):
    gx = _sigmoid(pre[:, :LRU_BW] + bgx)
    ga = _sigmoid(pre[:, LRU_BW:] + bga)
    log_a = (LRU_C * ga) * logsig
    a = jnp.exp(log_a)
    om = 1.0 - a * a
    mult = jnp.where(om > 0.0, om * lax.rsqrt(om), 0.0)
    u = mult * (gx * xc_n)
    return a, u


def _gla_token_step(qs, k, eg, v, s0):
    eye = (lax.broadcasted_iota(jnp.int32, (DK, DK), 0)
           == lax.broadcasted_iota(jnp.int32, (DK, DK), 1))

    def to_col(row):
        return jnp.sum(jnp.where(eye, jnp.broadcast_to(row, (DK, DK)), 0.0),
                       axis=-1, keepdims=True)

    qk = jnp.sum(qs * k, axis=-1, keepdims=True)
    qg = jnp.broadcast_to(qs * eg, (SUBLANES, DK)).astype(_BF)
    o = _dot(qg, s0.astype(_BF))[0:1] + qk * v
    return o, to_col(eg) * s0 + to_col(k) * v


def _ada_kernel(c_ref, w_ref, b_ref, o_ref):
    c = c_ref[...]
    o_ref[...] = _dot(_silu(c).astype(_BF), w_ref[...].astype(_BF)) + b_ref[...]


def _ada(c_all, w_ada, b_ada):
    n = c_all.shape[0]
    return pl.pallas_call(
        _ada_kernel,
        out_shape=jax.ShapeDtypeStruct((n, 3 * D), _F32),
        grid=(3,),
        in_specs=[
            pl.BlockSpec((n, D), lambda j: (0, 0)),
            pl.BlockSpec((D, D), lambda j: (0, j)),
            pl.BlockSpec((1, D), lambda j: (0, j)),
        ],
        out_specs=pl.BlockSpec((n, D), lambda j: (0, j)),
        compiler_params=pltpu.CompilerParams(dimension_semantics=("arbitrary",)),
        name="ada",
    )(c_all, w_ada, b_ada)


def _stage_blocks(n, src_block, store_block, stage, sem):
    def copy(i, slot):
        return pltpu.make_async_copy(src_block(i), stage.at[slot], sem.at[slot])

    copy(0, 0).start()

    def body(i, carry):
        slot = lax.rem(i, 2)

        @pl.when(i + 1 < n)
        def _():
            copy(i + 1, 1 - slot).start()

        copy(i, slot).wait()
        store_block(i, stage[slot].astype(_BF))
        return carry

    lax.fori_loop(0, n, body, 0)


def _stage_in_proj(wt_hbm, wmain_s, stage, sem):
    per_group = D // STAGE_ROWS

    def block(g, c):
        row0 = g * D + (g // (N_GROUPS - 1)) * RANK + c * STAGE_ROWS
        return wt_hbm.at[pl.ds(pl.multiple_of(row0, SUBLANES), STAGE_ROWS), :]

    def copy(g, c):
        return pltpu.make_async_copy(block(g, c), stage.at[c % 2], sem.at[c % 2])

    copy(0, 0).start()

    def body(g, carry):
        for c in range(per_group):
            if c + 1 < per_group:
                copy(g, c + 1).start()
            else:
                @pl.when(g + 1 < N_GROUPS)
                def _():
                    copy(g + 1, 0).start()
            copy(g, c).wait()
            wmain_s[g, :, c * STAGE_ROWS:(c + 1) * STAGE_ROWS] = stage[c % 2].T.astype(_BF)
        return carry

    lax.fori_loop(0, N_GROUPS, body, 0)


def _prompt_kernel(xn_ref, xp_ref, adan_ref, adap_ref, gnorm_ref, wt_hbm, wout_hbm,
                   wglr_ref, convw_ref, convb_ref, wgate_ref, bgx_ref, bga_ref, lam_ref, wg2_ref,
                   bg2_ref, ggla_ref, gfin_ref, sqs_ref, sk_ref, seg_ref, sv_ref, ss0_ref,
                   y_ref, hlast_ref, convout_ref, sout_ref, so_ref, ssnew_ref,
                   wmain_s, wout_s, wglr_s, wg2_s, wgate_s, stage, sem,
                   hn_s, xa_s, za_s, xc_s, pre_s, q_s, k_s, v_s, zg_s, b_s, mix_s, yacc_s, ctail_s,
                   h_s, st_s, *, tm, nt, n_tiles, s_per_step, s_blocks):
    i = pl.program_id(0)
    t = lax.rem(jnp.minimum(i, n_tiles - 1), nt)
    t_next = lax.rem(jnp.minimum(i + 1, n_tiles - 1), nt)
    slot = lax.rem(i, 2)
    half = D // 2
    quarter = D // 4

    def front_norm(x_ref, ada_ref):
        x = x_ref[0]
        hn = (_rms(x) * gnorm_ref[...]) * (1.0 + ada_ref[0, 1:2, :]) + ada_ref[0, 0:1, :]
        hn_s[...] = hn.astype(_BF)

    def front_xa():
        xa_s[SUBLANES:SUBLANES + tm, :] = _dot(hn_s[...], wmain_s[0])

    def front_za():
        za_s[...] = _dot(hn_s[...], wmain_s[1])

    def front_conv():
        xe = xa_s[...]
        xc = convb_ref[...]
        for j in range(CONV_W):
            lag = CONV_W - 1 - j
            xj = xe if lag == 0 else pltpu.roll(xe, lag, axis=0)
            xc = xc + xj[SUBLANES:, :] * convw_ref[j:j + 1, :]
        xc_s[...] = xc

    def front_gates():
        for n in range(LRU_BLOCKS):
            pre_s[:, 2 * n * LRU_BW:2 * (n + 1) * LRU_BW] = _dot(
                xc_s[:, n * LRU_BW:(n + 1) * LRU_BW].astype(_BF), wgate_s[n])

    @pl.when(i == 0)
    def _():
        def out_rows(j):
            return pl.ds(pl.multiple_of(j * STAGE_ROWS, STAGE_ROWS), STAGE_ROWS)

        def store_out(j, val):
            wout_s[out_rows(j), :] = val

        _stage_in_proj(wt_hbm, wmain_s, stage, sem)
        _stage_blocks(D_MIX // STAGE_ROWS, lambda j: wout_hbm.at[out_rows(j), :], store_out,
                      stage, sem)
        wglr_s[...] = wglr_ref[...].astype(_BF)
        wg2_s[...] = wg2_ref[...].astype(_BF)
        wgate_s[...] = wgate_ref[...].astype(_BF)
        mix_s[...] = jnp.zeros_like(mix_s)
        xa_s[0:SUBLANES, :] = jnp.zeros((SUBLANES, D), _F32)
        front_norm(xp_ref, adap_ref)
        front_xa()
        front_za()
        front_conv()
        front_gates()

    @pl.when(t == 0)
    def _():
        h_s[...] = jnp.zeros_like(h_s)
        st_s[...] = jnp.zeros_like(st_s)

    for j in range(s_per_step):
        srow = slice(j, j + 1)
        for h in range(HEADS):
            lk = slice(h * DK, (h + 1) * DK)
            lv = slice(h * DV, (h + 1) * DV)
            o, snew = _gla_token_step(sqs_ref[0, srow, lk], sk_ref[0, srow, lk],
                                      seg_ref[0, srow, lk], sv_ref[0, srow, lv], ss0_ref[j, h])
            so_ref[0, srow, lv] = o
            ssnew_ref[j, h] = snew

    def out_slab(j):
        def run():
            cols = slice(j * quarter, (j + 1) * quarter)
            o = _dot(mix_s[1 - slot], wout_s[:, cols])
            yacc_s[:, cols] = xp_ref[0, :, cols] + adap_ref[0, 2:3, cols] * o
        return run

    def out_norm():
        y_ref[0] = _rms(yacc_s[...]) * gfin_ref[...]

    def proj(dst, dst_cols, group, w_cols):
        def run():
            dst[:, dst_cols] = _dot(hn_s[...], wmain_s[group, :, w_cols])
        return run

    def decay():
        glr = lax.dot_general(hn_s[...], wglr_s[...], _NT,
                              preferred_element_type=_F32)
        la = _log_sigmoid(_dot(glr.astype(_BF), wg2_s[...]) + bg2_ref[...]) / TAU
        b_s[...] = _chunk_cumsum(la)

    lo, hi, full = slice(0, half), slice(half, D), slice(0, D_QK)
    tasks = [
        (out_slab(0), proj(q_s, full, 2, lo)),
        (out_slab(1), proj(k_s, full, 2, hi)),
        (out_slab(2), decay),
        (out_slab(3), out_norm),
        (proj(v_s, lo, 3, lo),),
        (proj(v_s, hi, 3, hi),),
        (proj(zg_s, lo, 4, lo),),
        (proj(zg_s, hi, 4, hi),),
    ]

    logsig = _log_sigmoid(lam_ref[...])
    for n in range(LRU_BLOCKS):
        ln = slice(n * LRU_BW, (n + 1) * LRU_BW)
        a, u = _lru_block(xc_s[:, ln], pre_s[:, 2 * n * LRU_BW:2 * (n + 1) * LRU_BW],
                          bgx_ref[:, ln], bga_ref[:, ln], logsig[:, ln])
        h = _linear_scan(a, u, h_s[:, ln])
        h_s[:, ln] = h[tm - 1:tm]
        mix_s[slot, :, ln] = (h * _silu(za_s[:, ln])).astype(_BF)
        for task in tasks[n]:
            task()

    tril = (lax.broadcasted_iota(jnp.int32, (CHUNK, CHUNK), 0)
            >= lax.broadcasted_iota(jnp.int32, (CHUNK, CHUNK), 1))

    def gla_chunk(c):
        rows = slice(c * CHUNK, (c + 1) * CHUNK)
        for h in range(HEADS):
            lk = slice(h * DK, (h + 1) * DK)
            lv = slice(h * DV, (h + 1) * DV)
            b = b_s[rows, lk]
            bl = b[CHUNK - 1:CHUNK]
            q = q_s[rows, lk] * (DK ** -0.5)
            k = k_s[rows, lk]
            v = v_s[rows, lv]
            qi = (q * jnp.exp(b)).astype(_BF)
            ki = (k * jnp.exp(-b)).astype(_BF)
            kd = (k * jnp.exp(bl - b)).astype(_BF)
            vb = v.astype(_BF)
            att = lax.dot_general(qi, ki, _NT, preferred_element_type=_F32)
            att = jnp.where(tril, att, 0.0).astype(_BF)
            st = st_s[h]
            o = _dot(att, vb) + lax.dot_general(qi, st.astype(_BF), _NT,
                                                preferred_element_type=_F32)
            st_s[h] = st * jnp.exp(bl) + _dot(v.T.astype(_BF), kd)
            o = _rms(o) * ggla_ref[:, lv]
            mix_s[slot, rows, D + h * DV:D + (h + 1) * DV] = (
                o * _silu(zg_s[rows, lv])).astype(_BF)

    def next_tail():
        tail = xa_s[tm:tm + SUBLANES, :]
        ctail_s[...] = tail
        xa_s[0:SUBLANES, :] = jnp.where(t_next != 0, tail, 0.0)

    @pl.when(i >= 0)
    def _():
        next_tail()
        front_norm(xn_ref, adan_ref)
        gla_chunk(0)
        front_xa()
        gla_chunk(1)
        front_za()
        front_conv()
        gla_chunk(2)
        front_gates()
        gla_chunk(3)

    @pl.when((t == nt - 1) & (i < n_tiles))
    def _():
        hlast_ref[0] = h_s[...]
        convout_ref[0] = ctail_s[SUBLANES - (CONV_W - 1):SUBLANES, :]
        for h in range(HEADS):
            sout_ref[0, h] = st_s[h].T


def _const_spec(shape):
    nd = len(shape)
    return pl.BlockSpec(shape, lambda i: (0,) * nd, pipeline_mode=pl.Buffered(1))


def _prompt(x, ada3, gnorm, w_in_t, w_out2d, wglr, convw, convb, wgate, bgx, bga, lam,
            wg2, bg2, ggla, gfin, sqs, sk, seg, sv, ss0, tm=TM):
    bsz, seq, _ = x.shape
    nt = seq // tm
    n_tiles = bsz * nt
    ns = ss0.shape[0]
    s_per_step = -(-ns // n_tiles)
    s_blocks = ns // s_per_step
    assert tm // CHUNK == 4, "the step body spreads the next tile's input side over 4 GLA chunks"
    assert s_blocks * s_per_step == ns and s_blocks <= n_tiles + 1
    kern = functools.partial(_prompt_kernel, tm=tm, nt=nt, n_tiles=n_tiles,
                             s_per_step=s_per_step, s_blocks=s_blocks)
    out_shape = (
        jax.ShapeDtypeStruct((bsz, seq, D), _F32),
        jax.ShapeDtypeStruct((bsz, 1, D), _F32),
        jax.ShapeDtypeStruct((bsz, CONV_W - 1, D), _F32),
        jax.ShapeDtypeStruct((bsz, HEADS, DK, DV), _F32),
        jax.ShapeDtypeStruct((s_blocks, s_per_step, D), _F32),
        jax.ShapeDtypeStruct(ss0.shape, _F32),
    )
    by_step = lambda a: a.reshape(s_blocks, s_per_step, a.shape[-1])
    sqs, sk, seg, sv = by_step(sqs), by_step(sk), by_step(seg), by_step(sv)
    s_rows = lambda i: (jnp.minimum(i, s_blocks - 1), 0, 0)
    s_block = lambda i: (jnp.minimum(i, s_blocks - 1), 0, 0, 0)
    cur = lambda i: jnp.minimum(i, n_tiles - 1)
    prev = lambda i: jnp.maximum(i - 1, 0)
    nxt = lambda i: jnp.minimum(i + 1, n_tiles - 1)
    tile = lambda f: (lambda i: (f(i) // nt, lax.rem(f(i), nt), 0))
    seq_of = lambda f: (lambda i: (f(i) // nt, 0, 0))
    hbm = pl.BlockSpec(memory_space=pl.ANY)
    in_specs = [
        pl.BlockSpec((1, tm, D), tile(nxt)),
        pl.BlockSpec((1, tm, D), tile(prev)),
        pl.BlockSpec((1, 3, D), seq_of(nxt)),
        pl.BlockSpec((1, 3, D), seq_of(prev)),
        _const_spec((1, D)),
        hbm, hbm,
        _const_spec((RANK_PAD, D)),
        _const_spec((CONV_W, D)),
        _const_spec((1, D)),
        _const_spec((LRU_BLOCKS, LRU_BW, 2 * LRU_BW)),
        _const_spec((1, D)),
        _const_spec((1, D)),
        _const_spec((1, D)),
        _const_spec((RANK_PAD, D_QK)),
        _const_spec((1, D_QK)),
        _const_spec((1, D)),
        _const_spec((1, D)),
        pl.BlockSpec((1, s_per_step, D_QK), s_rows),
        pl.BlockSpec((1, s_per_step, D_QK), s_rows),
        pl.BlockSpec((1, s_per_step, D_QK), s_rows),
        pl.BlockSpec((1, s_per_step, D), s_rows),
        pl.BlockSpec((s_per_step, HEADS, DK, DV), s_block),
    ]
    out_specs = (
        pl.BlockSpec((1, tm, D), tile(prev)),
        pl.BlockSpec((1, 1, D), seq_of(cur)),
        pl.BlockSpec((1, CONV_W - 1, D), seq_of(cur)),
        pl.BlockSpec((1, HEADS, DK, DV), lambda i: (cur(i) // nt, 0, 0, 0)),
        pl.BlockSpec((1, s_per_step, D), s_rows),
        pl.BlockSpec((s_per_step, HEADS, DK, DV), s_block),
    )
    scratch = [
        pltpu.VMEM((N_GROUPS, D, D), _BF),
        pltpu.VMEM((D_MIX, D), _BF),
        pltpu.VMEM((RANK_PAD, D), _BF),
        pltpu.VMEM((RANK_PAD, D_QK), _BF),
        pltpu.VMEM((LRU_BLOCKS, LRU_BW, 2 * LRU_BW), _BF),
        pltpu.VMEM((2, STAGE_ROWS, D), _F32),
        pltpu.SemaphoreType.DMA((2,)),
        pltpu.VMEM((tm, D), _BF),
        pltpu.VMEM((tm + SUBLANES, D), _F32),
        pltpu.VMEM((tm, D), _F32),
        pltpu.VMEM((tm, D), _F32),
        pltpu.VMEM((tm, 2 * D), _F32),
        pltpu.VMEM((tm, D_QK), _F32),
        pltpu.VMEM((tm, D_QK), _F32),
        pltpu.VMEM((tm, D), _F32),
        pltpu.VMEM((tm, D), _F32),
        pltpu.VMEM((tm, D_QK), _F32),
        pltpu.VMEM((2, tm, D_MIX), _BF),
        pltpu.VMEM((tm, D), _F32),
        pltpu.VMEM((SUBLANES, D), _F32),
        pltpu.VMEM((1, D), _F32),
        pltpu.VMEM((HEADS, DV, DK), _F32),
    ]
    return pl.pallas_call(
        kern,
        out_shape=out_shape,
        grid=(n_tiles + 1,),
        in_specs=in_specs,
        out_specs=out_specs,
        scratch_shapes=scratch,
        compiler_params=pltpu.CompilerParams(
            dimension_semantics=("arbitrary",),
            vmem_limit_bytes=VMEM_LIMIT),
        name="prompt_layer",
    )(x, x, ada3, ada3, gnorm, w_in_t, w_out2d, wglr, convw, convb, wgate, bgx, bga, lam,
      wg2, bg2, ggla, gfin, sqs, sk, seg, sv, ss0)


def _sample_in_kernel(x_ref, shift_ref, scale_ref, h0_ref, c0_ref, c1_ref, c2_ref,
                      gnorm_ref, wt_ref, wglr_ref, convw_ref, convb_ref, wgate_ref,
                      bgx_ref, bga_ref, lam_ref, wg2_ref, bg2_ref,
                      hnew_ref, xa_ref, ylru_ref, qs_ref, k_ref, eg_ref, v_ref, zg_ref):
    x = x_ref[...]
    hn = (_rms(x) * gnorm_ref[...]) * (1.0 + scale_ref[...]) + shift_ref[...]
    hnb = hn.astype(_BF)

    def proj(c0, width):
        return lax.dot_general(hnb, wt_ref[c0:c0 + width, :].astype(_BF), _NT,
                               preferred_element_type=_F32)

    xa = proj(0, D)
    za = proj(D, D)
    qs_ref[...] = proj(2 * D, D_QK) * (DK ** -0.5)
    k_ref[...] = proj(2 * D + D_QK, D_QK)
    v_ref[...] = proj(3 * D, D)
    zg_ref[...] = proj(O_GLR + RANK, D)
    glr = lax.dot_general(hnb, wglr_ref[...].astype(_BF), _NT, preferred_element_type=_F32)
    la = _log_sigmoid(_dot(glr.astype(_BF), wg2_ref[...].astype(_BF)) + bg2_ref[...]) / TAU
    eg_ref[...] = jnp.exp(la)
    xa_ref[...] = xa

    xc = convb_ref[...]
    for j, cj in enumerate((c0_ref[...], c1_ref[...], c2_ref[...], xa)):
        xc = xc + cj * convw_ref[j:j + 1, :]
    xcb = xc.astype(_BF)
    logsig = _log_sigmoid(lam_ref[...])
    for n in range(LRU_BLOCKS):
        ln = slice(n * LRU_BW, (n + 1) * LRU_BW)
        pre = _dot(xcb[:, ln], wgate_ref[n].astype(_BF))
        a, u = _lru_block(xc[:, ln], pre, bgx_ref[:, ln], bga_ref[:, ln], logsig[:, ln])
        h = a * h0_ref[:, ln] + u
        hnew_ref[:, ln] = h
        ylru_ref[:, ln] = h * _silu(za[:, ln])


def _sample_in(x, shift, scale, h0, c0, c1, c2, gnorm, w_in_t, wglr, convw, convb, wgate,
               bgx, bga, lam, wg2, bg2):
    n = x.shape[0]
    f = lambda w: jax.ShapeDtypeStruct((n, w), _F32)
    return pl.pallas_call(
        _sample_in_kernel,
        out_shape=(f(D), f(D), f(D), f(D_QK), f(D_QK), f(D_QK), f(D), f(D)),
        compiler_params=pltpu.CompilerParams(vmem_limit_bytes=VMEM_LIMIT),
        name="sample_in",
    )(x, shift, scale, h0, c0, c1, c2, gnorm, w_in_t, wglr, convw, convb, wgate, bgx, bga,
      lam, wg2, bg2)


def _sample_out_kernel(x_ref, gate_ref, o_ref, zg_ref, ylru_ref, ggla_ref, wout_ref, gfin_ref,
                       y_ref):
    parts = [ylru_ref[...].astype(_BF)]
    for h in range(HEADS):
        lv = slice(h * DV, (h + 1) * DV)
        o = _rms(o_ref[:, lv]) * ggla_ref[:, lv]
        parts.append((o * _silu(zg_ref[:, lv])).astype(_BF))
    mix = jnp.concatenate(parts, axis=-1)
    y = x_ref[...] + gate_ref[...] * _dot(mix, wout_ref[...].astype(_BF))
    y_ref[...] = _rms(y) * gfin_ref[...]


def _sample_out(x, gate, o, zg, ylru, ggla, w_out2d, gfin):
    return pl.pallas_call(
        _sample_out_kernel,
        out_shape=jax.ShapeDtypeStruct(x.shape, _F32),
        compiler_params=pltpu.CompilerParams(vmem_limit_bytes=VMEM_LIMIT),
        name="sample_out",
    )(x, gate, o, zg, ylru, ggla, w_out2d, gfin)


def kernel(x_prompt, x_sample, state_lru_h, state_lru_conv, state_gla, c_prompt, c_sample,
           g_norm, w_ada, b_ada, w_in, conv_w, conv_b, w_gate_x, b_gate_x, w_gate_a, b_gate_a,
           lru_lambda, w_gla_g2, b_gla_g2, g_gla_norm, w_out, g_final):
    depth = g_norm.shape[0]
    assert depth == 1, "single-layer trunk"
    bp = x_prompt.shape[0]
    ns = x_sample.shape[0]

    w_in_t = jnp.swapaxes(w_in, 1, 2).reshape(w_in.shape[-1], D)
    w_out2d = w_out.reshape(D_MIX, D)
    wglr = jnp.pad(w_in_t[O_GLR:O_GLR + RANK, :], ((0, RANK_PAD - RANK), (0, 0)))
    wg2 = jnp.pad(w_gla_g2.reshape(RANK, D_QK), ((0, RANK_PAD - RANK), (0, 0)))
    wgate = jnp.concatenate([w_gate_x.reshape(LRU_BLOCKS, LRU_BW, LRU_BW),
                             w_gate_a.reshape(LRU_BLOCKS, LRU_BW, LRU_BW)], axis=-1)
    row = lambda a: a.reshape(1, -1)
    gnorm, convb = row(g_norm), row(conv_b)
    bgx, bga, lam = row(b_gate_x), row(b_gate_a), row(lru_lambda)
    bg2, ggla, gfin = row(b_gla_g2), row(g_gla_norm), row(g_final)
    convw = conv_w.reshape(CONV_W, D)

    ada = _ada(jnp.concatenate([c_prompt, c_sample], axis=0), w_ada.reshape(D, 3 * D),
               row(b_ada))
    ada_p = ada[:bp].reshape(bp, 3, D)
    ada_s = ada[bp:]

    xs = x_sample.reshape(ns, D)
    conv0 = state_lru_conv.reshape(ns, CONV_W - 1, D)
    hs, xa, ylru, qs, ks, eg, vs, zg = _sample_in(
        xs, ada_s[:, :D], ada_s[:, D:2 * D], state_lru_h.reshape(ns, D),
        conv0[:, 0], conv0[:, 1], conv0[:, 2],
        gnorm, w_in_t, wglr, convw, convb, wgate, bgx, bga, lam, wg2, bg2)

    yp, hp, cp, sp, o, ss = _prompt(x_prompt, ada_p, gnorm, w_in_t, w_out2d, wglr, convw,
                                    convb, wgate, bgx, bga, lam, wg2, bg2, ggla, gfin,
                                    qs, ks, eg, vs, state_gla.reshape(ns, HEADS, DK, DV))

    ys = _sample_out(xs, ada_s[:, 2 * D:], o.reshape(ns, D), zg, ylru, ggla, w_out2d, gfin)
    cs = jnp.stack([conv0[:, 1], conv0[:, 2], xa], axis=1)

    return (yp, ys.reshape(ns, 1, D), hp.reshape(1, bp, D), cp[None], sp[None],
            hs[None], cs[None], ss[None])
```

```python
import functools

import jax
import jax.numpy as jnp
from jax import lax
from jax.experimental import pallas as pl
from jax.experimental.pallas import tpu as pltpu

EPS = 1e-6
D = 1024
LRU_BLOCKS = 8
LRU_BW = D // LRU_BLOCKS
LRU_C = 8.0
CONV_W = 4
HEADS = 4
DV = D // HEADS
DK = DV // 2
RANK = 16
RANK_PAD = 128
TAU = 16.0
CHUNK = 64
D_QK = HEADS * DK
D_MIX = 2 * D
O_GLR = 4 * D
N_GROUPS = 5

SUBLANES = 8
TM = 256
STAGE_ROWS = 256
VMEM_LIMIT = 56 * 1024 * 1024
NEG_LOG2E = -1.4426950408889634

_BF = jnp.bfloat16
_F32 = jnp.float32

_NT = (((1,), (1,)), ((), ()))


def _sigmoid(x):
    return 1.0 / (1.0 + jnp.exp2(x * NEG_LOG2E))


def _silu(x):
    return x * _sigmoid(x)


def _log_sigmoid(x):
    return jnp.minimum(x, 0.0) - jnp.log(1.0 + jnp.exp2(jnp.abs(x) * NEG_LOG2E))


def _rms(x):
    return x * lax.rsqrt(jnp.mean(x * x, axis=-1, keepdims=True) + EPS)


def _dot(a, b):
    return jnp.dot(a, b, preferred_element_type=_F32)


def _group_scan(a3, u3):
    row = lax.broadcasted_iota(jnp.int32, a3.shape, 1)
    for s in (1, 2, 4):
        a_sh = pltpu.roll(a3, s, axis=1)
        u_sh = pltpu.roll(u3, s, axis=1)
        m = row >= s
        u3 = jnp.where(m, a3 * u_sh + u3, u3)
        a3 = jnp.where(m, a3 * a_sh, a3)
    return a3, u3


def _group_cumsum(x3):
    row = lax.broadcasted_iota(jnp.int32, x3.shape, 1)
    for s in (1, 2, 4):
        x3 = jnp.where(row >= s, x3 + pltpu.roll(x3, s, axis=1), x3)
    return x3


def _linear_scan(a, u, h0):
    r, l = a.shape
    g = r // SUBLANES
    a3, u3 = _group_scan(a.reshape(g, SUBLANES, l), u.reshape(g, SUBLANES, l))
    carry = h0
    out = []
    for i in range(g):
        hg = a3[i] * carry + u3[i]
        out.append(hg)
        carry = hg[SUBLANES - 1:SUBLANES]
    return jnp.concatenate(out, axis=0)


def _chunk_cumsum(x):
    r, l = x.shape
    g = r // SUBLANES
    per_chunk = CHUNK // SUBLANES
    x3 = _group_cumsum(x.reshape(g, SUBLANES, l))
    out = []
    carry = None
    for i in range(g):
        xg = x3[i] if i % per_chunk == 0 else x3[i] + carry
        out.append(xg)
        carry = xg[SUBLANES - 1:SUBLANES]
    return jnp.concatenate(out, axis=0)


def _lru_block(xc_n, pre, bgx, bga, c_logsig):
    gx = _sigmoid(pre[:, :LRU_BW] + bgx)
    ga = _sigmoid(pre[:, LRU_BW:] + bga)
    log_a = ga * c_logsig
    a = jnp.exp(log_a)
    om = 1.0 - a * a
    mult = jnp.where(om > 0.0, om * lax.rsqrt(om), 0.0)
    u = mult * (gx * xc_n)
    return a, u


def _gla_token_step(qs, k, eg, v, s0):
    eye = (lax.broadcasted_iota(jnp.int32, (DK, DK), 0)
           == lax.broadcasted_iota(jnp.int32, (DK, DK), 1))

    def to_col(row):
        return jnp.sum(jnp.where(eye, jnp.broadcast_to(row, (DK, DK)), 0.0),
                       axis=-1, keepdims=True)

    qk = jnp.sum(qs * k, axis=-1, keepdims=True)
    qg = jnp.broadcast_to(qs * eg, (SUBLANES, DK)).astype(_BF)
    o = _dot(qg, s0.astype(_BF))[0:1] + qk * v
    return o, to_col(eg) * s0 + to_col(k) * v


def _ada_kernel(c_ref, w_ref, b_ref, o_ref):
    c = c_ref[...]
    o_ref[...] = _dot(_silu(c).astype(_BF), w_ref[...].astype(_BF)) + b_ref[...]


def _ada(c_all, w_ada, b_ada):
    n = c_all.shape[0]
    return pl.pallas_call(
        _ada_kernel,
        out_shape=jax.ShapeDtypeStruct((n, 3 * D), _F32),
        grid=(3,),
        in_specs=[
            pl.BlockSpec((n, D), lambda j: (0, 0)),
            pl.BlockSpec((D, D), lambda j: (0, j)),
            pl.BlockSpec((1, D), lambda j: (0, j)),
        ],
        out_specs=pl.BlockSpec((n, D), lambda j: (0, j)),
        compiler_params=pltpu.CompilerParams(dimension_semantics=("arbitrary",)),
        name="ada",
    )(c_all, w_ada, b_ada)


def _stage_blocks(n, src_block, store_block, stage, sem):
    def copy(i, slot):
        return pltpu.make_async_copy(src_block(i), stage.at[slot], sem.at[slot])

    copy(0, 0).start()

    def body(i, carry):
        slot = lax.rem(i, 2)

        @pl.when(i + 1 < n)
        def _():
            copy(i + 1, 1 - slot).start()

        copy(i, slot).wait()
        store_block(i, stage[slot].astype(_BF))
        return carry

    lax.fori_loop(0, n, body, 0)


def _stage_in_proj(wt_hbm, wmain_s, stage, sem):
    per_group = D // STAGE_ROWS

    def block(g, c):
        row0 = g * D + (g // (N_GROUPS - 1)) * RANK + c * STAGE_ROWS
        return wt_hbm.at[pl.ds(pl.multiple_of(row0, SUBLANES), STAGE_ROWS), :]

    def copy(g, c):
        return pltpu.make_async_copy(block(g, c), stage.at[c % 2], sem.at[c % 2])

    copy(0, 0).start()

    def body(g, carry):
        for c in range(per_group):
            if c + 1 < per_group:
                copy(g, c + 1).start()
            else:
                @pl.when(g + 1 < N_GROUPS)
                def _():
                    copy(g + 1, 0).start()
            copy(g, c).wait()
            wmain_s[g, :, c * STAGE_ROWS:(c + 1) * STAGE_ROWS] = stage[c % 2].T.astype(_BF)
        return carry

    lax.fori_loop(0, N_GROUPS, body, 0)


def _prompt_kernel(xn_ref, xp_ref, adan_ref, adap_ref, gnorm_ref, wt_hbm, wout_hbm,
                   wglr_ref, convw_ref, convb_ref, wgate_ref, bgx_ref, bga_ref, lam_ref, wg2_ref,
                   bg2_ref, ggla_ref, gfin_ref, sqs_ref, sk_ref, seg_ref, sv_ref, ss0_ref,
                   y_ref, hlast_ref, convout_ref, sout_ref, so_ref, ssnew_ref,
                   wmain_s, wout_s, wglr_s, wg2_s, wgate_s, stage, sem,
                   hn_s, xa_s, za_s, xc_s, pre_s, q_s, k_s, v_s, zg_s, b_s, mix_s, yacc_s, ctail_s,
                   h_s, st_s, *, tm, nt, n_tiles, s_per_step, s_blocks):
    i = pl.program_id(0)
    t = lax.rem(jnp.minimum(i, n_tiles - 1), nt)
    t_next = lax.rem(jnp.minimum(i + 1, n_tiles - 1), nt)
    slot = lax.rem(i, 2)
    half = D // 2
    quarter = D // 4

    def front_norm(x_ref, ada_ref):
        x = x_ref[0]
        gain = gnorm_ref[...] * (1.0 + ada_ref[0, 1:2, :])
        hn_s[...] = (_rms(x) * gain + ada_ref[0, 0:1, :]).astype(_BF)

    def front_xa():
        xa_s[SUBLANES:SUBLANES + tm, :] = _dot(hn_s[...], wmain_s[0])

    def front_za():
        za_s[...] = _dot(hn_s[...], wmain_s[1])

    def front_conv():
        xe = xa_s[...]
        xc = convb_ref[...]
        for j in range(CONV_W):
            lag = CONV_W - 1 - j
            xj = xe if lag == 0 else pltpu.roll(xe, lag, axis=0)
            xc = xc + xj[SUBLANES:, :] * convw_ref[j:j + 1, :]
        xc_s[...] = xc

    def front_gates():
        for n in range(LRU_BLOCKS):
            pre_s[:, 2 * n * LRU_BW:2 * (n + 1) * LRU_BW] = _dot(
                xc_s[:, n * LRU_BW:(n + 1) * LRU_BW].astype(_BF), wgate_s[n])

    @pl.when(i == 0)
    def _():
        def out_rows(j):
            return pl.ds(pl.multiple_of(j * STAGE_ROWS, STAGE_ROWS), STAGE_ROWS)

        def store_out(j, val):
            wout_s[out_rows(j), :] = val

        _stage_in_proj(wt_hbm, wmain_s, stage, sem)
        _stage_blocks(D_MIX // STAGE_ROWS, lambda j: wout_hbm.at[out_rows(j), :], store_out,
                      stage, sem)
        wglr_s[...] = wglr_ref[...].astype(_BF)
        wg2_s[...] = wg2_ref[...].astype(_BF)
        wgate_s[...] = wgate_ref[...].astype(_BF)
        mix_s[...] = jnp.zeros_like(mix_s)
        xa_s[0:SUBLANES, :] = jnp.zeros((SUBLANES, D), _F32)
        front_norm(xp_ref, adap_ref)
        front_xa()
        front_za()
        front_conv()
        front_gates()

    @pl.when(t == 0)
    def _():
        h_s[...] = jnp.zeros_like(h_s)
        st_s[...] = jnp.zeros_like(st_s)

    for j in range(s_per_step):
        srow = slice(j, j + 1)
        for h in range(HEADS):
            lk = slice(h * DK, (h + 1) * DK)
            lv = slice(h * DV, (h + 1) * DV)
            o, snew = _gla_token_step(sqs_ref[0, srow, lk], sk_ref[0, srow, lk],
                                      seg_ref[0, srow, lk], sv_ref[0, srow, lv], ss0_ref[j, h])
            so_ref[0, srow, lv] = o
            ssnew_ref[j, h] = snew

    def out_slab(j):
        def run():
            cols = slice(j * quarter, (j + 1) * quarter)
            o = _dot(mix_s[1 - slot], wout_s[:, cols])
            yacc_s[:, cols] = xp_ref[0, :, cols] + adap_ref[0, 2:3, cols] * o
        return run

    def out_norm():
        y_ref[0] = _rms(yacc_s[...]) * gfin_ref[...]

    def proj(dst, dst_cols, group, w_cols):
        def run():
            dst[:, dst_cols] = _dot(hn_s[...], wmain_s[group, :, w_cols])
        return run

    def decay():
        glr = lax.dot_general(hn_s[...], wglr_s[...], _NT,
                              preferred_element_type=_F32)
        la = _log_sigmoid(_dot(glr.astype(_BF), wg2_s[...]) + bg2_ref[...]) / TAU
        b_s[...] = _chunk_cumsum(la)

    lo, hi, full = slice(0, half), slice(half, D), slice(0, D_QK)
    tasks = [
        (out_slab(0), proj(q_s, full, 2, lo)),
        (out_slab(1), proj(k_s, full, 2, hi)),
        (out_slab(2), decay),
        (out_slab(3), out_norm),
        (proj(v_s, lo, 3, lo),),
        (proj(v_s, hi, 3, hi),),
        (proj(zg_s, lo, 4, lo),),
        (proj(zg_s, hi, 4, hi),),
    ]

    logsig = LRU_C * _log_sigmoid(lam_ref[...])
    for n in range(LRU_BLOCKS):
        ln = slice(n * LRU_BW, (n + 1) * LRU_BW)
        a, u = _lru_block(xc_s[:, ln], pre_s[:, 2 * n * LRU_BW:2 * (n + 1) * LRU_BW],
                          bgx_ref[:, ln], bga_ref[:, ln], logsig[:, ln])
        h = _linear_scan(a, u, h_s[:, ln])
        h_s[:, ln] = h[tm - 1:tm]
        mix_s[slot, :, ln] = (h * _silu(za_s[:, ln])).astype(_BF)
        for task in tasks[n]:
            task()

    tril = (lax.broadcasted_iota(jnp.int32, (CHUNK, CHUNK), 0)
            >= lax.broadcasted_iota(jnp.int32, (CHUNK, CHUNK), 1))

    def gla_chunk(c):
        rows = slice(c * CHUNK, (c + 1) * CHUNK)
        for h in range(HEADS):
            lk = slice(h * DK, (h + 1) * DK)
            lv = slice(h * DV, (h + 1) * DV)
            b = b_s[rows, lk]
            bl = b[CHUNK - 1:CHUNK]
            q = q_s[rows, lk] * (DK ** -0.5)
            k = k_s[rows, lk]
            v = v_s[rows, lv]
            qi = (q * jnp.exp(b)).astype(_BF)
            ki = (k * jnp.exp(-b)).astype(_BF)
            kd = (k * jnp.exp(bl - b)).astype(_BF)
            vb = v.astype(_BF)
            att = lax.dot_general(qi, ki, _NT, preferred_element_type=_F32)
            att = jnp.where(tril, att, 0.0).astype(_BF)
            st = st_s[h]
            o = _dot(att, vb) + lax.dot_general(qi, st.astype(_BF), _NT,
                                                preferred_element_type=_F32)
            st_s[h] = st * jnp.exp(bl) + _dot(v.T.astype(_BF), kd)
            o = _rms(o) * ggla_ref[:, lv]
            mix_s[slot, rows, D + h * DV:D + (h + 1) * DV] = (
                o * _silu(zg_s[rows, lv])).astype(_BF)

    def next_tail():
        tail = xa_s[tm:tm + SUBLANES, :]
        ctail_s[...] = tail
        xa_s[0:SUBLANES, :] = jnp.where(t_next != 0, tail, 0.0)

    next_tail()
    front_norm(xn_ref, adan_ref)
    gla_chunk(0)
    front_xa()
    gla_chunk(1)
    front_za()
    front_conv()
    gla_chunk(2)
    front_gates()
    gla_chunk(3)

    @pl.when((t == nt - 1) & (i < n_tiles))
    def _():
        hlast_ref[0] = h_s[...]
        convout_ref[0] = ctail_s[SUBLANES - (CONV_W - 1):SUBLANES, :]
        for h in range(HEADS):
            sout_ref[0, h] = st_s[h].T


def _const_spec(shape):
    nd = len(shape)
    return pl.BlockSpec(shape, lambda i: (0,) * nd, pipeline_mode=pl.Buffered(1))


def _prompt(x, ada3, gnorm, w_in_t, w_out2d, wglr, convw, convb, wgate, bgx, bga, lam,
            wg2, bg2, ggla, gfin, sqs, sk, seg, sv, ss0, tm=TM):
    bsz, seq, _ = x.shape
    nt = seq // tm
    n_tiles = bsz * nt
    ns = ss0.shape[0]
    s_per_step = -(-ns // n_tiles)
    s_blocks = ns // s_per_step
    assert tm // CHUNK == 4, "the step body spreads the next tile's input side over 4 GLA chunks"
    assert s_blocks * s_per_step == ns and s_blocks <= n_tiles + 1
    kern = functools.partial(_prompt_kernel, tm=tm, nt=nt, n_tiles=n_tiles,
                             s_per_step=s_per_step, s_blocks=s_blocks)
    out_shape = (
        jax.ShapeDtypeStruct((bsz, seq, D), _F32),
        jax.ShapeDtypeStruct((bsz, 1, D), _F32),
        jax.ShapeDtypeStruct((bsz, CONV_W - 1, D), _F32),
        jax.ShapeDtypeStruct((bsz, HEADS, DK, DV), _F32),
        jax.ShapeDtypeStruct((s_blocks, s_per_step, D), _F32),
        jax.ShapeDtypeStruct(ss0.shape, _F32),
    )
    by_step = lambda a: a.reshape(s_blocks, s_per_step, a.shape[-1])
    sqs, sk, seg, sv = by_step(sqs), by_step(sk), by_step(seg), by_step(sv)
    s_rows = lambda i: (jnp.minimum(i, s_blocks - 1), 0, 0)
    s_block = lambda i: (jnp.minimum(i, s_blocks - 1), 0, 0, 0)
    cur = lambda i: jnp.minimum(i, n_tiles - 1)
    prev = lambda i: jnp.maximum(i - 1, 0)
    nxt = lambda i: jnp.minimum(i + 1, n_tiles - 1)
    tile = lambda f: (lambda i: (f(i) // nt, lax.rem(f(i), nt), 0))
    seq_of = lambda f: (lambda i: (f(i) // nt, 0, 0))
    hbm = pl.BlockSpec(memory_space=pl.ANY)
    in_specs = [
        pl.BlockSpec((1, tm, D), tile(nxt)),
        pl.BlockSpec((1, tm, D), tile(prev)),
        pl.BlockSpec((1, 3, D), seq_of(nxt)),
        pl.BlockSpec((1, 3, D), seq_of(prev)),
        _const_spec((1, D)),
        hbm, hbm,
        _const_spec((RANK_PAD, D)),
        _const_spec((CONV_W, D)),
        _const_spec((1, D)),
        _const_spec((LRU_BLOCKS, LRU_BW, 2 * LRU_BW)),
        _const_spec((1, D)),
        _const_spec((1, D)),
        _const_spec((1, D)),
        _const_spec((RANK_PAD, D_QK)),
        _const_spec((1, D_QK)),
        _const_spec((1, D)),
        _const_spec((1, D)),
        pl.BlockSpec((1, s_per_step, D_QK), s_rows),
        pl.BlockSpec((1, s_per_step, D_QK), s_rows),
        pl.BlockSpec((1, s_per_step, D_QK), s_rows),
        pl.BlockSpec((1, s_per_step, D), s_rows),
        pl.BlockSpec((s_per_step, HEADS, DK, DV), s_block),
    ]
    out_specs = (
        pl.BlockSpec((1, tm, D), tile(prev)),
        pl.BlockSpec((1, 1, D), seq_of(cur)),
        pl.BlockSpec((1, CONV_W - 1, D), seq_of(cur)),
        pl.BlockSpec((1, HEADS, DK, DV), lambda i: (cur(i) // nt, 0, 0, 0)),
        pl.BlockSpec((1, s_per_step, D), s_rows),
        pl.BlockSpec((s_per_step, HEADS, DK, DV), s_block),
    )
    scratch = [
        pltpu.VMEM((N_GROUPS, D, D), _BF),
        pltpu.VMEM((D_MIX, D), _BF),
        pltpu.VMEM((RANK_PAD, D), _BF),
        pltpu.VMEM((RANK_PAD, D_QK), _BF),
        pltpu.VMEM((LRU_BLOCKS, LRU_BW, 2 * LRU_BW), _BF),
        pltpu.VMEM((2, STAGE_ROWS, D), _F32),
        pltpu.SemaphoreType.DMA((2,)),
        pltpu.VMEM((tm, D), _BF),
        pltpu.VMEM((tm + SUBLANES, D), _F32),
        pltpu.VMEM((tm, D), _F32),
        pltpu.VMEM((tm, D), _F32),
        pltpu.VMEM((tm, 2 * D), _F32),
        pltpu.VMEM((tm, D_QK), _F32),
        pltpu.VMEM((tm, D_QK), _F32),
        pltpu.VMEM((tm, D), _F32),
        pltpu.VMEM((tm, D), _F32),
        pltpu.VMEM((tm, D_QK), _F32),
        pltpu.VMEM((2, tm, D_MIX), _BF),
        pltpu.VMEM((tm, D), _F32),
        pltpu.VMEM((SUBLANES, D), _F32),
        pltpu.VMEM((1, D), _F32),
        pltpu.VMEM((HEADS, DV, DK), _F32),
    ]
    return pl.pallas_call(
        kern,
        out_shape=out_shape,
        grid=(n_tiles + 1,),
        in_specs=in_specs,
        out_specs=out_specs,
        scratch_shapes=scratch,
        compiler_params=pltpu.CompilerParams(
            dimension_semantics=("arbitrary",),
            vmem_limit_bytes=VMEM_LIMIT),
        name="prompt_layer",
    )(x, x, ada3, ada3, gnorm, w_in_t, w_out2d, wglr, convw, convb, wgate, bgx, bga, lam,
      wg2, bg2, ggla, gfin, sqs, sk, seg, sv, ss0)


def _sample_in_kernel(x_ref, shift_ref, scale_ref, h0_ref, c0_ref, c1_ref, c2_ref,
                      gnorm_ref, wt_ref, wglr_ref, convw_ref, convb_ref, wgate_ref,
                      bgx_ref, bga_ref, lam_ref, wg2_ref, bg2_ref,
                      hnew_ref, xa_ref, ylru_ref, qs_ref, k_ref, eg_ref, v_ref, zg_ref):
    x = x_ref[...]
    hn = (_rms(x) * gnorm_ref[...]) * (1.0 + scale_ref[...]) + shift_ref[...]
    hnb = hn.astype(_BF)

    def proj(c0, width):
        return lax.dot_general(hnb, wt_ref[c0:c0 + width, :].astype(_BF), _NT,
                               preferred_element_type=_F32)

    xa = proj(0, D)
    za = proj(D, D)
    qs_ref[...] = proj(2 * D, D_QK) * (DK ** -0.5)
    k_ref[...] = proj(2 * D + D_QK, D_QK)
    v_ref[...] = proj(3 * D, D)
    zg_ref[...] = proj(O_GLR + RANK, D)
    glr = lax.dot_general(hnb, wglr_ref[...].astype(_BF), _NT, preferred_element_type=_F32)
    la = _log_sigmoid(_dot(glr.astype(_BF), wg2_ref[...].astype(_BF)) + bg2_ref[...]) / TAU
    eg_ref[...] = jnp.exp(la)
    xa_ref[...] = xa

    xc = convb_ref[...]
    for j, cj in enumerate((c0_ref[...], c1_ref[...], c2_ref[...], xa)):
        xc = xc + cj * convw_ref[j:j + 1, :]
    xcb = xc.astype(_BF)
    logsig = LRU_C * _log_sigmoid(lam_ref[...])
    for n in range(LRU_BLOCKS):
        ln = slice(n * LRU_BW, (n + 1) * LRU_BW)
        pre = _dot(xcb[:, ln], wgate_ref[n].astype(_BF))
        a, u = _lru_block(xc[:, ln], pre, bgx_ref[:, ln], bga_ref[:, ln], logsig[:, ln])
        h = a * h0_ref[:, ln] + u
        hnew_ref[:, ln] = h
        ylru_ref[:, ln] = h * _silu(za[:, ln])


def _sample_in(x, shift, scale, h0, c0, c1, c2, gnorm, w_in_t, wglr, convw, convb, wgate,
               bgx, bga, lam, wg2, bg2):
    n = x.shape[0]
    f = lambda w: jax.ShapeDtypeStruct((n, w), _F32)
    return pl.pallas_call(
        _sample_in_kernel,
        out_shape=(f(D), f(D), f(D), f(D_QK), f(D_QK), f(D_QK), f(D), f(D)),
        compiler_params=pltpu.CompilerParams(vmem_limit_bytes=VMEM_LIMIT),
        name="sample_in",
    )(x, shift, scale, h0, c0, c1, c2, gnorm, w_in_t, wglr, convw, convb, wgate, bgx, bga,
      lam, wg2, bg2)


def _sample_out_kernel(x_ref, gate_ref, o_ref, zg_ref, ylru_ref, ggla_ref, wout_ref, gfin_ref,
                       y_ref):
    parts = [ylru_ref[...].astype(_BF)]
    for h in range(HEADS):
        lv = slice(h * DV, (h + 1) * DV)
        o = _rms(o_ref[:, lv]) * ggla_ref[:, lv]
        parts.append((o * _silu(zg_ref[:, lv])).astype(_BF))
    mix = jnp.concatenate(parts, axis=-1)
    y = x_ref[...] + gate_ref[...] * _dot(mix, wout_ref[...].astype(_BF))
    y_ref[...] = _rms(y) * gfin_ref[...]


def _sample_out(x, gate, o, zg, ylru, ggla, w_out2d, gfin):
    return pl.pallas_call(
        _sample_out_kernel,
        out_shape=jax.ShapeDtypeStruct(x.shape, _F32),
        compiler_params=pltpu.CompilerParams(vmem_limit_bytes=VMEM_LIMIT),
        name="sample_out",
    )(x, gate, o, zg, ylru, ggla, w_out2d, gfin)


def kernel(x_prompt, x_sample, state_lru_h, state_lru_conv, state_gla, c_prompt, c_sample,
           g_norm, w_ada, b_ada, w_in, conv_w, conv_b, w_gate_x, b_gate_x, w_gate_a, b_gate_a,
           lru_lambda, w_gla_g2, b_gla_g2, g_gla_norm, w_out, g_final):
    depth = g_norm.shape[0]
    assert depth == 1, "single-layer trunk"
    bp = x_prompt.shape[0]
    ns = x_sample.shape[0]

    w_in_t = jnp.swapaxes(w_in, 1, 2).reshape(w_in.shape[-1], D)
    w_out2d = w_out.reshape(D_MIX, D)
    wglr = jnp.pad(w_in_t[O_GLR:O_GLR + RANK, :], ((0, RANK_PAD - RANK), (0, 0)))
    wg2 = jnp.pad(w_gla_g2.reshape(RANK, D_QK), ((0, RANK_PAD - RANK), (0, 0)))
    wgate = jnp.concatenate([w_gate_x.reshape(LRU_BLOCKS, LRU_BW, LRU_BW),
                             w_gate_a.reshape(LRU_BLOCKS, LRU_BW, LRU_BW)], axis=-1)
    row = lambda a: a.reshape(1, -1)
    gnorm, convb = row(g_norm), row(conv_b)
    bgx, bga, lam = row(b_gate_x), row(b_gate_a), row(lru_lambda)
    bg2, ggla, gfin = row(b_gla_g2), row(g_gla_norm), row(g_final)
    convw = conv_w.reshape(CONV_W, D)

    ada = _ada(jnp.concatenate([c_prompt, c_sample], axis=0), w_ada.reshape(D, 3 * D),
               row(b_ada))
    ada_p = ada[:bp].reshape(bp, 3, D)
    ada_s = ada[bp:]

    xs = x_sample.reshape(ns, D)
    conv0 = state_lru_conv.reshape(ns, CONV_W - 1, D)
    hs, xa, ylru, qs, ks, eg, vs, zg = _sample_in(
        xs, ada_s[:, :D], ada_s[:, D:2 * D], state_lru_h.reshape(ns, D),
        conv0[:, 0], conv0[:, 1], conv0[:, 2],
        gnorm, w_in_t, wglr, convw, convb, wgate, bgx, bga, lam, wg2, bg2)

    yp, hp, cp, sp, o, ss = _prompt(x_prompt, ada_p, gnorm, w_in_t, w_out2d, wglr, convw,
                                    convb, wgate, bgx, bga, lam, wg2, bg2, ggla, gfin,
                                    qs, ks, eg, vs, state_gla.reshape(ns, HEADS, DK, DV))

    ys = _sample_out(xs, ada_s[:, 2 * D:], o.reshape(ns, D), zg, ylru, ggla, w_out2d, gfin)
    cs = jnp.stack([conv0[:, 1], conv0[:, 2], xa], axis=1)

    return (yp, ys.reshape(ns, 1, D), hp.reshape(1, bp, D), cp[None], sp[None],
            hs[None], cs[None], ss[None])
```

```python
import functools

import jax
import jax.numpy as jnp
from jax import lax
from jax.experimental import pallas as pl
from jax.experimental.pallas import tpu as pltpu

EPS = 1e-6
D = 1024
LRU_BLOCKS = 8
LRU_BW = D // LRU_BLOCKS
LRU_C = 8.0
CONV_W = 4
HEADS = 4
DV = D // HEADS
DK = DV // 2
RANK = 16
RANK_PAD = 128
TAU = 16.0
CHUNK = 64
D_QK = HEADS * DK
D_MIX = 2 * D
O_GLR = 4 * D
N_GROUPS = 5

SUBLANES = 8
TM = 256
STAGE_ROWS = 256
VMEM_LIMIT = 56 * 1024 * 1024
NEG_LOG2E = -1.4426950408889634

_BF = jnp.bfloat16
_F32 = jnp.float32

_NT = (((1,), (1,)), ((), ()))


def _sigmoid(x):
    return 1.0 / (1.0 + jnp.exp2(x * NEG_LOG2E))


def _silu(x):
    return x * _sigmoid(x)


def _log_sigmoid(x):
    return jnp.minimum(x, 0.0) - jnp.log(1.0 + jnp.exp2(jnp.abs(x) * NEG_LOG2E))


def _rms(x):
    return x * lax.rsqrt(jnp.mean(x * x, axis=-1, keepdims=True) + EPS)


def _dot(a, b):
    return jnp.dot(a, b, preferred_element_type=_F32)


def _group_scan(a3, u3):
    row = lax.broadcasted_iota(jnp.int32, a3.shape, 1)
    for s in (1, 2, 4):
        a_sh = pltpu.roll(a3, s, axis=1)
        u_sh = pltpu.roll(u3, s, axis=1)
        m = row >= s
        u3 = jnp.where(m, a3 * u_sh + u3, u3)
        a3 = jnp.where(m, a3 * a_sh, a3)
    return a3, u3


def _group_cumsum(x3):
    row = lax.broadcasted_iota(jnp.int32, x3.shape, 1)
    for s in (1, 2, 4):
        x3 = jnp.where(row >= s, x3 + pltpu.roll(x3, s, axis=1), x3)
    return x3


def _linear_scan(a, u, h0):
    r, l = a.shape
    g = r // SUBLANES
    a3, u3 = _group_scan(a.reshape(g, SUBLANES, l), u.reshape(g, SUBLANES, l))
    carry = h0
    out = []
    for i in range(g):
        hg = a3[i] * carry + u3[i]
        out.append(hg)
        carry = hg[SUBLANES - 1:SUBLANES]
    return jnp.concatenate(out, axis=0)


def _chunk_cumsum(x):
    r, l = x.shape
    g = r // SUBLANES
    per_chunk = CHUNK // SUBLANES
    x3 = _group_cumsum(x.reshape(g, SUBLANES, l))
    out = []
    carry = None
    for i in range(g):
        xg = x3[i] if i % per_chunk == 0 else x3[i] + carry
        out.append(xg)
        carry = xg[SUBLANES - 1:SUBLANES]
    return jnp.concatenate(out, axis=0)


def _lru_block(xc_n, pre, bgx, bga, c_logsig):
    gx = _sigmoid(pre[:, :LRU_BW] + bgx)
    ga = _sigmoid(pre[:, LRU_BW:] + bga)
    log_a = ga * c_logsig
    a = jnp.exp(log_a)
    om = 1.0 - a * a
    mult = jnp.where(om > 0.0, om * lax.rsqrt(om), 0.0)
    u = mult * (gx * xc_n)
    return a, u


def _gla_token_step(qs, k, eg, v, s0):
    eye = (lax.broadcasted_iota(jnp.int32, (DK, DK), 0)
           == lax.broadcasted_iota(jnp.int32, (DK, DK), 1))

    def to_col(row):
        return jnp.sum(jnp.where(eye, jnp.broadcast_to(row, (DK, DK)), 0.0),
                       axis=-1, keepdims=True)

    qk = jnp.sum(qs * k, axis=-1, keepdims=True)
    qg = jnp.broadcast_to(qs * eg, (SUBLANES, DK)).astype(_BF)
    o = _dot(qg, s0.astype(_BF))[0:1] + qk * v
    return o, to_col(eg) * s0 + to_col(k) * v


def _stage_blocks(n, src_block, store_block, stage, sem):
    def copy(i, slot):
        return pltpu.make_async_copy(src_block(i), stage.at[slot], sem.at[slot])

    copy(0, 0).start()

    def body(i, carry):
        slot = lax.rem(i, 2)

        @pl.when(i + 1 < n)
        def _():
            copy(i + 1, 1 - slot).start()

        copy(i, slot).wait()
        store_block(i, stage[slot].astype(_BF))
        return carry

    lax.fori_loop(0, n, body, 0)


def _stage_in_proj(wt_hbm, wmain_s, stage, sem):
    per_group = D // STAGE_ROWS

    def block(g, c):
        row0 = g * D + (g // (N_GROUPS - 1)) * RANK + c * STAGE_ROWS
        return wt_hbm.at[pl.ds(pl.multiple_of(row0, SUBLANES), STAGE_ROWS), :]

    def copy(g, c):
        return pltpu.make_async_copy(block(g, c), stage.at[c % 2], sem.at[c % 2])

    copy(0, 0).start()

    def body(g, carry):
        for c in range(per_group):
            if c + 1 < per_group:
                copy(g, c + 1).start()
            else:
                @pl.when(g + 1 < N_GROUPS)
                def _():
                    copy(g + 1, 0).start()
            copy(g, c).wait()
            wmain_s[g, :, c * STAGE_ROWS:(c + 1) * STAGE_ROWS] = stage[c % 2].T.astype(_BF)
        return carry

    lax.fori_loop(0, N_GROUPS, body, 0)


def _prompt_kernel(xn_ref, xp_ref, adan_ref, adap_ref, gnorm_ref, wt_hbm, wout_hbm,
                   wglr_ref, convw_ref, convb_ref, wgate_ref, bgx_ref, bga_ref, lam_ref, wg2_ref,
                   bg2_ref, ggla_ref, gfin_ref, sqs_ref, sk_ref, seg_ref, sv_ref, ss0_ref,
                   y_ref, hlast_ref, convout_ref, sout_ref, so_ref, ssnew_ref,
                   wmain_s, wout_s, wglr_s, wg2_s, wgate_s, stage, sem,
                   hn_s, xa_s, za_s, xc_s, pre_s, q_s, k_s, v_s, zg_s, b_s, mix_s, yacc_s, ctail_s,
                   h_s, st_s, *, tm, nt, n_tiles, s_per_step, s_blocks):
    i = pl.program_id(0)
    t = lax.rem(jnp.minimum(i, n_tiles - 1), nt)
    t_next = lax.rem(jnp.minimum(i + 1, n_tiles - 1), nt)
    slot = lax.rem(i, 2)
    half = D // 2
    quarter = D // 4

    def front_norm(x_ref, ada_ref):
        x = x_ref[0]
        gain = gnorm_ref[...] * (1.0 + ada_ref[0, 1:2, :])
        hn_s[...] = (_rms(x) * gain + ada_ref[0, 0:1, :]).astype(_BF)

    def front_xa():
        xa_s[SUBLANES:SUBLANES + tm, :] = _dot(hn_s[...], wmain_s[0])

    def front_za():
        za_s[...] = _dot(hn_s[...], wmain_s[1])

    def front_conv():
        xe = xa_s[...]
        xc = convb_ref[...]
        for j in range(CONV_W):
            lag = CONV_W - 1 - j
            xj = xe if lag == 0 else pltpu.roll(xe, lag, axis=0)
            xc = xc + xj[SUBLANES:, :] * convw_ref[j:j + 1, :]
        xc_s[...] = xc

    def front_gates():
        for n in range(LRU_BLOCKS):
            pre_s[:, 2 * n * LRU_BW:2 * (n + 1) * LRU_BW] = _dot(
                xc_s[:, n * LRU_BW:(n + 1) * LRU_BW].astype(_BF), wgate_s[n])

    @pl.when(i == 0)
    def _():
        def out_rows(j):
            return pl.ds(pl.multiple_of(j * STAGE_ROWS, STAGE_ROWS), STAGE_ROWS)

        def store_out(j, val):
            wout_s[out_rows(j), :] = val

        _stage_in_proj(wt_hbm, wmain_s, stage, sem)
        _stage_blocks(D_MIX // STAGE_ROWS, lambda j: wout_hbm.at[out_rows(j), :], store_out,
                      stage, sem)
        wglr_s[...] = wglr_ref[...].astype(_BF)
        wg2_s[...] = wg2_ref[...].astype(_BF)
        wgate_s[...] = wgate_ref[...].astype(_BF)
        mix_s[...] = jnp.zeros_like(mix_s)
        xa_s[0:SUBLANES, :] = jnp.zeros((SUBLANES, D), _F32)
        front_norm(xp_ref, adap_ref)
        front_xa()
        front_za()
        front_conv()
        front_gates()

    @pl.when(t == 0)
    def _():
        h_s[...] = jnp.zeros_like(h_s)
        st_s[...] = jnp.zeros_like(st_s)

    def out_slab(j):
        def run():
            cols = slice(j * quarter, (j + 1) * quarter)
            o = _dot(mix_s[1 - slot], wout_s[:, cols])
            yacc_s[:, cols] = xp_ref[0, :, cols] + adap_ref[0, 2:3, cols] * o
        return run

    def out_norm():
        y_ref[0] = _rms(yacc_s[...]) * gfin_ref[...]

    @pl.when(i < n_tiles)
    def _():
        for j in range(s_per_step):
            srow = slice(j, j + 1)
            for h in range(HEADS):
                lk = slice(h * DK, (h + 1) * DK)
                lv = slice(h * DV, (h + 1) * DV)
                o, snew = _gla_token_step(sqs_ref[0, srow, lk], sk_ref[0, srow, lk],
                                          seg_ref[0, srow, lk], sv_ref[0, srow, lv], ss0_ref[j, h])
                so_ref[0, srow, lv] = o
                ssnew_ref[j, h] = snew

        def proj(dst, dst_cols, group, w_cols):
            def run():
                dst[:, dst_cols] = _dot(hn_s[...], wmain_s[group, :, w_cols])
            return run

        def decay():
            glr = lax.dot_general(hn_s[...], wglr_s[...], _NT,
                                  preferred_element_type=_F32)
            la = _log_sigmoid(_dot(glr.astype(_BF), wg2_s[...]) + bg2_ref[...]) / TAU
            b_s[...] = _chunk_cumsum(la)

        lo, hi, full = slice(0, half), slice(half, D), slice(0, D_QK)
        tasks = [
            (out_slab(0), proj(q_s, full, 2, lo)),
            (out_slab(1), proj(k_s, full, 2, hi)),
            (out_slab(2), decay),
            (out_slab(3), out_norm),
            (proj(v_s, lo, 3, lo),),
            (proj(v_s, hi, 3, hi),),
            (proj(zg_s, lo, 4, lo),),
            (proj(zg_s, hi, 4, hi),),
        ]

        logsig = LRU_C * _log_sigmoid(lam_ref[...])
        for n in range(LRU_BLOCKS):
            ln = slice(n * LRU_BW, (n + 1) * LRU_BW)
            a, u = _lru_block(xc_s[:, ln], pre_s[:, 2 * n * LRU_BW:2 * (n + 1) * LRU_BW],
                              bgx_ref[:, ln], bga_ref[:, ln], logsig[:, ln])
            h = _linear_scan(a, u, h_s[:, ln])
            h_s[:, ln] = h[tm - 1:tm]
            mix_s[slot, :, ln] = (h * _silu(za_s[:, ln])).astype(_BF)
            for task in tasks[n]:
                task()

        tril = (lax.broadcasted_iota(jnp.int32, (CHUNK, CHUNK), 0)
                >= lax.broadcasted_iota(jnp.int32, (CHUNK, CHUNK), 1))

        def gla_chunk(c):
            rows = slice(c * CHUNK, (c + 1) * CHUNK)
            for h in range(HEADS):
                lk = slice(h * DK, (h + 1) * DK)
                lv = slice(h * DV, (h + 1) * DV)
                b = b_s[rows, lk]
                bl = b[CHUNK - 1:CHUNK]
                q = q_s[rows, lk] * (DK ** -0.5)
                k = k_s[rows, lk]
                v = v_s[rows, lv]
                qi = (q * jnp.exp(b)).astype(_BF)
                ki = (k * jnp.exp(-b)).astype(_BF)
                kd = (k * jnp.exp(bl - b)).astype(_BF)
                vb = v.astype(_BF)
                att = lax.dot_general(qi, ki, _NT, preferred_element_type=_F32)
                att = jnp.where(tril, att, 0.0).astype(_BF)
                st = st_s[h]
                o = _dot(att, vb) + lax.dot_general(qi, st.astype(_BF), _NT,
                                                    preferred_element_type=_F32)
                st_s[h] = st * jnp.exp(bl) + _dot(v.T.astype(_BF), kd)
                o = _rms(o) * ggla_ref[:, lv]
                mix_s[slot, rows, D + h * DV:D + (h + 1) * DV] = (
                    o * _silu(zg_s[rows, lv])).astype(_BF)

        def next_tail():
            tail = xa_s[tm:tm + SUBLANES, :]
            ctail_s[...] = tail
            xa_s[0:SUBLANES, :] = jnp.where(t_next != 0, tail, 0.0)

        next_tail()
        front_norm(xn_ref, adan_ref)
        gla_chunk(0)
        front_xa()
        gla_chunk(1)
        front_za()
        front_conv()
        gla_chunk(2)
        front_gates()
        gla_chunk(3)

    @pl.when(i == n_tiles)
    def _():
        for j in range(D // quarter):
            out_slab(j)()
        out_norm()

    @pl.when((t == nt - 1) & (i < n_tiles))
    def _():
        hlast_ref[0] = h_s[...]
        convout_ref[0] = ctail_s[SUBLANES - (CONV_W - 1):SUBLANES, :]
        for h in range(HEADS):
            sout_ref[0, h] = st_s[h].T


def _const_spec(shape):
    nd = len(shape)
    return pl.BlockSpec(shape, lambda i: (0,) * nd, pipeline_mode=pl.Buffered(1))


def _prompt(x, ada3, gnorm, w_in_t, w_out2d, wglr, convw, convb, wgate, bgx, bga, lam,
            wg2, bg2, ggla, gfin, sqs, sk, seg, sv, ss0, tm=TM):
    bsz, seq, _ = x.shape
    nt = seq // tm
    n_tiles = bsz * nt
    ns = ss0.shape[0]
    s_per_step = -(-ns // n_tiles)
    s_blocks = ns // s_per_step
    assert tm // CHUNK == 4, "the step body spreads the next tile's input side over 4 GLA chunks"
    assert s_blocks * s_per_step == ns and s_blocks <= n_tiles
    kern = functools.partial(_prompt_kernel, tm=tm, nt=nt, n_tiles=n_tiles,
                             s_per_step=s_per_step, s_blocks=s_blocks)
    out_shape = (
        jax.ShapeDtypeStruct((bsz, seq, D), _F32),
        jax.ShapeDtypeStruct((bsz, 1, D), _F32),
        jax.ShapeDtypeStruct((bsz, CONV_W - 1, D), _F32),
        jax.ShapeDtypeStruct((bsz, HEADS, DK, DV), _F32),
        jax.ShapeDtypeStruct((s_blocks, s_per_step, D), _F32),
        jax.ShapeDtypeStruct(ss0.shape, _F32),
    )
    by_step = lambda a: a.reshape(s_blocks, s_per_step, a.shape[-1])
    sqs, sk, seg, sv = by_step(sqs), by_step(sk), by_step(seg), by_step(sv)
    s_rows = lambda i: (jnp.minimum(i, s_blocks - 1), 0, 0)
    s_block = lambda i: (jnp.minimum(i, s_blocks - 1), 0, 0, 0)
    cur = lambda i: jnp.minimum(i, n_tiles - 1)
    prev = lambda i: jnp.maximum(i - 1, 0)
    nxt = lambda i: jnp.minimum(i + 1, n_tiles - 1)
    tile = lambda f: (lambda i: (f(i) // nt, lax.rem(f(i), nt), 0))
    seq_of = lambda f: (lambda i: (f(i) // nt, 0, 0))
    hbm = pl.BlockSpec(memory_space=pl.ANY)
    in_specs = [
        pl.BlockSpec((1, tm, D), tile(nxt)),
        pl.BlockSpec((1, tm, D), tile(prev)),
        pl.BlockSpec((1, 3, D), seq_of(nxt)),
        pl.BlockSpec((1, 3, D), seq_of(prev)),
        _const_spec((1, D)),
        hbm, hbm,
        _const_spec((RANK_PAD, D)),
        _const_spec((CONV_W, D)),
        _const_spec((1, D)),
        _const_spec((LRU_BLOCKS, LRU_BW, 2 * LRU_BW)),
        _const_spec((1, D)),
        _const_spec((1, D)),
        _const_spec((1, D)),
        _const_spec((RANK_PAD, D_QK)),
        _const_spec((1, D_QK)),
        _const_spec((1, D)),
        _const_spec((1, D)),
        pl.BlockSpec((1, s_per_step, D_QK), s_rows),
        pl.BlockSpec((1, s_per_step, D_QK), s_rows),
        pl.BlockSpec((1, s_per_step, D_QK), s_rows),
        pl.BlockSpec((1, s_per_step, D), s_rows),
        pl.BlockSpec((s_per_step, HEADS, DK, DV), s_block),
    ]
    out_specs = (
        pl.BlockSpec((1, tm, D), tile(prev)),
        pl.BlockSpec((1, 1, D), seq_of(cur)),
        pl.BlockSpec((1, CONV_W - 1, D), seq_of(cur)),
        pl.BlockSpec((1, HEADS, DK, DV), lambda i: (cur(i) // nt, 0, 0, 0)),
        pl.BlockSpec((1, s_per_step, D), s_rows),
        pl.BlockSpec((s_per_step, HEADS, DK, DV), s_block),
    )
    scratch = [
        pltpu.VMEM((N_GROUPS, D, D), _BF),
        pltpu.VMEM((D_MIX, D), _BF),
        pltpu.VMEM((RANK_PAD, D), _BF),
        pltpu.VMEM((RANK_PAD, D_QK), _BF),
        pltpu.VMEM((LRU_BLOCKS, LRU_BW, 2 * LRU_BW), _BF),
        pltpu.VMEM((2, STAGE_ROWS, D), _F32),
        pltpu.SemaphoreType.DMA((2,)),
        pltpu.VMEM((tm, D), _BF),
        pltpu.VMEM((tm + SUBLANES, D), _F32),
        pltpu.VMEM((tm, D), _F32),
        pltpu.VMEM((tm, D), _F32),
        pltpu.VMEM((tm, 2 * D), _F32),
        pltpu.VMEM((tm, D_QK), _F32),
        pltpu.VMEM((tm, D_QK), _F32),
        pltpu.VMEM((tm, D), _F32),
        pltpu.VMEM((tm, D), _F32),
        pltpu.VMEM((tm, D_QK), _F32),
        pltpu.VMEM((2, tm, D_MIX), _BF),
        pltpu.VMEM((tm, D), _F32),
        pltpu.VMEM((SUBLANES, D), _F32),
        pltpu.VMEM((1, D), _F32),
        pltpu.VMEM((HEADS, DV, DK), _F32),
    ]
    return pl.pallas_call(
        kern,
        out_shape=out_shape,
        grid=(n_tiles + 1,),
        in_specs=in_specs,
        out_specs=out_specs,
        scratch_shapes=scratch,
        compiler_params=pltpu.CompilerParams(
            dimension_semantics=("arbitrary",),
            vmem_limit_bytes=VMEM_LIMIT),
        name="prompt_layer",
    )(x, x, ada3, ada3, gnorm, w_in_t, w_out2d, wglr, convw, convb, wgate, bgx, bga, lam,
      wg2, bg2, ggla, gfin, sqs, sk, seg, sv, ss0)


def _decode_in_kernel(c_ref, wada_ref, bada_ref, x_ref, h0_ref, c0_ref, c1_ref, c2_ref,
                      gnorm_ref, wt_ref, wglr_ref, convw_ref, convb_ref, wgate_ref,
                      bgx_ref, bga_ref, lam_ref, wg2_ref, bg2_ref,
                      adap_ref, gate_ref, hnew_ref, xa_ref, ylru_ref, qs_ref, k_ref, eg_ref,
                      v_ref, zg_ref, *, bp):
    sc = _silu(c_ref[...]).astype(_BF)
    ada = [_dot(sc, wada_ref[:, j * D:(j + 1) * D].astype(_BF)) + bada_ref[:, j * D:(j + 1) * D]
           for j in range(3)]
    for j in range(3):
        for b in range(bp):
            adap_ref[b, j:j + 1, :] = ada[j][b:b + 1, :]
    shift, scale = ada[0][bp:, :], ada[1][bp:, :]
    gate_ref[...] = ada[2][bp:, :]

    x = x_ref[...]
    gain = gnorm_ref[...] * (1.0 + scale)
    hnb = (_rms(x) * gain + shift).astype(_BF)

    def proj(c0, width):
        return lax.dot_general(hnb, wt_ref[c0:c0 + width, :].astype(_BF), _NT,
                               preferred_element_type=_F32)

    xa = proj(0, D)
    za = proj(D, D)
    qs_ref[...] = proj(2 * D, D_QK) * (DK ** -0.5)
    k_ref[...] = proj(2 * D + D_QK, D_QK)
    v_ref[...] = proj(3 * D, D)
    zg_ref[...] = proj(O_GLR + RANK, D)
    glr = lax.dot_general(hnb, wglr_ref[...].astype(_BF), _NT, preferred_element_type=_F32)
    la = _log_sigmoid(_dot(glr.astype(_BF), wg2_ref[...].astype(_BF)) + bg2_ref[...]) / TAU
    eg_ref[...] = jnp.exp(la)
    xa_ref[...] = xa

    xc = convb_ref[...]
    for j, cj in enumerate((c0_ref[...], c1_ref[...], c2_ref[...], xa)):
        xc = xc + cj * convw_ref[j:j + 1, :]
    xcb = xc.astype(_BF)
    logsig = LRU_C * _log_sigmoid(lam_ref[...])
    for n in range(LRU_BLOCKS):
        ln = slice(n * LRU_BW, (n + 1) * LRU_BW)
        pre = _dot(xcb[:, ln], wgate_ref[n].astype(_BF))
        a, u = _lru_block(xc[:, ln], pre, bgx_ref[:, ln], bga_ref[:, ln], logsig[:, ln])
        h = a * h0_ref[:, ln] + u
        hnew_ref[:, ln] = h
        ylru_ref[:, ln] = h * _silu(za[:, ln])


def _decode_in(c_all, w_ada, b_ada, x, h0, c0, c1, c2, gnorm, w_in_t, wglr, convw, convb, wgate,
               bgx, bga, lam, wg2, bg2, bp):
    n = x.shape[0]
    assert bp % SUBLANES == 0, "decode rows must start on a sublane-tile boundary of the adaLN rows"
    f = lambda w: jax.ShapeDtypeStruct((n, w), _F32)
    return pl.pallas_call(
        functools.partial(_decode_in_kernel, bp=bp),
        out_shape=(jax.ShapeDtypeStruct((bp, 3, D), _F32), f(D),
                   f(D), f(D), f(D), f(D_QK), f(D_QK), f(D_QK), f(D), f(D)),
        compiler_params=pltpu.CompilerParams(vmem_limit_bytes=VMEM_LIMIT),
        name="decode_in",
    )(c_all, w_ada, b_ada, x, h0, c0, c1, c2, gnorm, w_in_t, wglr, convw, convb, wgate, bgx, bga,
      lam, wg2, bg2)


def _decode_out_kernel(x_ref, gate_ref, o_ref, zg_ref, ylru_ref, ggla_ref, wout_ref, gfin_ref,
                       y_ref):
    parts = [ylru_ref[...].astype(_BF)]
    for h in range(HEADS):
        lv = slice(h * DV, (h + 1) * DV)
        o = _rms(o_ref[:, lv]) * ggla_ref[:, lv]
        parts.append((o * _silu(zg_ref[:, lv])).astype(_BF))
    mix = jnp.concatenate(parts, axis=-1)
    y = x_ref[...] + gate_ref[...] * _dot(mix, wout_ref[...].astype(_BF))
    y_ref[...] = _rms(y) * gfin_ref[...]


def _decode_out(x, gate, o, zg, ylru, ggla, w_out2d, gfin):
    return pl.pallas_call(
        _decode_out_kernel,
        out_shape=jax.ShapeDtypeStruct(x.shape, _F32),
        compiler_params=pltpu.CompilerParams(vmem_limit_bytes=VMEM_LIMIT),
        name="decode_out",
    )(x, gate, o, zg, ylru, ggla, w_out2d, gfin)


def kernel(x_prompt, x_sample, state_lru_h, state_lru_conv, state_gla, c_prompt, c_sample,
           g_norm, w_ada, b_ada, w_in, conv_w, conv_b, w_gate_x, b_gate_x, w_gate_a, b_gate_a,
           lru_lambda, w_gla_g2, b_gla_g2, g_gla_norm, w_out, g_final):
    depth = g_norm.shape[0]
    assert depth == 1, "single-layer trunk"
    bp = x_prompt.shape[0]
    ns = x_sample.shape[0]

    w_in_t = jnp.swapaxes(w_in, 1, 2).reshape(w_in.shape[-1], D)
    w_out2d = w_out.reshape(D_MIX, D)
    wglr = jnp.pad(w_in_t[O_GLR:O_GLR + RANK, :], ((0, RANK_PAD - RANK), (0, 0)))
    wg2 = jnp.pad(w_gla_g2.reshape(RANK, D_QK), ((0, RANK_PAD - RANK), (0, 0)))
    wgate = jnp.concatenate([w_gate_x.reshape(LRU_BLOCKS, LRU_BW, LRU_BW),
                             w_gate_a.reshape(LRU_BLOCKS, LRU_BW, LRU_BW)], axis=-1)
    row = lambda a: a.reshape(1, -1)
    gnorm, convb = row(g_norm), row(conv_b)
    bgx, bga, lam = row(b_gate_x), row(b_gate_a), row(lru_lambda)
    bg2, ggla, gfin = row(b_gla_g2), row(g_gla_norm), row(g_final)
    convw = conv_w.reshape(CONV_W, D)

    xs = x_sample.reshape(ns, D)
    conv0 = state_lru_conv.reshape(ns, CONV_W - 1, D)
    ada_p, gate_s, hs, xa, ylru, qs, ks, eg, vs, zg = _decode_in(
        jnp.concatenate([c_prompt, c_sample], axis=0), w_ada.reshape(D, 3 * D), row(b_ada),
        xs, state_lru_h.reshape(ns, D), conv0[:, 0], conv0[:, 1], conv0[:, 2],
        gnorm, w_in_t, wglr, convw, convb, wgate, bgx, bga, lam, wg2, bg2, bp)

    yp, hp, cp, sp, o, ss = _prompt(x_prompt, ada_p, gnorm, w_in_t, w_out2d, wglr, convw,
                                    convb, wgate, bgx, bga, lam, wg2, bg2, ggla, gfin,
                                    qs, ks, eg, vs, state_gla.reshape(ns, HEADS, DK, DV))

    ys = _decode_out(xs, gate_s, o.reshape(ns, D), zg, ylru, ggla, w_out2d, gfin)
    cs = jnp.stack([conv0[:, 1], conv0[:, 2], xa], axis=1)

    return (yp, ys.reshape(ns, 1, D), hp.reshape(1, bp, D), cp[None], sp[None],
            hs[None], cs[None], ss[None])
```

```python
import functools

import jax
import jax.numpy as jnp
from jax import lax
from jax.experimental import pallas as pl
from jax.experimental.pallas import tpu as pltpu

EPS = 1e-6
D = 1024
LRU_BLOCKS = 8
LRU_BW = D // LRU_BLOCKS
LRU_C = 8.0
CONV_W = 4
HEADS = 4
DV = D // HEADS
DK = DV // 2
RANK = 16
RANK_PAD = 128
TAU = 16.0
CHUNK = 64
D_QK = HEADS * DK
D_MIX = 2 * D
O_GLR = 4 * D
N_GROUPS = 5

SUBLANES = 8
TM = 256
STAGE_ROWS = 256
VMEM_LIMIT = 56 * 1024 * 1024
NEG_LOG2E = -1.4426950408889634

_BF = jnp.bfloat16
_F32 = jnp.float32

_NT = (((1,), (1,)), ((), ()))


def _sigmoid(x):
    return 1.0 / (1.0 + jnp.exp2(x * NEG_LOG2E))


def _silu(x):
    return x * _sigmoid(x)


def _log_sigmoid(x):
    return jnp.minimum(x, 0.0) - jnp.log(1.0 + jnp.exp2(jnp.abs(x) * NEG_LOG2E))


def _rms(x):
    return x * lax.rsqrt(jnp.mean(x * x, axis=-1, keepdims=True) + EPS)


def _dot(a, b):
    return jnp.dot(a, b, preferred_element_type=_F32)


def _group_scan(a3, u3):
    row = lax.broadcasted_iota(jnp.int32, a3.shape, 1)
    for s in (1, 2, 4):
        a_sh = pltpu.roll(a3, s, axis=1)
        u_sh = pltpu.roll(u3, s, axis=1)
        m = row >= s
        u3 = jnp.where(m, a3 * u_sh + u3, u3)
        a3 = jnp.where(m, a3 * a_sh, a3)
    return a3, u3


def _group_cumsum(x3):
    row = lax.broadcasted_iota(jnp.int32, x3.shape, 1)
    for s in (1, 2, 4):
        x3 = jnp.where(row >= s, x3 + pltpu.roll(x3, s, axis=1), x3)
    return x3


def _linear_scan(a, u, h0):
    r, l = a.shape
    g = r // SUBLANES
    a3, u3 = _group_scan(a.reshape(g, SUBLANES, l), u.reshape(g, SUBLANES, l))
    carry = h0
    out = []
    for i in range(g):
        hg = a3[i] * carry + u3[i]
        out.append(hg)
        carry = hg[SUBLANES - 1:SUBLANES]
    return jnp.concatenate(out, axis=0)


def _chunk_cumsum(x):
    r, l = x.shape
    g = r // SUBLANES
    per_chunk = CHUNK // SUBLANES
    x3 = _group_cumsum(x.reshape(g, SUBLANES, l))
    out = []
    carry = None
    for i in range(g):
        xg = x3[i] if i % per_chunk == 0 else x3[i] + carry
        out.append(xg)
        carry = xg[SUBLANES - 1:SUBLANES]
    return jnp.concatenate(out, axis=0)


def _lru_block(xc_n, pre, bgx, bga, c_logsig):
    gx = _sigmoid(pre[:, :LRU_BW] + bgx)
    ga = _sigmoid(pre[:, LRU_BW:] + bga)
    log_a = ga * c_logsig
    a = jnp.exp(log_a)
    om = 1.0 - a * a
    mult = jnp.where(om > 0.0, om * lax.rsqrt(om), 0.0)
    u = mult * (gx * xc_n)
    return a, u


def _gla_token_step(qs, k, eg, v, s0):
    eye = (lax.broadcasted_iota(jnp.int32, (DK, DK), 0)
           == lax.broadcasted_iota(jnp.int32, (DK, DK), 1))

    def to_col(row):
        return jnp.sum(jnp.where(eye, jnp.broadcast_to(row, (DK, DK)), 0.0),
                       axis=-1, keepdims=True)

    qk = jnp.sum(qs * k, axis=-1, keepdims=True)
    qg = jnp.broadcast_to(qs * eg, (SUBLANES, DK)).astype(_BF)
    o = _dot(qg, s0.astype(_BF))[0:1] + qk * v
    return o, to_col(eg) * s0 + to_col(k) * v


def _stage_blocks(n, src_block, store_block, stage, sem):
    def copy(i, slot):
        return pltpu.make_async_copy(src_block(i), stage.at[slot], sem.at[slot])

    copy(0, 0).start()

    def body(i, carry):
        slot = lax.rem(i, 2)

        @pl.when(i + 1 < n)
        def _():
            copy(i + 1, 1 - slot).start()

        copy(i, slot).wait()
        store_block(i, stage[slot].astype(_BF))
        return carry

    lax.fori_loop(0, n, body, 0)


def _stage_in_proj(wt_hbm, wmain_s, stage, sem):
    per_group = D // STAGE_ROWS

    def block(g, c):
        row0 = g * D + (g // (N_GROUPS - 1)) * RANK + c * STAGE_ROWS
        return wt_hbm.at[pl.ds(pl.multiple_of(row0, SUBLANES), STAGE_ROWS), :]

    def copy(g, c):
        return pltpu.make_async_copy(block(g, c), stage.at[c % 2], sem.at[c % 2])

    copy(0, 0).start()

    def body(g, carry):
        for c in range(per_group):
            if c + 1 < per_group:
                copy(g, c + 1).start()
            else:
                @pl.when(g + 1 < N_GROUPS)
                def _():
                    copy(g + 1, 0).start()
            copy(g, c).wait()
            wmain_s[g, :, c * STAGE_ROWS:(c + 1) * STAGE_ROWS] = stage[c % 2].T.astype(_BF)
        return carry

    lax.fori_loop(0, N_GROUPS, body, 0)


def _prompt_kernel(xn_ref, xp_ref, adan_ref, adap_ref, gnorm_ref, wt_hbm, wout_hbm,
                   convw_ref, convb_ref, wgx_ref, wga_ref, bgx_ref, bga_ref, lam_ref, wg2_ref,
                   bg2_ref, ggla_ref, gfin_ref, sqs_ref, sk_ref, seg_ref, sv_ref, ss0_ref,
                   y_ref, hlast_ref, convout_ref, sout_ref, so_ref, ssnew_ref,
                   wmain_s, wout_s, wglr_s, wg2_s, wgate_s, stage, sem,
                   hn_s, xa_s, za_s, xc_s, pre_s, q_s, k_s, v_s, zg_s, b_s, mix_s, yacc_s, ctail_s,
                   h_s, st_s, *, tm, nt, n_tiles, s_per_step, s_blocks):
    i = pl.program_id(0)
    t = lax.rem(jnp.minimum(i, n_tiles - 1), nt)
    t_next = lax.rem(jnp.minimum(i + 1, n_tiles - 1), nt)
    slot = lax.rem(i, 2)
    half = D // 2
    quarter = D // 4

    def front_norm(x_ref, ada_ref):
        x = x_ref[0]
        gain = gnorm_ref[...] * (1.0 + ada_ref[0, 1:2, :])
        hn_s[...] = (_rms(x) * gain + ada_ref[0, 0:1, :]).astype(_BF)

    def front_xa():
        xa_s[SUBLANES:SUBLANES + tm, :] = _dot(hn_s[...], wmain_s[0])

    def front_za():
        za_s[...] = _dot(hn_s[...], wmain_s[1])

    def front_conv():
        xe = xa_s[...]
        xc = convb_ref[...]
        for j in range(CONV_W):
            lag = CONV_W - 1 - j
            xj = xe if lag == 0 else pltpu.roll(xe, lag, axis=0)
            xc = xc + xj[SUBLANES:, :] * convw_ref[j:j + 1, :]
        xc_s[...] = xc

    def front_gates():
        for n in range(LRU_BLOCKS):
            pre_s[:, 2 * n * LRU_BW:2 * (n + 1) * LRU_BW] = _dot(
                xc_s[:, n * LRU_BW:(n + 1) * LRU_BW].astype(_BF), wgate_s[n])

    @pl.when(i == 0)
    def _():
        def out_rows(j):
            return pl.ds(pl.multiple_of(j * STAGE_ROWS, STAGE_ROWS), STAGE_ROWS)

        def store_out(j, val):
            wout_s[out_rows(j), :] = val

        _stage_in_proj(wt_hbm, wmain_s, stage, sem)
        _stage_blocks(D_MIX // STAGE_ROWS, lambda j: wout_hbm.at[out_rows(j), :], store_out,
                      stage, sem)
        glr_rows = pltpu.make_async_copy(wt_hbm.at[pl.ds(O_GLR, RANK), :],
                                         stage.at[0, pl.ds(0, RANK), :], sem.at[0])
        glr_rows.start()
        wglr_s[...] = jnp.zeros_like(wglr_s)
        wg2_s[...] = jnp.zeros_like(wg2_s)
        wg2_s[0:RANK, :] = wg2_ref[...].astype(_BF)
        wgate_s[:, :, 0:LRU_BW] = wgx_ref[...].astype(_BF)
        wgate_s[:, :, LRU_BW:2 * LRU_BW] = wga_ref[...].astype(_BF)
        glr_rows.wait()
        wglr_s[0:RANK, :] = stage[0, 0:RANK, :].astype(_BF)
        so_ref[...] = jnp.zeros_like(so_ref)
        mix_s[...] = jnp.zeros_like(mix_s)
        xa_s[0:SUBLANES, :] = jnp.zeros((SUBLANES, D), _F32)
        front_norm(xp_ref, adap_ref)
        front_xa()
        front_za()
        front_conv()
        front_gates()

    @pl.when(t == 0)
    def _():
        h_s[...] = jnp.zeros_like(h_s)
        st_s[...] = jnp.zeros_like(st_s)

    def out_slab(j):
        def run():
            cols = slice(j * quarter, (j + 1) * quarter)
            o = _dot(mix_s[1 - slot], wout_s[:, cols])
            yacc_s[:, cols] = xp_ref[0, :, cols] + adap_ref[0, 2:3, cols] * o
        return run

    def out_norm():
        y_ref[0] = _rms(yacc_s[...]) * gfin_ref[...]

    @pl.when(i < n_tiles)
    def _():
        first = i * s_per_step
        grp = pl.ds(pl.multiple_of((first // SUBLANES) * SUBLANES, SUBLANES), SUBLANES)
        r0 = lax.rem(first, SUBLANES)
        sub = lax.broadcasted_iota(jnp.int32, (SUBLANES, 1), 0)
        g_qs, g_k, g_eg, g_v = sqs_ref[grp, :], sk_ref[grp, :], seg_ref[grp, :], sv_ref[grp, :]
        g_o = so_ref[grp, :]
        for j in range(s_per_step):
            mine = sub == r0 + j
            pick = lambda a: jnp.sum(jnp.where(mine, a, 0.0), axis=0, keepdims=True)
            qs_j, k_j, eg_j, v_j = pick(g_qs), pick(g_k), pick(g_eg), pick(g_v)
            o_heads = []
            for h in range(HEADS):
                lk = slice(h * DK, (h + 1) * DK)
                lv = slice(h * DV, (h + 1) * DV)
                o, snew = _gla_token_step(qs_j[:, lk], k_j[:, lk], eg_j[:, lk], v_j[:, lv],
                                          ss0_ref[j, h])
                o_heads.append(o)
                ssnew_ref[j, h] = snew
            g_o = jnp.where(mine, jnp.concatenate(o_heads, axis=-1), g_o)
        so_ref[grp, :] = g_o

        def proj(dst, dst_cols, group, w_cols):
            def run():
                dst[:, dst_cols] = _dot(hn_s[...], wmain_s[group, :, w_cols])
            return run

        def decay():
            glr = lax.dot_general(hn_s[...], wglr_s[...], _NT,
                                  preferred_element_type=_F32)
            la = _log_sigmoid(_dot(glr.astype(_BF), wg2_s[...]) + bg2_ref[...]) / TAU
            b_s[...] = _chunk_cumsum(la)

        lo, hi, full = slice(0, half), slice(half, D), slice(0, D_QK)
        tasks = [
            (out_slab(0), proj(q_s, full, 2, lo)),
            (out_slab(1), proj(k_s, full, 2, hi)),
            (out_slab(2), decay),
            (out_slab(3), out_norm),
            (proj(v_s, lo, 3, lo),),
            (proj(v_s, hi, 3, hi),),
            (proj(zg_s, lo, 4, lo),),
            (proj(zg_s, hi, 4, hi),),
        ]

        logsig = LRU_C * _log_sigmoid(lam_ref[...])
        for n in range(LRU_BLOCKS):
            ln = slice(n * LRU_BW, (n + 1) * LRU_BW)
            a, u = _lru_block(xc_s[:, ln], pre_s[:, 2 * n * LRU_BW:2 * (n + 1) * LRU_BW],
                              bgx_ref[:, ln], bga_ref[:, ln], logsig[:, ln])
            h = _linear_scan(a, u, h_s[:, ln])
            h_s[:, ln] = h[tm - 1:tm]
            mix_s[slot, :, ln] = (h * _silu(za_s[:, ln])).astype(_BF)
            for task in tasks[n]:
                task()

        tril = (lax.broadcasted_iota(jnp.int32, (CHUNK, CHUNK), 0)
                >= lax.broadcasted_iota(jnp.int32, (CHUNK, CHUNK), 1))

        def gla_chunk(c):
            rows = slice(c * CHUNK, (c + 1) * CHUNK)
            for h in range(HEADS):
                lk = slice(h * DK, (h + 1) * DK)
                lv = slice(h * DV, (h + 1) * DV)
                b = b_s[rows, lk]
                bl = b[CHUNK - 1:CHUNK]
                q = q_s[rows, lk] * (DK ** -0.5)
                k = k_s[rows, lk]
                v = v_s[rows, lv]
                qi = (q * jnp.exp(b)).astype(_BF)
                ki = (k * jnp.exp(-b)).astype(_BF)
                kd = (k * jnp.exp(bl - b)).astype(_BF)
                vb = v.astype(_BF)
                att = lax.dot_general(qi, ki, _NT, preferred_element_type=_F32)
                att = jnp.where(tril, att, 0.0).astype(_BF)
                st = st_s[h]
                o = _dot(att, vb) + lax.dot_general(qi, st.astype(_BF), _NT,
                                                    preferred_element_type=_F32)
                st_s[h] = st * jnp.exp(bl) + _dot(v.T.astype(_BF), kd)
                o = _rms(o) * ggla_ref[:, lv]
                mix_s[slot, rows, D + h * DV:D + (h + 1) * DV] = (
                    o * _silu(zg_s[rows, lv])).astype(_BF)

        def next_tail():
            tail = xa_s[tm:tm + SUBLANES, :]
            ctail_s[...] = tail
            xa_s[0:SUBLANES, :] = jnp.where(t_next != 0, tail, 0.0)

        next_tail()
        front_norm(xn_ref, adan_ref)
        gla_chunk(0)
        front_xa()
        gla_chunk(1)
        front_za()
        front_conv()
        gla_chunk(2)
        front_gates()
        gla_chunk(3)

    @pl.when(i == n_tiles)
    def _():
        for j in range(D // quarter):
            out_slab(j)()
        out_norm()

    @pl.when((t == nt - 1) & (i < n_tiles))
    def _():
        hlast_ref[0] = h_s[...]
        convout_ref[0] = ctail_s[SUBLANES - (CONV_W - 1):SUBLANES, :]
        for h in range(HEADS):
            sout_ref[0, h] = st_s[h].T


def _const_spec(shape):
    nd = len(shape)
    return pl.BlockSpec(shape, lambda i: (0,) * nd, pipeline_mode=pl.Buffered(1))


def _prompt(x, ada3, gnorm, w_in_t, w_out2d, convw, convb, wgx, wga, bgx, bga, lam,
            wg2, bg2, ggla, gfin, sqs, sk, seg, sv, ss0, tm=TM):
    bsz, seq, _ = x.shape
    nt = seq // tm
    n_tiles = bsz * nt
    ns = ss0.shape[0]
    s_per_step = -(-ns // n_tiles)
    s_blocks = ns // s_per_step
    assert tm // CHUNK == 4, "the step body spreads the next tile's input side over 4 GLA chunks"
    assert s_blocks * s_per_step == ns and s_blocks <= n_tiles and SUBLANES % s_per_step == 0
    kern = functools.partial(_prompt_kernel, tm=tm, nt=nt, n_tiles=n_tiles,
                             s_per_step=s_per_step, s_blocks=s_blocks)
    out_shape = (
        jax.ShapeDtypeStruct((bsz, seq, D), _F32),
        jax.ShapeDtypeStruct((bsz, 1, D), _F32),
        jax.ShapeDtypeStruct((bsz, CONV_W - 1, D), _F32),
        jax.ShapeDtypeStruct((bsz, HEADS, DK, DV), _F32),
        jax.ShapeDtypeStruct((ns, D), _F32),
        jax.ShapeDtypeStruct(ss0.shape, _F32),
    )
    s_block = lambda i: (jnp.minimum(i, s_blocks - 1), 0, 0, 0)
    cur = lambda i: jnp.minimum(i, n_tiles - 1)
    prev = lambda i: jnp.maximum(i - 1, 0)
    nxt = lambda i: jnp.minimum(i + 1, n_tiles - 1)
    tile = lambda f: (lambda i: (f(i) // nt, lax.rem(f(i), nt), 0))
    seq_of = lambda f: (lambda i: (f(i) // nt, 0, 0))
    hbm = pl.BlockSpec(memory_space=pl.ANY)
    in_specs = [
        pl.BlockSpec((1, tm, D), tile(nxt)),
        pl.BlockSpec((1, tm, D), tile(prev)),
        pl.BlockSpec((1, 3, D), seq_of(nxt)),
        pl.BlockSpec((1, 3, D), seq_of(prev)),
        _const_spec((1, D)),
        hbm, hbm,
        _const_spec((CONV_W, D)),
        _const_spec((1, D)),
        _const_spec((LRU_BLOCKS, LRU_BW, LRU_BW)),
        _const_spec((LRU_BLOCKS, LRU_BW, LRU_BW)),
        _const_spec((1, D)),
        _const_spec((1, D)),
        _const_spec((1, D)),
        _const_spec((RANK, D_QK)),
        _const_spec((1, D_QK)),
        _const_spec((1, D)),
        _const_spec((1, D)),
        _const_spec((ns, D_QK)),
        _const_spec((ns, D_QK)),
        _const_spec((ns, D_QK)),
        _const_spec((ns, D)),
        pl.BlockSpec((s_per_step, HEADS, DK, DV), s_block),
    ]
    out_specs = (
        pl.BlockSpec((1, tm, D), tile(prev)),
        pl.BlockSpec((1, 1, D), seq_of(cur)),
        pl.BlockSpec((1, CONV_W - 1, D), seq_of(cur)),
        pl.BlockSpec((1, HEADS, DK, DV), lambda i: (cur(i) // nt, 0, 0, 0)),
        pl.BlockSpec((ns, D), lambda i: (0, 0)),
        pl.BlockSpec((s_per_step, HEADS, DK, DV), s_block),
    )
    scratch = [
        pltpu.VMEM((N_GROUPS, D, D), _BF),
        pltpu.VMEM((D_MIX, D), _BF),
        pltpu.VMEM((RANK_PAD, D), _BF),
        pltpu.VMEM((RANK_PAD, D_QK), _BF),
        pltpu.VMEM((LRU_BLOCKS, LRU_BW, 2 * LRU_BW), _BF),
        pltpu.VMEM((2, STAGE_ROWS, D), _F32),
        pltpu.SemaphoreType.DMA((2,)),
        pltpu.VMEM((tm, D), _BF),
        pltpu.VMEM((tm + SUBLANES, D), _F32),
        pltpu.VMEM((tm, D), _F32),
        pltpu.VMEM((tm, D), _F32),
        pltpu.VMEM((tm, 2 * D), _F32),
        pltpu.VMEM((tm, D_QK), _F32),
        pltpu.VMEM((tm, D_QK), _F32),
        pltpu.VMEM((tm, D), _F32),
        pltpu.VMEM((tm, D), _F32),
        pltpu.VMEM((tm, D_QK), _F32),
        pltpu.VMEM((2, tm, D_MIX), _BF),
        pltpu.VMEM((tm, D), _F32),
        pltpu.VMEM((SUBLANES, D), _F32),
        pltpu.VMEM((1, D), _F32),
        pltpu.VMEM((HEADS, DV, DK), _F32),
    ]
    return pl.pallas_call(
        kern,
        out_shape=out_shape,
        grid=(n_tiles + 1,),
        in_specs=in_specs,
        out_specs=out_specs,
        scratch_shapes=scratch,
        compiler_params=pltpu.CompilerParams(
            dimension_semantics=("arbitrary",),
            vmem_limit_bytes=VMEM_LIMIT),
        name="prompt_layer",
    )(x, x, ada3, ada3, gnorm, w_in_t, w_out2d, convw, convb, wgx, wga, bgx, bga, lam,
      wg2, bg2, ggla, gfin, sqs, sk, seg, sv, ss0)


def _decode_in_kernel(c_ref, wada_ref, bada_ref, x_ref, h0_ref, c0_ref, c1_ref, c2_ref,
                      gnorm_ref, wt_ref, convw_ref, convb_ref, wgx_ref, wga_ref,
                      bgx_ref, bga_ref, lam_ref, wg2_ref, bg2_ref,
                      adap_ref, gate_ref, hnew_ref, xa_ref, ylru_ref, qs_ref, k_ref, eg_ref,
                      v_ref, zg_ref, *, bp):
    sc = _silu(c_ref[...]).astype(_BF)
    ada = [_dot(sc, wada_ref[:, j * D:(j + 1) * D].astype(_BF)) + bada_ref[:, j * D:(j + 1) * D]
           for j in range(3)]
    for j in range(3):
        for b in range(bp):
            adap_ref[b, j:j + 1, :] = ada[j][b:b + 1, :]
    shift, scale = ada[0][bp:, :], ada[1][bp:, :]
    gate_ref[...] = ada[2][bp:, :]

    x = x_ref[...]
    gain = gnorm_ref[...] * (1.0 + scale)
    hnb = (_rms(x) * gain + shift).astype(_BF)

    def proj(c0, width):
        return lax.dot_general(hnb, wt_ref[c0:c0 + width, :].astype(_BF), _NT,
                               preferred_element_type=_F32)

    xa = proj(0, D)
    za = proj(D, D)
    qs_ref[...] = proj(2 * D, D_QK) * (DK ** -0.5)
    k_ref[...] = proj(2 * D + D_QK, D_QK)
    v_ref[...] = proj(3 * D, D)
    zg_ref[...] = proj(O_GLR + RANK, D)
    pad_rows = lambda a: jnp.concatenate(
        [a, jnp.zeros((RANK_PAD - RANK, a.shape[1]), _F32)], axis=0).astype(_BF)
    glr = lax.dot_general(hnb, pad_rows(wt_ref[O_GLR:O_GLR + RANK, :]), _NT,
                          preferred_element_type=_F32)
    la = _log_sigmoid(_dot(glr.astype(_BF), pad_rows(wg2_ref[...])) + bg2_ref[...]) / TAU
    eg_ref[...] = jnp.exp(la)
    xa_ref[...] = xa

    xc = convb_ref[...]
    for j, cj in enumerate((c0_ref[...], c1_ref[...], c2_ref[...], xa)):
        xc = xc + cj * convw_ref[j:j + 1, :]
    xcb = xc.astype(_BF)
    logsig = LRU_C * _log_sigmoid(lam_ref[...])
    for n in range(LRU_BLOCKS):
        ln = slice(n * LRU_BW, (n + 1) * LRU_BW)
        pre = jnp.concatenate([_dot(xcb[:, ln], wgx_ref[n].astype(_BF)),
                               _dot(xcb[:, ln], wga_ref[n].astype(_BF))], axis=-1)
        a, u = _lru_block(xc[:, ln], pre, bgx_ref[:, ln], bga_ref[:, ln], logsig[:, ln])
        h = a * h0_ref[:, ln] + u
        hnew_ref[:, ln] = h
        ylru_ref[:, ln] = h * _silu(za[:, ln])


def _decode_in(c_all, w_ada, b_ada, x, h0, c0, c1, c2, gnorm, w_in_t, convw, convb, wgx, wga,
               bgx, bga, lam, wg2, bg2, bp):
    n = x.shape[0]
    assert bp % SUBLANES == 0, "decode rows must start on a sublane-tile boundary of the adaLN rows"
    f = lambda w: jax.ShapeDtypeStruct((n, w), _F32)
    return pl.pallas_call(
        functools.partial(_decode_in_kernel, bp=bp),
        out_shape=(jax.ShapeDtypeStruct((bp, 3, D), _F32), f(D),
                   f(D), f(D), f(D), f(D_QK), f(D_QK), f(D_QK), f(D), f(D)),
        compiler_params=pltpu.CompilerParams(vmem_limit_bytes=VMEM_LIMIT),
        name="decode_in",
    )(c_all, w_ada, b_ada, x, h0, c0, c1, c2, gnorm, w_in_t, convw, convb, wgx, wga, bgx, bga,
      lam, wg2, bg2)


def _decode_out_kernel(x_ref, gate_ref, o_ref, zg_ref, ylru_ref, ggla_ref, wout_ref, gfin_ref,
                       y_ref):
    parts = [ylru_ref[...].astype(_BF)]
    for h in range(HEADS):
        lv = slice(h * DV, (h + 1) * DV)
        o = _rms(o_ref[:, lv]) * ggla_ref[:, lv]
        parts.append((o * _silu(zg_ref[:, lv])).astype(_BF))
    mix = jnp.concatenate(parts, axis=-1)
    y = x_ref[...] + gate_ref[...] * _dot(mix, wout_ref[...].astype(_BF))
    y_ref[...] = _rms(y) * gfin_ref[...]


def _decode_out(x, gate, o, zg, ylru, ggla, w_out2d, gfin):
    return pl.pallas_call(
        _decode_out_kernel,
        out_shape=jax.ShapeDtypeStruct(x.shape, _F32),
        compiler_params=pltpu.CompilerParams(vmem_limit_bytes=VMEM_LIMIT),
        name="decode_out",
    )(x, gate, o, zg, ylru, ggla, w_out2d, gfin)


def kernel(x_prompt, x_sample, state_lru_h, state_lru_conv, state_gla, c_prompt, c_sample,
           g_norm, w_ada, b_ada, w_in, conv_w, conv_b, w_gate_x, b_gate_x, w_gate_a, b_gate_a,
           lru_lambda, w_gla_g2, b_gla_g2, g_gla_norm, w_out, g_final):
    depth = g_norm.shape[0]
    assert depth == 1, "single-layer trunk"
    bp = x_prompt.shape[0]
    ns = x_sample.shape[0]

    w_in_t = jnp.swapaxes(w_in, 1, 2).reshape(w_in.shape[-1], D)
    w_out2d = w_out.reshape(D_MIX, D)
    wg2 = w_gla_g2.reshape(RANK, D_QK)
    wgx = w_gate_x.reshape(LRU_BLOCKS, LRU_BW, LRU_BW)
    wga = w_gate_a.reshape(LRU_BLOCKS, LRU_BW, LRU_BW)
    row = lambda a: a.reshape(1, -1)
    gnorm, convb = row(g_norm), row(conv_b)
    bgx, bga, lam = row(b_gate_x), row(b_gate_a), row(lru_lambda)
    bg2, ggla, gfin = row(b_gla_g2), row(g_gla_norm), row(g_final)
    convw = conv_w.reshape(CONV_W, D)

    xs = x_sample.reshape(ns, D)
    conv0 = state_lru_conv.reshape(ns, CONV_W - 1, D)
    ada_p, gate_s, hs, xa, ylru, qs, ks, eg, vs, zg = _decode_in(
        jnp.concatenate([c_prompt, c_sample], axis=0), w_ada.reshape(D, 3 * D), row(b_ada),
        xs, state_lru_h.reshape(ns, D), conv0[:, 0], conv0[:, 1], conv0[:, 2],
        gnorm, w_in_t, convw, convb, wgx, wga, bgx, bga, lam, wg2, bg2, bp)

    yp, hp, cp, sp, o, ss = _prompt(x_prompt, ada_p, gnorm, w_in_t, w_out2d, convw,
                                    convb, wgx, wga, bgx, bga, lam, wg2, bg2, ggla, gfin,
                                    qs, ks, eg, vs, state_gla.reshape(ns, HEADS, DK, DV))

    ys = _decode_out(xs, gate_s, o, zg, ylru, ggla, w_out2d, gfin)
    cs = jnp.stack([conv0[:, 1], conv0[:, 2], xa], axis=1)

    return (yp, ys.reshape(ns, 1, D), hp.reshape(1, bp, D), cp[None], sp[None],
            hs[None], cs[None], ss[None])
```

```python
import functools

import jax
import jax.numpy as jnp
from jax import lax
from jax.experimental import pallas as pl
from jax.experimental.pallas import tpu as pltpu

EPS = 1e-6
D = 1024
LRU_BLOCKS = 8
LRU_BW = D // LRU_BLOCKS
LRU_C = 8.0
CONV_W = 4
HEADS = 4
DV = D // HEADS
DK = DV // 2
RANK = 16
RANK_PAD = 128
TAU = 16.0
CHUNK = 64
D_QK = HEADS * DK
D_MIX = 2 * D
O_GLR = 4 * D
N_GROUPS = 5

SUBLANES = 8
TM = 256
STAGE_ROWS = 256
VMEM_LIMIT = 56 * 1024 * 1024
NEG_LOG2E = -1.4426950408889634

_BF = jnp.bfloat16
_F32 = jnp.float32

_NT = (((1,), (1,)), ((), ()))


def _sigmoid(x):
    return 1.0 / (1.0 + jnp.exp2(x * NEG_LOG2E))


def _silu(x):
    return x * _sigmoid(x)


def _log_sigmoid(x):
    return jnp.minimum(x, 0.0) - jnp.log(1.0 + jnp.exp2(jnp.abs(x) * NEG_LOG2E))


def _rms(x):
    return x * lax.rsqrt(jnp.mean(x * x, axis=-1, keepdims=True) + EPS)


def _dot(a, b):
    return jnp.dot(a, b, preferred_element_type=_F32)


def _group_scan(a3, u3):
    row = lax.broadcasted_iota(jnp.int32, a3.shape, 1)
    for s in (1, 2, 4):
        a_sh = pltpu.roll(a3, s, axis=1)
        u_sh = pltpu.roll(u3, s, axis=1)
        m = row >= s
        u3 = jnp.where(m, a3 * u_sh + u3, u3)
        a3 = jnp.where(m, a3 * a_sh, a3)
    return a3, u3


def _group_cumsum(x3):
    row = lax.broadcasted_iota(jnp.int32, x3.shape, 1)
    for s in (1, 2, 4):
        x3 = jnp.where(row >= s, x3 + pltpu.roll(x3, s, axis=1), x3)
    return x3


def _linear_scan(a, u, h0):
    r, l = a.shape
    g = r // SUBLANES
    a3, u3 = _group_scan(a.reshape(g, SUBLANES, l), u.reshape(g, SUBLANES, l))
    carry = h0
    out = []
    for i in range(g):
        hg = a3[i] * carry + u3[i]
        out.append(hg)
        carry = hg[SUBLANES - 1:SUBLANES]
    return jnp.concatenate(out, axis=0)


def _chunk_cumsum(x):
    r, l = x.shape
    g = r // SUBLANES
    per_chunk = CHUNK // SUBLANES
    x3 = _group_cumsum(x.reshape(g, SUBLANES, l))
    out = []
    carry = None
    for i in range(g):
        xg = x3[i] if i % per_chunk == 0 else x3[i] + carry
        out.append(xg)
        carry = xg[SUBLANES - 1:SUBLANES]
    return jnp.concatenate(out, axis=0)


def _lru_block(xc_n, pre, bgx, bga, c_logsig):
    gx = _sigmoid(pre[:, :LRU_BW] + bgx)
    ga = _sigmoid(pre[:, LRU_BW:] + bga)
    log_a = ga * c_logsig
    a = jnp.exp(log_a)
    om = 1.0 - a * a
    mult = jnp.where(om > 0.0, om * lax.rsqrt(om), 0.0)
    u = mult * (gx * xc_n)
    return a, u


def _gla_token_step(qs, k, eg, v, s0):
    eye = (lax.broadcasted_iota(jnp.int32, (DK, DK), 0)
           == lax.broadcasted_iota(jnp.int32, (DK, DK), 1))

    def to_col(row):
        return jnp.sum(jnp.where(eye, jnp.broadcast_to(row, (DK, DK)), 0.0),
                       axis=-1, keepdims=True)

    qk = jnp.sum(qs * k, axis=-1, keepdims=True)
    qg = jnp.broadcast_to(qs * eg, (SUBLANES, DK)).astype(_BF)
    o = _dot(qg, s0.astype(_BF))[0:1] + qk * v
    return o, to_col(eg) * s0 + to_col(k) * v


def _stage_blocks(n, src_block, store_block, stage, sem):
    def copy(i, slot):
        return pltpu.make_async_copy(src_block(i), stage.at[slot], sem.at[slot])

    copy(0, 0).start()

    def body(i, carry):
        slot = lax.rem(i, 2)

        @pl.when(i + 1 < n)
        def _():
            copy(i + 1, 1 - slot).start()

        copy(i, slot).wait()
        store_block(i, stage[slot].astype(_BF))
        return carry

    lax.fori_loop(0, n, body, 0)


def _stage_in_proj(wt_hbm, wmain_s, stage, sem):
    per_group = D // STAGE_ROWS

    def block(g, c):
        row0 = g * D + (g // (N_GROUPS - 1)) * RANK + c * STAGE_ROWS
        return wt_hbm.at[pl.ds(pl.multiple_of(row0, SUBLANES), STAGE_ROWS), :]

    def copy(g, c):
        return pltpu.make_async_copy(block(g, c), stage.at[c % 2], sem.at[c % 2])

    copy(0, 0).start()

    def body(g, carry):
        for c in range(per_group):
            if c + 1 < per_group:
                copy(g, c + 1).start()
            else:
                @pl.when(g + 1 < N_GROUPS)
                def _():
                    copy(g + 1, 0).start()
            copy(g, c).wait()
            wmain_s[g, :, c * STAGE_ROWS:(c + 1) * STAGE_ROWS] = stage[c % 2].T.astype(_BF)
        return carry

    lax.fori_loop(0, N_GROUPS, body, 0)


def _prompt_kernel(xn_ref, xp_ref, adan_ref, adap_ref, gnorm_ref, wt_hbm, wout_hbm,
                   convw_ref, convb_ref, wgx_ref, wga_ref, bgx_ref, bga_ref, lam_ref, wg2_ref,
                   bg2_ref, ggla_ref, gfin_ref, sqs_ref, sk_ref, seg_ref, sv_ref, ss0_ref,
                   sx_ref, sgate_ref, szg_ref, sylru_ref,
                   y_ref, hlast_ref, convout_ref, sout_ref, ys_ref, ssnew_ref,
                   wmain_s, wout_s, wglr_s, wg2_s, wgate_s, stage, sem,
                   hn_s, xa_s, za_s, xc_s, pre_s, q_s, k_s, v_s, zg_s, b_s, mix_s, yacc_s, ctail_s,
                   h_s, st_s, so_s, *, tm, nt, n_tiles, s_per_step, s_blocks):
    i = pl.program_id(0)
    t = lax.rem(jnp.minimum(i, n_tiles - 1), nt)
    t_next = lax.rem(jnp.minimum(i + 1, n_tiles - 1), nt)
    slot = lax.rem(i, 2)
    half = D // 2
    quarter = D // 4

    def front_norm(x_ref, ada_ref):
        x = x_ref[0]
        gain = gnorm_ref[...] * (1.0 + ada_ref[0, 1:2, :])
        hn_s[...] = (_rms(x) * gain + ada_ref[0, 0:1, :]).astype(_BF)

    def front_xa():
        xa_s[SUBLANES:SUBLANES + tm, :] = _dot(hn_s[...], wmain_s[0])

    def front_za():
        za_s[...] = _dot(hn_s[...], wmain_s[1])

    def front_conv():
        xe = xa_s[...]
        xc = convb_ref[...]
        for j in range(CONV_W):
            lag = CONV_W - 1 - j
            xj = xe if lag == 0 else pltpu.roll(xe, lag, axis=0)
            xc = xc + xj[SUBLANES:, :] * convw_ref[j:j + 1, :]
        xc_s[...] = xc

    def front_gates():
        for n in range(LRU_BLOCKS):
            pre_s[:, 2 * n * LRU_BW:2 * (n + 1) * LRU_BW] = _dot(
                xc_s[:, n * LRU_BW:(n + 1) * LRU_BW].astype(_BF), wgate_s[n])

    @pl.when(i == 0)
    def _():
        def out_rows(j):
            return pl.ds(pl.multiple_of(j * STAGE_ROWS, STAGE_ROWS), STAGE_ROWS)

        def store_out(j, val):
            wout_s[out_rows(j), :] = val

        _stage_in_proj(wt_hbm, wmain_s, stage, sem)
        _stage_blocks(D_MIX // STAGE_ROWS, lambda j: wout_hbm.at[out_rows(j), :], store_out,
                      stage, sem)
        glr_rows = pltpu.make_async_copy(wt_hbm.at[pl.ds(O_GLR, RANK), :],
                                         stage.at[0, pl.ds(0, RANK), :], sem.at[0])
        glr_rows.start()
        wglr_s[...] = jnp.zeros_like(wglr_s)
        wg2_s[...] = jnp.zeros_like(wg2_s)
        wg2_s[0:RANK, :] = wg2_ref[...].astype(_BF)
        wgate_s[:, :, 0:LRU_BW] = wgx_ref[...].astype(_BF)
        wgate_s[:, :, LRU_BW:2 * LRU_BW] = wga_ref[...].astype(_BF)
        glr_rows.wait()
        wglr_s[0:RANK, :] = stage[0, 0:RANK, :].astype(_BF)
        so_s[...] = jnp.zeros_like(so_s)
        mix_s[...] = jnp.zeros_like(mix_s)
        xa_s[0:SUBLANES, :] = jnp.zeros((SUBLANES, D), _F32)
        front_norm(xp_ref, adap_ref)
        front_xa()
        front_za()
        front_conv()
        front_gates()

    @pl.when(t == 0)
    def _():
        h_s[...] = jnp.zeros_like(h_s)
        st_s[...] = jnp.zeros_like(st_s)

    def out_slab(j):
        def run():
            cols = slice(j * quarter, (j + 1) * quarter)
            o = _dot(mix_s[1 - slot], wout_s[:, cols])
            yacc_s[:, cols] = xp_ref[0, :, cols] + adap_ref[0, 2:3, cols] * o
        return run

    def out_norm():
        y_ref[0] = _rms(yacc_s[...]) * gfin_ref[...]

    @pl.when(i < n_tiles)
    def _():
        first = i * s_per_step
        grp = pl.ds(pl.multiple_of((first // SUBLANES) * SUBLANES, SUBLANES), SUBLANES)
        r0 = lax.rem(first, SUBLANES)
        sub = lax.broadcasted_iota(jnp.int32, (SUBLANES, 1), 0)
        g_qs, g_k, g_eg, g_v = sqs_ref[grp, :], sk_ref[grp, :], seg_ref[grp, :], sv_ref[grp, :]
        g_o = so_s[grp, :]
        for j in range(s_per_step):
            mine = sub == r0 + j
            pick = lambda a: jnp.sum(jnp.where(mine, a, 0.0), axis=0, keepdims=True)
            qs_j, k_j, eg_j, v_j = pick(g_qs), pick(g_k), pick(g_eg), pick(g_v)
            o_heads = []
            for h in range(HEADS):
                lk = slice(h * DK, (h + 1) * DK)
                lv = slice(h * DV, (h + 1) * DV)
                o, snew = _gla_token_step(qs_j[:, lk], k_j[:, lk], eg_j[:, lk], v_j[:, lv],
                                          ss0_ref[j, h])
                o_heads.append(o)
                ssnew_ref[j, h] = snew
            g_o = jnp.where(mine, jnp.concatenate(o_heads, axis=-1), g_o)
        so_s[grp, :] = g_o

        def proj(dst, dst_cols, group, w_cols):
            def run():
                dst[:, dst_cols] = _dot(hn_s[...], wmain_s[group, :, w_cols])
            return run

        def decay():
            glr = lax.dot_general(hn_s[...], wglr_s[...], _NT,
                                  preferred_element_type=_F32)
            la = _log_sigmoid(_dot(glr.astype(_BF), wg2_s[...]) + bg2_ref[...]) / TAU
            b_s[...] = _chunk_cumsum(la)

        lo, hi, full = slice(0, half), slice(half, D), slice(0, D_QK)
        tasks = [
            (out_slab(0), proj(q_s, full, 2, lo)),
            (out_slab(1), proj(k_s, full, 2, hi)),
            (out_slab(2), decay),
            (out_slab(3), out_norm),
            (proj(v_s, lo, 3, lo),),
            (proj(v_s, hi, 3, hi),),
            (proj(zg_s, lo, 4, lo),),
            (proj(zg_s, hi, 4, hi),),
        ]

        logsig = LRU_C * _log_sigmoid(lam_ref[...])
        for n in range(LRU_BLOCKS):
            ln = slice(n * LRU_BW, (n + 1) * LRU_BW)
            a, u = _lru_block(xc_s[:, ln], pre_s[:, 2 * n * LRU_BW:2 * (n + 1) * LRU_BW],
                              bgx_ref[:, ln], bga_ref[:, ln], logsig[:, ln])
            h = _linear_scan(a, u, h_s[:, ln])
            h_s[:, ln] = h[tm - 1:tm]
            mix_s[slot, :, ln] = (h * _silu(za_s[:, ln])).astype(_BF)
            for task in tasks[n]:
                task()

        tril = (lax.broadcasted_iota(jnp.int32, (CHUNK, CHUNK), 0)
                >= lax.broadcasted_iota(jnp.int32, (CHUNK, CHUNK), 1))

        def gla_chunk(c):
            rows = slice(c * CHUNK, (c + 1) * CHUNK)
            for h in range(HEADS):
                lk = slice(h * DK, (h + 1) * DK)
                lv = slice(h * DV, (h + 1) * DV)
                b = b_s[rows, lk]
                bl = b[CHUNK - 1:CHUNK]
                q = q_s[rows, lk] * (DK ** -0.5)
                k = k_s[rows, lk]
                v = v_s[rows, lv]
                qi = (q * jnp.exp(b)).astype(_BF)
                ki = (k * jnp.exp(-b)).astype(_BF)
                kd = (k * jnp.exp(bl - b)).astype(_BF)
                vb = v.astype(_BF)
                att = lax.dot_general(qi, ki, _NT, preferred_element_type=_F32)
                att = jnp.where(tril, att, 0.0).astype(_BF)
                st = st_s[h]
                o = _dot(att, vb) + lax.dot_general(qi, st.astype(_BF), _NT,
                                                    preferred_element_type=_F32)
                st_s[h] = st * jnp.exp(bl) + _dot(v.T.astype(_BF), kd)
                o = _rms(o) * ggla_ref[:, lv]
                mix_s[slot, rows, D + h * DV:D + (h + 1) * DV] = (
                    o * _silu(zg_s[rows, lv])).astype(_BF)

        def next_tail():
            tail = xa_s[tm:tm + SUBLANES, :]
            ctail_s[...] = tail
            xa_s[0:SUBLANES, :] = jnp.where(t_next != 0, tail, 0.0)

        next_tail()
        front_norm(xn_ref, adan_ref)
        gla_chunk(0)
        front_xa()
        gla_chunk(1)
        front_za()
        front_conv()
        gla_chunk(2)
        front_gates()
        gla_chunk(3)

    @pl.when(i == n_tiles)
    def _():
        for j in range(D // quarter):
            out_slab(j)()
        out_norm()
        parts = [sylru_ref[...].astype(_BF)]
        for h in range(HEADS):
            lv = slice(h * DV, (h + 1) * DV)
            o = _rms(so_s[:, lv]) * ggla_ref[:, lv]
            parts.append((o * _silu(szg_ref[:, lv])).astype(_BF))
        ys = sx_ref[...] + sgate_ref[...] * _dot(jnp.concatenate(parts, axis=-1), wout_s[...])
        ys_ref[...] = _rms(ys) * gfin_ref[...]

    @pl.when((t == nt - 1) & (i < n_tiles))
    def _():
        hlast_ref[0] = h_s[...]
        convout_ref[0] = ctail_s[SUBLANES - (CONV_W - 1):SUBLANES, :]
        for h in range(HEADS):
            sout_ref[0, h] = st_s[h].T


def _const_spec(shape):
    nd = len(shape)
    return pl.BlockSpec(shape, lambda i: (0,) * nd, pipeline_mode=pl.Buffered(1))


def _prompt(x, ada3, gnorm, w_in_t, w_out2d, convw, convb, wgx, wga, bgx, bga, lam,
            wg2, bg2, ggla, gfin, sqs, sk, seg, sv, ss0, sx, sgate, szg, sylru, tm=TM):
    bsz, seq, _ = x.shape
    nt = seq // tm
    n_tiles = bsz * nt
    ns = ss0.shape[0]
    s_per_step = -(-ns // n_tiles)
    s_blocks = ns // s_per_step
    assert tm // CHUNK == 4, "the step body spreads the next tile's input side over 4 GLA chunks"
    assert s_blocks * s_per_step == ns and s_blocks <= n_tiles and SUBLANES % s_per_step == 0
    kern = functools.partial(_prompt_kernel, tm=tm, nt=nt, n_tiles=n_tiles,
                             s_per_step=s_per_step, s_blocks=s_blocks)
    out_shape = (
        jax.ShapeDtypeStruct((bsz, seq, D), _F32),
        jax.ShapeDtypeStruct((bsz, 1, D), _F32),
        jax.ShapeDtypeStruct((bsz, CONV_W - 1, D), _F32),
        jax.ShapeDtypeStruct((bsz, HEADS, DK, DV), _F32),
        jax.ShapeDtypeStruct((ns, D), _F32),
        jax.ShapeDtypeStruct(ss0.shape, _F32),
    )
    s_block = lambda i: (jnp.minimum(i, s_blocks - 1), 0, 0, 0)
    cur = lambda i: jnp.minimum(i, n_tiles - 1)
    prev = lambda i: jnp.maximum(i - 1, 0)
    nxt = lambda i: jnp.minimum(i + 1, n_tiles - 1)
    tile = lambda f: (lambda i: (f(i) // nt, lax.rem(f(i), nt), 0))
    seq_of = lambda f: (lambda i: (f(i) // nt, 0, 0))
    hbm = pl.BlockSpec(memory_space=pl.ANY)
    in_specs = [
        pl.BlockSpec((1, tm, D), tile(nxt)),
        pl.BlockSpec((1, tm, D), tile(prev)),
        pl.BlockSpec((1, 3, D), seq_of(nxt)),
        pl.BlockSpec((1, 3, D), seq_of(prev)),
        _const_spec((1, D)),
        hbm, hbm,
        _const_spec((CONV_W, D)),
        _const_spec((1, D)),
        _const_spec((LRU_BLOCKS, LRU_BW, LRU_BW)),
        _const_spec((LRU_BLOCKS, LRU_BW, LRU_BW)),
        _const_spec((1, D)),
        _const_spec((1, D)),
        _const_spec((1, D)),
        _const_spec((RANK, D_QK)),
        _const_spec((1, D_QK)),
        _const_spec((1, D)),
        _const_spec((1, D)),
        _const_spec((ns, D_QK)),
        _const_spec((ns, D_QK)),
        _const_spec((ns, D_QK)),
        _const_spec((ns, D)),
        pl.BlockSpec((s_per_step, HEADS, DK, DV), s_block),
        _const_spec((ns, D)),
        _const_spec((ns, D)),
        _const_spec((ns, D)),
        _const_spec((ns, D)),
    ]
    out_specs = (
        pl.BlockSpec((1, tm, D), tile(prev)),
        pl.BlockSpec((1, 1, D), seq_of(cur)),
        pl.BlockSpec((1, CONV_W - 1, D), seq_of(cur)),
        pl.BlockSpec((1, HEADS, DK, DV), lambda i: (cur(i) // nt, 0, 0, 0)),
        pl.BlockSpec((ns, D), lambda i: (0, 0)),
        pl.BlockSpec((s_per_step, HEADS, DK, DV), s_block),
    )
    scratch = [
        pltpu.VMEM((N_GROUPS, D, D), _BF),
        pltpu.VMEM((D_MIX, D), _BF),
        pltpu.VMEM((RANK_PAD, D), _BF),
        pltpu.VMEM((RANK_PAD, D_QK), _BF),
        pltpu.VMEM((LRU_BLOCKS, LRU_BW, 2 * LRU_BW), _BF),
        pltpu.VMEM((2, STAGE_ROWS, D), _F32),
        pltpu.SemaphoreType.DMA((2,)),
        pltpu.VMEM((tm, D), _BF),
        pltpu.VMEM((tm + SUBLANES, D), _F32),
        pltpu.VMEM((tm, D), _F32),
        pltpu.VMEM((tm, D), _F32),
        pltpu.VMEM((tm, 2 * D), _F32),
        pltpu.VMEM((tm, D_QK), _F32),
        pltpu.VMEM((tm, D_QK), _F32),
        pltpu.VMEM((tm, D), _F32),
        pltpu.VMEM((tm, D), _F32),
        pltpu.VMEM((tm, D_QK), _F32),
        pltpu.VMEM((2, tm, D_MIX), _BF),
        pltpu.VMEM((tm, D), _F32),
        pltpu.VMEM((SUBLANES, D), _F32),
        pltpu.VMEM((1, D), _F32),
        pltpu.VMEM((HEADS, DV, DK), _F32),
        pltpu.VMEM((ns, D), _F32),
    ]
    return pl.pallas_call(
        kern,
        out_shape=out_shape,
        grid=(n_tiles + 1,),
        in_specs=in_specs,
        out_specs=out_specs,
        scratch_shapes=scratch,
        compiler_params=pltpu.CompilerParams(
            dimension_semantics=("arbitrary",),
            vmem_limit_bytes=VMEM_LIMIT),
        name="prompt_layer",
    )(x, x, ada3, ada3, gnorm, w_in_t, w_out2d, convw, convb, wgx, wga, bgx, bga, lam,
      wg2, bg2, ggla, gfin, sqs, sk, seg, sv, ss0, sx, sgate, szg, sylru)


def _decode_in_kernel(c_ref, wada_ref, bada_ref, x_ref, h0_ref, c0_ref, c1_ref, c2_ref,
                      gnorm_ref, wt_ref, convw_ref, convb_ref, wgx_ref, wga_ref,
                      bgx_ref, bga_ref, lam_ref, wg2_ref, bg2_ref,
                      adap_ref, gate_ref, hnew_ref, xa_ref, ylru_ref, qs_ref, k_ref, eg_ref,
                      v_ref, zg_ref, *, bp):
    sc = _silu(c_ref[...]).astype(_BF)
    ada = [_dot(sc, wada_ref[:, j * D:(j + 1) * D].astype(_BF)) + bada_ref[:, j * D:(j + 1) * D]
           for j in range(3)]
    for j in range(3):
        for b in range(bp):
            adap_ref[b, j:j + 1, :] = ada[j][b:b + 1, :]
    shift, scale = ada[0][bp:, :], ada[1][bp:, :]
    gate_ref[...] = ada[2][bp:, :]

    x = x_ref[...]
    gain = gnorm_ref[...] * (1.0 + scale)
    hnb = (_rms(x) * gain + shift).astype(_BF)

    def proj(c0, width):
        return lax.dot_general(hnb, wt_ref[c0:c0 + width, :].astype(_BF), _NT,
                               preferred_element_type=_F32)

    xa = proj(0, D)
    za = proj(D, D)
    qs_ref[...] = proj(2 * D, D_QK) * (DK ** -0.5)
    k_ref[...] = proj(2 * D + D_QK, D_QK)
    v_ref[...] = proj(3 * D, D)
    zg_ref[...] = proj(O_GLR + RANK, D)
    pad_rows = lambda a: jnp.concatenate(
        [a, jnp.zeros((RANK_PAD - RANK, a.shape[1]), _F32)], axis=0).astype(_BF)
    glr = lax.dot_general(hnb, pad_rows(wt_ref[O_GLR:O_GLR + RANK, :]), _NT,
                          preferred_element_type=_F32)
    la = _log_sigmoid(_dot(glr.astype(_BF), pad_rows(wg2_ref[...])) + bg2_ref[...]) / TAU
    eg_ref[...] = jnp.exp(la)
    xa_ref[...] = xa

    xc = convb_ref[...]
    for j, cj in enumerate((c0_ref[...], c1_ref[...], c2_ref[...], xa)):
        xc = xc + cj * convw_ref[j:j + 1, :]
    xcb = xc.astype(_BF)
    logsig = LRU_C * _log_sigmoid(lam_ref[...])
    for n in range(LRU_BLOCKS):
        ln = slice(n * LRU_BW, (n + 1) * LRU_BW)
        pre = jnp.concatenate([_dot(xcb[:, ln], wgx_ref[n].astype(_BF)),
                               _dot(xcb[:, ln], wga_ref[n].astype(_BF))], axis=-1)
        a, u = _lru_block(xc[:, ln], pre, bgx_ref[:, ln], bga_ref[:, ln], logsig[:, ln])
        h = a * h0_ref[:, ln] + u
        hnew_ref[:, ln] = h
        ylru_ref[:, ln] = h * _silu(za[:, ln])


def _decode_in(c_all, w_ada, b_ada, x, h0, c0, c1, c2, gnorm, w_in_t, convw, convb, wgx, wga,
               bgx, bga, lam, wg2, bg2, bp):
    n = x.shape[0]
    assert bp % SUBLANES == 0, "decode rows must start on a sublane-tile boundary of the adaLN rows"
    f = lambda w: jax.ShapeDtypeStruct((n, w), _F32)
    return pl.pallas_call(
        functools.partial(_decode_in_kernel, bp=bp),
        out_shape=(jax.ShapeDtypeStruct((bp, 3, D), _F32), f(D),
                   f(D), f(D), f(D), f(D_QK), f(D_QK), f(D_QK), f(D), f(D)),
        compiler_params=pltpu.CompilerParams(vmem_limit_bytes=VMEM_LIMIT),
        name="decode_in",
    )(c_all, w_ada, b_ada, x, h0, c0, c1, c2, gnorm, w_in_t, convw, convb, wgx, wga, bgx, bga,
      lam, wg2, bg2)


def kernel(x_prompt, x_sample, state_lru_h, state_lru_conv, state_gla, c_prompt, c_sample,
           g_norm, w_ada, b_ada, w_in, conv_w, conv_b, w_gate_x, b_gate_x, w_gate_a, b_gate_a,
           lru_lambda, w_gla_g2, b_gla_g2, g_gla_norm, w_out, g_final):
    depth = g_norm.shape[0]
    assert depth == 1, "single-layer trunk"
    bp = x_prompt.shape[0]
    ns = x_sample.shape[0]

    w_in_t = jnp.swapaxes(w_in, 1, 2).reshape(w_in.shape[-1], D)
    w_out2d = w_out.reshape(D_MIX, D)
    wg2 = w_gla_g2.reshape(RANK, D_QK)
    wgx = w_gate_x.reshape(LRU_BLOCKS, LRU_BW, LRU_BW)
    wga = w_gate_a.reshape(LRU_BLOCKS, LRU_BW, LRU_BW)
    row = lambda a: a.reshape(1, -1)
    gnorm, convb = row(g_norm), row(conv_b)
    bgx, bga, lam = row(b_gate_x), row(b_gate_a), row(lru_lambda)
    bg2, ggla, gfin = row(b_gla_g2), row(g_gla_norm), row(g_final)
    convw = conv_w.reshape(CONV_W, D)

    xs = x_sample.reshape(ns, D)
    conv0 = state_lru_conv.reshape(ns, CONV_W - 1, D)
    ada_p, gate_s, hs, xa, ylru, qs, ks, eg, vs, zg = _decode_in(
        jnp.concatenate([c_prompt, c_sample], axis=0), w_ada.reshape(D, 3 * D), row(b_ada),
        xs, state_lru_h.reshape(ns, D), conv0[:, 0], conv0[:, 1], conv0[:, 2],
        gnorm, w_in_t, convw, convb, wgx, wga, bgx, bga, lam, wg2, bg2, bp)

    yp, hp, cp, sp, ys, ss = _prompt(x_prompt, ada_p, gnorm, w_in_t, w_out2d, convw,
                                     convb, wgx, wga, bgx, bga, lam, wg2, bg2, ggla, gfin,
                                     qs, ks, eg, vs, state_gla.reshape(ns, HEADS, DK, DV),
                                     xs, gate_s, zg, ylru)
    cs = jnp.stack([conv0[:, 1], conv0[:, 2], xa], axis=1)

    return (yp, ys.reshape(ns, 1, D), hp.reshape(1, bp, D), cp[None], sp[None],
            hs[None], cs[None], ss[None])
```

```python
import functools

import jax
import jax.numpy as jnp
from jax import lax
from jax.experimental import pallas as pl
from jax.experimental.pallas import tpu as pltpu

EPS = 1e-6
D = 1024
LRU_BLOCKS = 8
LRU_BW = D // LRU_BLOCKS
LRU_C = 8.0
CONV_W = 4
HEADS = 4
DV = D // HEADS
DK = DV // 2
RANK = 16
RANK_PAD = 128
TAU = 16.0
CHUNK = 64
D_QK = HEADS * DK
D_MIX = 2 * D
O_GLR = 4 * D
N_GROUPS = 5

SUBLANES = 8
TM = 256
STAGE_ROWS = 256
VMEM_LIMIT = 56 * 1024 * 1024
NEG_LOG2E = -1.4426950408889634

_BF = jnp.bfloat16
_F32 = jnp.float32

_NT = (((1,), (1,)), ((), ()))


def _sigmoid(x):
    return 1.0 / (1.0 + jnp.exp2(x * NEG_LOG2E))


def _silu(x):
    return x * _sigmoid(x)


def _log_sigmoid(x):
    return jnp.minimum(x, 0.0) - jnp.log(1.0 + jnp.exp2(jnp.abs(x) * NEG_LOG2E))


def _rms(x):
    return x * lax.rsqrt(jnp.mean(x * x, axis=-1, keepdims=True) + EPS)


def _dot(a, b):
    return jnp.dot(a, b, preferred_element_type=_F32)


def _group_scan(a3, u3):
    row = lax.broadcasted_iota(jnp.int32, a3.shape, 1)
    for s in (1, 2, 4):
        a_sh = pltpu.roll(a3, s, axis=1)
        u_sh = pltpu.roll(u3, s, axis=1)
        m = row >= s
        u3 = jnp.where(m, a3 * u_sh + u3, u3)
        a3 = jnp.where(m, a3 * a_sh, a3)
    return a3, u3


def _group_cumsum(x3):
    row = lax.broadcasted_iota(jnp.int32, x3.shape, 1)
    for s in (1, 2, 4):
        x3 = jnp.where(row >= s, x3 + pltpu.roll(x3, s, axis=1), x3)
    return x3


def _linear_scan(a, u, h0):
    r, l = a.shape
    g = r // SUBLANES
    a3, u3 = _group_scan(a.reshape(g, SUBLANES, l), u.reshape(g, SUBLANES, l))
    carry = h0
    out = []
    for i in range(g):
        hg = a3[i] * carry + u3[i]
        out.append(hg)
        carry = hg[SUBLANES - 1:SUBLANES]
    return jnp.concatenate(out, axis=0)


def _chunk_cumsum(x):
    r, l = x.shape
    g = r // SUBLANES
    per_chunk = CHUNK // SUBLANES
    x3 = _group_cumsum(x.reshape(g, SUBLANES, l))
    out = []
    carry = None
    for i in range(g):
        xg = x3[i] if i % per_chunk == 0 else x3[i] + carry
        out.append(xg)
        carry = xg[SUBLANES - 1:SUBLANES]
    return jnp.concatenate(out, axis=0)


def _lru_block(xc_n, pre, bgx, bga, c_logsig):
    gx = _sigmoid(pre[:, :LRU_BW] + bgx)
    ga = _sigmoid(pre[:, LRU_BW:] + bga)
    log_a = ga * c_logsig
    a = jnp.exp(log_a)
    om = 1.0 - a * a
    mult = jnp.where(om > 0.0, om * lax.rsqrt(om), 0.0)
    u = mult * (gx * xc_n)
    return a, u


def _gla_token_step(qs, k, eg, v, s0):
    eye = (lax.broadcasted_iota(jnp.int32, (DK, DK), 0)
           == lax.broadcasted_iota(jnp.int32, (DK, DK), 1))

    def to_col(row):
        return jnp.sum(jnp.where(eye, jnp.broadcast_to(row, (DK, DK)), 0.0),
                       axis=-1, keepdims=True)

    qk = jnp.sum(qs * k, axis=-1, keepdims=True)
    qg = jnp.broadcast_to(qs * eg, (SUBLANES, DK)).astype(_BF)
    o = _dot(qg, s0.astype(_BF))[0:1] + qk * v
    return o, to_col(eg) * s0 + to_col(k) * v


def _stage_blocks(n, src_block, store_block, stage, sem):
    def copy(i, slot):
        return pltpu.make_async_copy(src_block(i), stage.at[slot], sem.at[slot])

    copy(0, 0).start()

    def body(i, carry):
        slot = lax.rem(i, 2)

        @pl.when(i + 1 < n)
        def _():
            copy(i + 1, 1 - slot).start()

        copy(i, slot).wait()
        store_block(i, stage[slot].astype(_BF))
        return carry

    lax.fori_loop(0, n, body, 0)


def _stage_in_proj(wt_hbm, wmain_s, stage, sem):
    per_group = D // STAGE_ROWS

    def block(g, c):
        row0 = g * D + (g // (N_GROUPS - 1)) * RANK + c * STAGE_ROWS
        return wt_hbm.at[pl.ds(pl.multiple_of(row0, SUBLANES), STAGE_ROWS), :]

    def copy(g, c):
        return pltpu.make_async_copy(block(g, c), stage.at[c % 2], sem.at[c % 2])

    copy(0, 0).start()

    def body(g, carry):
        for c in range(per_group):
            if c + 1 < per_group:
                copy(g, c + 1).start()
            else:
                @pl.when(g + 1 < N_GROUPS)
                def _():
                    copy(g + 1, 0).start()
            copy(g, c).wait()
            wmain_s[g, :, c * STAGE_ROWS:(c + 1) * STAGE_ROWS] = stage[c % 2].T.astype(_BF)
        return carry

    lax.fori_loop(0, N_GROUPS, body, 0)


def _prompt_kernel(xn_ref, xp_ref, adan_ref, adap_ref, gnorm_ref, wt_hbm, wout_hbm,
                   convw_ref, convb_ref, wgx_ref, wga_ref, bgx_ref, bga_ref, lam_ref, wg2_ref,
                   bg2_ref, ggla_ref, gfin_ref, sqs_ref, sk_ref, seg_ref, sv_ref, ss0_ref,
                   sx_ref, sgate_ref, szg_ref, sylru_ref,
                   y_ref, hlast_ref, convout_ref, sout_ref, ys_ref, ssnew_ref,
                   wmain_s, wout_s, wglr_s, wg2_s, wgate_s, stage, sem,
                   hn_s, xa_s, za_s, xc_s, pre_s, q_s, k_s, v_s, zg_s, b_s, mix_s, yacc_s, ctail_s,
                   h_s, st_s, so_s, *, tm, nt, n_tiles, s_per_step, s_blocks):
    i = pl.program_id(0)
    t = lax.rem(jnp.minimum(i, n_tiles - 1), nt)
    t_next = lax.rem(jnp.minimum(i + 1, n_tiles - 1), nt)
    slot = lax.rem(i, 2)
    half = D // 2
    quarter = D // 4

    def front_norm(x_ref, ada_ref):
        x = x_ref[0]
        gain = gnorm_ref[...] * (1.0 + ada_ref[0, 1:2, :])
        hn_s[...] = (_rms(x) * gain + ada_ref[0, 0:1, :]).astype(_BF)

    def front_xa():
        xa_s[SUBLANES:SUBLANES + tm, :] = _dot(hn_s[...], wmain_s[0])

    def front_za():
        za_s[...] = _dot(hn_s[...], wmain_s[1])

    def front_conv():
        xe = xa_s[...]
        xc = convb_ref[...]
        for j in range(CONV_W):
            lag = CONV_W - 1 - j
            xj = xe if lag == 0 else pltpu.roll(xe, lag, axis=0)
            xc = xc + xj[SUBLANES:, :] * convw_ref[j:j + 1, :]
        xc_s[...] = xc

    def front_gates():
        for n in range(LRU_BLOCKS):
            pre_s[:, 2 * n * LRU_BW:2 * (n + 1) * LRU_BW] = _dot(
                xc_s[:, n * LRU_BW:(n + 1) * LRU_BW].astype(_BF), wgate_s[n])

    @pl.when(i == 0)
    def _():
        def out_rows(j):
            return pl.ds(pl.multiple_of(j * STAGE_ROWS, STAGE_ROWS), STAGE_ROWS)

        def store_out(j, val):
            wout_s[out_rows(j), :] = val

        _stage_in_proj(wt_hbm, wmain_s, stage, sem)
        _stage_blocks(D_MIX // STAGE_ROWS, lambda j: wout_hbm.at[out_rows(j), :], store_out,
                      stage, sem)
        glr_rows = pltpu.make_async_copy(wt_hbm.at[pl.ds(O_GLR, RANK), :],
                                         stage.at[0, pl.ds(0, RANK), :], sem.at[0])
        glr_rows.start()
        wglr_s[...] = jnp.zeros_like(wglr_s)
        wg2_s[...] = jnp.zeros_like(wg2_s)
        wg2_s[0:RANK, :] = wg2_ref[...].astype(_BF)
        wgate_s[:, :, 0:LRU_BW] = wgx_ref[...].astype(_BF)
        wgate_s[:, :, LRU_BW:2 * LRU_BW] = wga_ref[...].astype(_BF)
        glr_rows.wait()
        wglr_s[0:RANK, :] = stage[0, 0:RANK, :].astype(_BF)
        so_s[...] = jnp.zeros_like(so_s)
        mix_s[...] = jnp.zeros_like(mix_s)
        xa_s[0:SUBLANES, :] = jnp.zeros((SUBLANES, D), _F32)
        front_norm(xp_ref, adap_ref)
        front_xa()
        front_za()
        front_conv()
        front_gates()

    @pl.when(t == 0)
    def _():
        h_s[...] = jnp.zeros_like(h_s)
        st_s[...] = jnp.zeros_like(st_s)

    def out_slab(j):
        def run():
            cols = slice(j * quarter, (j + 1) * quarter)
            o = _dot(mix_s[1 - slot], wout_s[:, cols])
            yacc_s[:, cols] = xp_ref[0, :, cols] + adap_ref[0, 2:3, cols] * o
        return run

    def out_norm():
        y_ref[0] = _rms(yacc_s[...]) * gfin_ref[...]

    @pl.when(i < n_tiles)
    def _():
        def decode_gla():
            first = i * s_per_step
            grp = pl.ds(pl.multiple_of((first // SUBLANES) * SUBLANES, SUBLANES), SUBLANES)
            r0 = lax.rem(first, SUBLANES)
            sub = lax.broadcasted_iota(jnp.int32, (SUBLANES, 1), 0)
            g_qs, g_k, g_eg, g_v = sqs_ref[grp, :], sk_ref[grp, :], seg_ref[grp, :], sv_ref[grp, :]
            g_o = so_s[grp, :]
            for j in range(s_per_step):
                mine = sub == r0 + j
                pick = lambda a: jnp.sum(jnp.where(mine, a, 0.0), axis=0, keepdims=True)
                qs_j, k_j, eg_j, v_j = pick(g_qs), pick(g_k), pick(g_eg), pick(g_v)
                o_heads = []
                for h in range(HEADS):
                    lk = slice(h * DK, (h + 1) * DK)
                    lv = slice(h * DV, (h + 1) * DV)
                    o, snew = _gla_token_step(qs_j[:, lk], k_j[:, lk], eg_j[:, lk], v_j[:, lv],
                                              ss0_ref[j, h])
                    o_heads.append(o)
                    ssnew_ref[j, h] = snew
                g_o = jnp.where(mine, jnp.concatenate(o_heads, axis=-1), g_o)
            so_s[grp, :] = g_o

        def proj(dst, dst_cols, group, w_cols):
            def run():
                dst[:, dst_cols] = _dot(hn_s[...], wmain_s[group, :, w_cols])
            return run

        def decay():
            glr = lax.dot_general(hn_s[...], wglr_s[...], _NT,
                                  preferred_element_type=_F32)
            la = _log_sigmoid(_dot(glr.astype(_BF), wg2_s[...]) + bg2_ref[...]) / TAU
            b_s[...] = _chunk_cumsum(la)

        lo, hi, full = slice(0, half), slice(half, D), slice(0, D_QK)
        tasks = [
            (out_slab(0), proj(q_s, full, 2, lo)),
            (out_slab(1), proj(k_s, full, 2, hi)),
            (out_slab(2), decay),
            (out_slab(3), out_norm),
            (proj(v_s, lo, 3, lo),),
            (proj(v_s, hi, 3, hi),),
            (proj(zg_s, lo, 4, lo),),
            (proj(zg_s, hi, 4, hi),),
        ]

        logsig = LRU_C * _log_sigmoid(lam_ref[...])
        for n in range(LRU_BLOCKS):
            ln = slice(n * LRU_BW, (n + 1) * LRU_BW)
            a, u = _lru_block(xc_s[:, ln], pre_s[:, 2 * n * LRU_BW:2 * (n + 1) * LRU_BW],
                              bgx_ref[:, ln], bga_ref[:, ln], logsig[:, ln])
            h = _linear_scan(a, u, h_s[:, ln])
            h_s[:, ln] = h[tm - 1:tm]
            mix_s[slot, :, ln] = (h * _silu(za_s[:, ln])).astype(_BF)
            for task in tasks[n]:
                task()

        tril = (lax.broadcasted_iota(jnp.int32, (CHUNK, CHUNK), 0)
                >= lax.broadcasted_iota(jnp.int32, (CHUNK, CHUNK), 1))

        def gla_chunk(c):
            rows = slice(c * CHUNK, (c + 1) * CHUNK)
            for h in range(HEADS):
                lk = slice(h * DK, (h + 1) * DK)
                lv = slice(h * DV, (h + 1) * DV)
                b = b_s[rows, lk]
                bl = b[CHUNK - 1:CHUNK]
                q = q_s[rows, lk] * (DK ** -0.5)
                k = k_s[rows, lk]
                v = v_s[rows, lv]
                qi = (q * jnp.exp(b)).astype(_BF)
                ki = (k * jnp.exp(-b)).astype(_BF)
                kd = (k * jnp.exp(bl - b)).astype(_BF)
                vb = v.astype(_BF)
                att = lax.dot_general(qi, ki, _NT, preferred_element_type=_F32)
                att = jnp.where(tril, att, 0.0).astype(_BF)
                st = st_s[h]
                o = _dot(att, vb) + lax.dot_general(qi, st.astype(_BF), _NT,
                                                    preferred_element_type=_F32)
                st_s[h] = st * jnp.exp(bl) + _dot(v.T.astype(_BF), kd)
                o = _rms(o) * ggla_ref[:, lv]
                mix_s[slot, rows, D + h * DV:D + (h + 1) * DV] = (
                    o * _silu(zg_s[rows, lv])).astype(_BF)

        def next_tail():
            tail = xa_s[tm:tm + SUBLANES, :]
            ctail_s[...] = tail
            xa_s[0:SUBLANES, :] = jnp.where(t_next != 0, tail, 0.0)

        next_tail()
        front_norm(xn_ref, adan_ref)
        gla_chunk(0)
        front_xa()
        gla_chunk(1)
        decode_gla()
        front_za()
        front_conv()
        gla_chunk(2)
        front_gates()
        gla_chunk(3)

    @pl.when(i == n_tiles)
    def _():
        for j in range(D // quarter):
            out_slab(j)()
        out_norm()
        parts = [sylru_ref[...].astype(_BF)]
        for h in range(HEADS):
            lv = slice(h * DV, (h + 1) * DV)
            o = _rms(so_s[:, lv]) * ggla_ref[:, lv]
            parts.append((o * _silu(szg_ref[:, lv])).astype(_BF))
        ys = sx_ref[...] + sgate_ref[...] * _dot(jnp.concatenate(parts, axis=-1), wout_s[...])
        ys_ref[...] = _rms(ys) * gfin_ref[...]

    @pl.when((t == nt - 1) & (i < n_tiles))
    def _():
        hlast_ref[0] = h_s[...]
        convout_ref[0] = ctail_s[SUBLANES - (CONV_W - 1):SUBLANES, :]
        for h in range(HEADS):
            sout_ref[0, h] = st_s[h].T


def _const_spec(shape):
    nd = len(shape)
    return pl.BlockSpec(shape, lambda i: (0,) * nd, pipeline_mode=pl.Buffered(1))


def _prompt(x, ada3, gnorm, w_in_t, w_out2d, convw, convb, wgx, wga, bgx, bga, lam,
            wg2, bg2, ggla, gfin, sqs, sk, seg, sv, ss0, sx, sgate, szg, sylru, tm=TM):
    bsz, seq, _ = x.shape
    nt = seq // tm
    n_tiles = bsz * nt
    ns = ss0.shape[0]
    s_per_step = -(-ns // n_tiles)
    s_blocks = ns // s_per_step
    assert tm // CHUNK == 4, "the step body spreads the next tile's input side over 4 GLA chunks"
    assert s_blocks * s_per_step == ns and s_blocks <= n_tiles and SUBLANES % s_per_step == 0
    kern = functools.partial(_prompt_kernel, tm=tm, nt=nt, n_tiles=n_tiles,
                             s_per_step=s_per_step, s_blocks=s_blocks)
    out_shape = (
        jax.ShapeDtypeStruct((bsz, seq, D), _F32),
        jax.ShapeDtypeStruct((bsz, 1, D), _F32),
        jax.ShapeDtypeStruct((bsz, CONV_W - 1, D), _F32),
        jax.ShapeDtypeStruct((bsz, HEADS, DK, DV), _F32),
        jax.ShapeDtypeStruct((ns, D), _F32),
        jax.ShapeDtypeStruct(ss0.shape, _F32),
    )
    s_block = lambda i: (jnp.minimum(i, s_blocks - 1), 0, 0, 0)
    cur = lambda i: jnp.minimum(i, n_tiles - 1)
    prev = lambda i: jnp.maximum(i - 1, 0)
    nxt = lambda i: jnp.minimum(i + 1, n_tiles - 1)
    tile = lambda f: (lambda i: (f(i) // nt, lax.rem(f(i), nt), 0))
    seq_of = lambda f: (lambda i: (f(i) // nt, 0, 0))
    hbm = pl.BlockSpec(memory_space=pl.ANY)
    in_specs = [
        pl.BlockSpec((1, tm, D), tile(nxt)),
        pl.BlockSpec((1, tm, D), tile(prev)),
        pl.BlockSpec((1, 3, D), seq_of(nxt)),
        pl.BlockSpec((1, 3, D), seq_of(prev)),
        _const_spec((1, D)),
        hbm, hbm,
        _const_spec((CONV_W, D)),
        _const_spec((1, D)),
        _const_spec((LRU_BLOCKS, LRU_BW, LRU_BW)),
        _const_spec((LRU_BLOCKS, LRU_BW, LRU_BW)),
        _const_spec((1, D)),
        _const_spec((1, D)),
        _const_spec((1, D)),
        _const_spec((RANK, D_QK)),
        _const_spec((1, D_QK)),
        _const_spec((1, D)),
        _const_spec((1, D)),
        _const_spec((ns, D_QK)),
        _const_spec((ns, D_QK)),
        _const_spec((ns, D_QK)),
        _const_spec((ns, D)),
        pl.BlockSpec((s_per_step, HEADS, DK, DV), s_block),
        _const_spec((ns, D)),
        _const_spec((ns, D)),
        _const_spec((ns, D)),
        _const_spec((ns, D)),
    ]
    out_specs = (
        pl.BlockSpec((1, tm, D), tile(prev)),
        pl.BlockSpec((1, 1, D), seq_of(cur)),
        pl.BlockSpec((1, CONV_W - 1, D), seq_of(cur)),
        pl.BlockSpec((1, HEADS, DK, DV), lambda i: (cur(i) // nt, 0, 0, 0)),
        pl.BlockSpec((ns, D), lambda i: (0, 0)),
        pl.BlockSpec((s_per_step, HEADS, DK, DV), s_block),
    )
    scratch = [
        pltpu.VMEM((N_GROUPS, D, D), _BF),
        pltpu.VMEM((D_MIX, D), _BF),
        pltpu.VMEM((RANK_PAD, D), _BF),
        pltpu.VMEM((RANK_PAD, D_QK), _BF),
        pltpu.VMEM((LRU_BLOCKS, LRU_BW, 2 * LRU_BW), _BF),
        pltpu.VMEM((2, STAGE_ROWS, D), _F32),
        pltpu.SemaphoreType.DMA((2,)),
        pltpu.VMEM((tm, D), _BF),
        pltpu.VMEM((tm + SUBLANES, D), _F32),
        pltpu.VMEM((tm, D), _F32),
        pltpu.VMEM((tm, D), _F32),
        pltpu.VMEM((tm, 2 * D), _F32),
        pltpu.VMEM((tm, D_QK), _F32),
        pltpu.VMEM((tm, D_QK), _F32),
        pltpu.VMEM((tm, D), _F32),
        pltpu.VMEM((tm, D), _F32),
        pltpu.VMEM((tm, D_QK), _F32),
        pltpu.VMEM((2, tm, D_MIX), _BF),
        pltpu.VMEM((tm, D), _F32),
        pltpu.VMEM((SUBLANES, D), _F32),
        pltpu.VMEM((1, D), _F32),
        pltpu.VMEM((HEADS, DV, DK), _F32),
        pltpu.VMEM((ns, D), _F32),
    ]
    return pl.pallas_call(
        kern,
        out_shape=out_shape,
        grid=(n_tiles + 1,),
        in_specs=in_specs,
        out_specs=out_specs,
        scratch_shapes=scratch,
        compiler_params=pltpu.CompilerParams(
            dimension_semantics=("arbitrary",),
            vmem_limit_bytes=VMEM_LIMIT),
        name="prompt_layer",
    )(x, x, ada3, ada3, gnorm, w_in_t, w_out2d, convw, convb, wgx, wga, bgx, bga, lam,
      wg2, bg2, ggla, gfin, sqs, sk, seg, sv, ss0, sx, sgate, szg, sylru)


def _decode_in_kernel(c_ref, wada_ref, bada_ref, x_ref, h0_ref, c0_ref, c1_ref, c2_ref,
                      gnorm_ref, wt_ref, convw_ref, convb_ref, wgx_ref, wga_ref,
                      bgx_ref, bga_ref, lam_ref, wg2_ref, bg2_ref,
                      adap_ref, gate_ref, hnew_ref, xa_ref, ylru_ref, qs_ref, k_ref, eg_ref,
                      v_ref, zg_ref, *, bp):
    sc = _silu(c_ref[...]).astype(_BF)
    ada = [_dot(sc, wada_ref[:, j * D:(j + 1) * D].astype(_BF)) + bada_ref[:, j * D:(j + 1) * D]
           for j in range(3)]
    for j in range(3):
        for b in range(bp):
            adap_ref[b, j:j + 1, :] = ada[j][b:b + 1, :]
    shift, scale = ada[0][bp:, :], ada[1][bp:, :]
    gate_ref[...] = ada[2][bp:, :]

    x = x_ref[...]
    gain = gnorm_ref[...] * (1.0 + scale)
    hnb = (_rms(x) * gain + shift).astype(_BF)

    def proj(c0, width):
        return lax.dot_general(hnb, wt_ref[c0:c0 + width, :].astype(_BF), _NT,
                               preferred_element_type=_F32)

    xa = proj(0, D)
    za = proj(D, D)
    qs_ref[...] = proj(2 * D, D_QK) * (DK ** -0.5)
    k_ref[...] = proj(2 * D + D_QK, D_QK)
    v_ref[...] = proj(3 * D, D)
    zg_ref[...] = proj(O_GLR + RANK, D)
    pad_rows = lambda a: jnp.concatenate(
        [a, jnp.zeros((RANK_PAD - RANK, a.shape[1]), _F32)], axis=0).astype(_BF)
    glr = lax.dot_general(hnb, pad_rows(wt_ref[O_GLR:O_GLR + RANK, :]), _NT,
                          preferred_element_type=_F32)
    la = _log_sigmoid(_dot(glr.astype(_BF), pad_rows(wg2_ref[...])) + bg2_ref[...]) / TAU
    eg_ref[...] = jnp.exp(la)
    xa_ref[...] = xa

    xc = convb_ref[...]
    for j, cj in enumerate((c0_ref[...], c1_ref[...], c2_ref[...], xa)):
        xc = xc + cj * convw_ref[j:j + 1, :]
    xcb = xc.astype(_BF)
    logsig = LRU_C * _log_sigmoid(lam_ref[...])
    for n in range(LRU_BLOCKS):
        ln = slice(n * LRU_BW, (n + 1) * LRU_BW)
        pre = jnp.concatenate([_dot(xcb[:, ln], wgx_ref[n].astype(_BF)),
                               _dot(xcb[:, ln], wga_ref[n].astype(_BF))], axis=-1)
        a, u = _lru_block(xc[:, ln], pre, bgx_ref[:, ln], bga_ref[:, ln], logsig[:, ln])
        h = a * h0_ref[:, ln] + u
        hnew_ref[:, ln] = h
        ylru_ref[:, ln] = h * _silu(za[:, ln])


def _decode_in(c_all, w_ada, b_ada, x, h0, c0, c1, c2, gnorm, w_in_t, convw, convb, wgx, wga,
               bgx, bga, lam, wg2, bg2, bp):
    n = x.shape[0]
    assert bp % SUBLANES == 0, "decode rows must start on a sublane-tile boundary of the adaLN rows"
    f = lambda w: jax.ShapeDtypeStruct((n, w), _F32)
    return pl.pallas_call(
        functools.partial(_decode_in_kernel, bp=bp),
        out_shape=(jax.ShapeDtypeStruct((bp, 3, D), _F32), f(D),
                   f(D), f(D), f(D), f(D_QK), f(D_QK), f(D_QK), f(D), f(D)),
        compiler_params=pltpu.CompilerParams(vmem_limit_bytes=VMEM_LIMIT),
        name="decode_in",
    )(c_all, w_ada, b_ada, x, h0, c0, c1, c2, gnorm, w_in_t, convw, convb, wgx, wga, bgx, bga,
      lam, wg2, bg2)


def kernel(x_prompt, x_sample, state_lru_h, state_lru_conv, state_gla, c_prompt, c_sample,
           g_norm, w_ada, b_ada, w_in, conv_w, conv_b, w_gate_x, b_gate_x, w_gate_a, b_gate_a,
           lru_lambda, w_gla_g2, b_gla_g2, g_gla_norm, w_out, g_final):
    depth = g_norm.shape[0]
    assert depth == 1, "single-layer trunk"
    bp = x_prompt.shape[0]
    ns = x_sample.shape[0]

    w_in_t = jnp.swapaxes(w_in, 1, 2).reshape(w_in.shape[-1], D)
    w_out2d = w_out.reshape(D_MIX, D)
    wg2 = w_gla_g2.reshape(RANK, D_QK)
    wgx = w_gate_x.reshape(LRU_BLOCKS, LRU_BW, LRU_BW)
    wga = w_gate_a.reshape(LRU_BLOCKS, LRU_BW, LRU_BW)
    row = lambda a: a.reshape(1, -1)
    gnorm, convb = row(g_norm), row(conv_b)
    bgx, bga, lam = row(b_gate_x), row(b_gate_a), row(lru_lambda)
    bg2, ggla, gfin = row(b_gla_g2), row(g_gla_norm), row(g_final)
    convw = conv_w.reshape(CONV_W, D)

    xs = x_sample.reshape(ns, D)
    conv0 = state_lru_conv.reshape(ns, CONV_W - 1, D)
    ada_p, gate_s, hs, xa, ylru, qs, ks, eg, vs, zg = _decode_in(
        jnp.concatenate([c_prompt, c_sample], axis=0), w_ada.reshape(D, 3 * D), row(b_ada),
        xs, state_lru_h.reshape(ns, D), conv0[:, 0], conv0[:, 1], conv0[:, 2],
        gnorm, w_in_t, convw, convb, wgx, wga, bgx, bga, lam, wg2, bg2, bp)

    yp, hp, cp, sp, ys, ss = _prompt(x_prompt, ada_p, gnorm, w_in_t, w_out2d, convw,
                                     convb, wgx, wga, bgx, bga, lam, wg2, bg2, ggla, gfin,
                                     qs, ks, eg, vs, state_gla.reshape(ns, HEADS, DK, DV),
                                     xs, gate_s, zg, ylru)
    cs = jnp.stack([conv0[:, 1], conv0[:, 2], xa], axis=1)

    return (yp, ys.reshape(ns, 1, D), hp.reshape(1, bp, D), cp[None], sp[None],
            hs[None], cs[None], ss[None])
```

```python
import functools

import jax
import jax.numpy as jnp
from jax import lax
from jax.experimental import pallas as pl
from jax.experimental.pallas import tpu as pltpu

EPS = 1e-6
D = 1024
LRU_BLOCKS = 8
LRU_BW = D // LRU_BLOCKS
LRU_C = 8.0
CONV_W = 4
HEADS = 4
DV = D // HEADS
DK = DV // 2
RANK = 16
RANK_PAD = 128
TAU = 16.0
CHUNK = 64
D_QK = HEADS * DK
D_MIX = 2 * D
O_GLR = 4 * D
N_GROUPS = 5

SUBLANES = 8
TM = 256
STAGE_ROWS = 256
VMEM_LIMIT = 56 * 1024 * 1024
NEG_LOG2E = -1.4426950408889634

_BF = jnp.bfloat16
_F32 = jnp.float32

_NT = (((1,), (1,)), ((), ()))


def _sigmoid(x):
    return 1.0 / (1.0 + jnp.exp2(x * NEG_LOG2E))


def _silu(x):
    return x * _sigmoid(x)


def _log_sigmoid(x):
    return jnp.minimum(x, 0.0) - jnp.log(1.0 + jnp.exp2(jnp.abs(x) * NEG_LOG2E))


def _rms(x):
    return x * lax.rsqrt(jnp.mean(x * x, axis=-1, keepdims=True) + EPS)


def _dot(a, b):
    return jnp.dot(a, b, preferred_element_type=_F32)


def _group_scan(a3, u3):
    row = lax.broadcasted_iota(jnp.int32, a3.shape, 1)
    for s in (1, 2, 4):
        a_sh = pltpu.roll(a3, s, axis=1)
        u_sh = pltpu.roll(u3, s, axis=1)
        m = row >= s
        u3 = jnp.where(m, a3 * u_sh + u3, u3)
        a3 = jnp.where(m, a3 * a_sh, a3)
    return a3, u3


def _group_cumsum(x3):
    row = lax.broadcasted_iota(jnp.int32, x3.shape, 1)
    for s in (1, 2, 4):
        x3 = jnp.where(row >= s, x3 + pltpu.roll(x3, s, axis=1), x3)
    return x3


def _linear_scan(a, u, h0):
    r, l = a.shape
    g = r // SUBLANES
    a3, u3 = _group_scan(a.reshape(g, SUBLANES, l), u.reshape(g, SUBLANES, l))
    carry = h0
    out = []
    for i in range(g):
        hg = a3[i] * carry + u3[i]
        out.append(hg)
        carry = hg[SUBLANES - 1:SUBLANES]
    return jnp.concatenate(out, axis=0)


def _chunk_cumsum(x):
    r, l = x.shape
    g = r // SUBLANES
    per_chunk = CHUNK // SUBLANES
    x3 = _group_cumsum(x.reshape(g, SUBLANES, l))
    out = []
    carry = None
    for i in range(g):
        xg = x3[i] if i % per_chunk == 0 else x3[i] + carry
        out.append(xg)
        carry = xg[SUBLANES - 1:SUBLANES]
    return jnp.concatenate(out, axis=0)


def _lru_block(xc_n, pre, bgx, bga, c_logsig):
    gx = _sigmoid(pre[:, :LRU_BW] + bgx)
    ga = _sigmoid(pre[:, LRU_BW:] + bga)
    log_a = ga * c_logsig
    a = jnp.exp(log_a)
    om = 1.0 - a * a
    mult = jnp.where(om > 0.0, om * lax.rsqrt(om), 0.0)
    u = mult * (gx * xc_n)
    return a, u


def _gla_token_step(qs, k, eg, v, s0):
    eye = (lax.broadcasted_iota(jnp.int32, (DK, DK), 0)
           == lax.broadcasted_iota(jnp.int32, (DK, DK), 1))

    def to_col(row):
        return jnp.sum(jnp.where(eye, jnp.broadcast_to(row, (DK, DK)), 0.0),
                       axis=-1, keepdims=True)

    qk = jnp.sum(qs * k, axis=-1, keepdims=True)
    qg = jnp.broadcast_to(qs * eg, (SUBLANES, DK)).astype(_BF)
    o = _dot(qg, s0.astype(_BF))[0:1] + qk * v
    return o, to_col(eg) * s0 + to_col(k) * v


def _stage_blocks(n, src_block, store_block, stage, sem):
    def copy(i, slot):
        return pltpu.make_async_copy(src_block(i), stage.at[slot], sem.at[slot])

    copy(0, 0).start()

    def body(i, carry):
        slot = lax.rem(i, 2)

        @pl.when(i + 1 < n)
        def _():
            copy(i + 1, 1 - slot).start()

        copy(i, slot).wait()
        store_block(i, stage[slot].astype(_BF))
        return carry

    lax.fori_loop(0, n, body, 0)


def _stage_in_proj(wt_hbm, wmain_s, stage, sem):
    per_group = D // STAGE_ROWS

    def block(g, c):
        row0 = g * D + (g // (N_GROUPS - 1)) * RANK + c * STAGE_ROWS
        return wt_hbm.at[pl.ds(pl.multiple_of(row0, SUBLANES), STAGE_ROWS), :]

    def copy(g, c):
        return pltpu.make_async_copy(block(g, c), stage.at[c % 2], sem.at[c % 2])

    copy(0, 0).start()

    def body(g, carry):
        for c in range(per_group):
            if c + 1 < per_group:
                copy(g, c + 1).start()
            else:
                @pl.when(g + 1 < N_GROUPS)
                def _():
                    copy(g + 1, 0).start()
            copy(g, c).wait()
            wmain_s[g, :, c * STAGE_ROWS:(c + 1) * STAGE_ROWS] = stage[c % 2].T.astype(_BF)
        return carry

    lax.fori_loop(0, N_GROUPS, body, 0)


def _prompt_kernel(xn_ref, xp_ref, c_ref, wada_hbm, bada_ref, gnorm_ref, wt_hbm, wout_hbm,
                   convw_ref, convb_ref, wgx_ref, wga_ref, bgx_ref, bga_ref, lam_ref, wg2_ref,
                   bg2_ref, ggla_ref, gfin_ref, ss0_ref, sx_ref, h0_ref, c0_ref, c1_ref, c2_ref,
                   y_ref, hlast_ref, convout_ref, sout_ref, ys_ref, ssnew_ref, hnew_ref, sxa_ref,
                   wmain_s, wout_s, wglr_s, wg2_s, wgate_s, stage, sem,
                   hn_s, xa_s, za_s, xc_s, pre_s, q_s, k_s, v_s, zg_s, b_s, mix_s, yacc_s, ctail_s,
                   h_s, st_s, so_s, ada_s, sqs_ref, sk_ref, seg_ref, sv_ref, sgate_ref, szg_ref,
                   sylru_ref, *, tm, nt, n_tiles, s_per_step, s_blocks):
    i = pl.program_id(0)
    bp = n_tiles // nt
    adan_ref = ada_s.at[pl.ds(jnp.minimum(i + 1, n_tiles - 1) // nt, 1)]
    adap_ref = ada_s.at[pl.ds(jnp.maximum(i - 1, 0) // nt, 1)]
    t = lax.rem(jnp.minimum(i, n_tiles - 1), nt)
    t_next = lax.rem(jnp.minimum(i + 1, n_tiles - 1), nt)
    slot = lax.rem(i, 2)
    half = D // 2
    quarter = D // 4

    def front_norm(x_ref, ada_ref):
        x = x_ref[0]
        gain = gnorm_ref[...] * (1.0 + ada_ref[0, 1:2, :])
        hn_s[...] = (_rms(x) * gain + ada_ref[0, 0:1, :]).astype(_BF)

    def front_xa():
        xa_s[SUBLANES:SUBLANES + tm, :] = _dot(hn_s[...], wmain_s[0])

    def front_za():
        za_s[...] = _dot(hn_s[...], wmain_s[1])

    def front_conv():
        xe = xa_s[...]
        xc = convb_ref[...]
        for j in range(CONV_W):
            lag = CONV_W - 1 - j
            xj = xe if lag == 0 else pltpu.roll(xe, lag, axis=0)
            xc = xc + xj[SUBLANES:, :] * convw_ref[j:j + 1, :]
        xc_s[...] = xc

    def front_gates():
        for n in range(LRU_BLOCKS):
            pre_s[:, 2 * n * LRU_BW:2 * (n + 1) * LRU_BW] = _dot(
                xc_s[:, n * LRU_BW:(n + 1) * LRU_BW].astype(_BF), wgate_s[n])

    def decode_front():
        sc = _silu(c_ref[...]).astype(_BF)
        blocks = [(j, kb) for j in range(3) for kb in range(D // STAGE_ROWS)]

        def ada_copy(n):
            j, kb = blocks[n]
            return pltpu.make_async_copy(
                wada_hbm.at[pl.ds(kb * STAGE_ROWS, STAGE_ROWS), pl.ds(j * D, D)],
                stage.at[n % 2], sem.at[n % 2])

        ada_copy(0).start()
        ada = []
        acc = None
        for n, (j, kb) in enumerate(blocks):
            if n + 1 < len(blocks):
                ada_copy(n + 1).start()
            ada_copy(n).wait()
            part = _dot(sc[:, kb * STAGE_ROWS:(kb + 1) * STAGE_ROWS], stage[n % 2].astype(_BF))
            acc = part if kb == 0 else acc + part
            if kb == D // STAGE_ROWS - 1:
                ada.append(acc + bada_ref[:, j * D:(j + 1) * D])
        for j in range(3):
            for b in range(bp):
                ada_s[b, j:j + 1, :] = ada[j][b:b + 1, :]
        shift, scale = ada[0][bp:, :], ada[1][bp:, :]
        sgate_ref[...] = ada[2][bp:, :]

        hnb = (_rms(sx_ref[...]) * (gnorm_ref[...] * (1.0 + scale)) + shift).astype(_BF)
        xa = _dot(hnb, wmain_s[0])
        za = _dot(hnb, wmain_s[1])
        qk = _dot(hnb, wmain_s[2])
        sqs_ref[...] = qk[:, :D_QK] * (DK ** -0.5)
        sk_ref[...] = qk[:, D_QK:]
        sv_ref[...] = _dot(hnb, wmain_s[3])
        szg_ref[...] = _dot(hnb, wmain_s[4])
        glr = lax.dot_general(hnb, wglr_s[...], _NT, preferred_element_type=_F32)
        la = _log_sigmoid(_dot(glr.astype(_BF), wg2_s[...]) + bg2_ref[...]) / TAU
        seg_ref[...] = jnp.exp(la)
        sxa_ref[...] = xa

        xc = convb_ref[...]
        for j, cj in enumerate((c0_ref[...], c1_ref[...], c2_ref[...], xa)):
            xc = xc + cj * convw_ref[j:j + 1, :]
        xcb = xc.astype(_BF)
        logsig = LRU_C * _log_sigmoid(lam_ref[...])
        for n in range(LRU_BLOCKS):
            ln = slice(n * LRU_BW, (n + 1) * LRU_BW)
            a, u = _lru_block(xc[:, ln], _dot(xcb[:, ln], wgate_s[n]), bgx_ref[:, ln],
                              bga_ref[:, ln], logsig[:, ln])
            h = a * h0_ref[:, ln] + u
            hnew_ref[:, ln] = h
            sylru_ref[:, ln] = h * _silu(za[:, ln])

    @pl.when(i == 0)
    def _():
        def out_rows(j):
            return pl.ds(pl.multiple_of(j * STAGE_ROWS, STAGE_ROWS), STAGE_ROWS)

        def store_out(j, val):
            wout_s[out_rows(j), :] = val

        _stage_in_proj(wt_hbm, wmain_s, stage, sem)
        _stage_blocks(D_MIX // STAGE_ROWS, lambda j: wout_hbm.at[out_rows(j), :], store_out,
                      stage, sem)
        glr_rows = pltpu.make_async_copy(wt_hbm.at[pl.ds(O_GLR, RANK), :],
                                         stage.at[0, pl.ds(0, RANK), :], sem.at[0])
        glr_rows.start()
        wglr_s[...] = jnp.zeros_like(wglr_s)
        wg2_s[...] = jnp.zeros_like(wg2_s)
        wg2_s[0:RANK, :] = wg2_ref[...].astype(_BF)
        wgate_s[:, :, 0:LRU_BW] = wgx_ref[...].astype(_BF)
        wgate_s[:, :, LRU_BW:2 * LRU_BW] = wga_ref[...].astype(_BF)
        glr_rows.wait()
        wglr_s[0:RANK, :] = stage[0, 0:RANK, :].astype(_BF)
        decode_front()
        so_s[...] = jnp.zeros_like(so_s)
        mix_s[...] = jnp.zeros_like(mix_s)
        xa_s[0:SUBLANES, :] = jnp.zeros((SUBLANES, D), _F32)
        front_norm(xp_ref, adap_ref)
        front_xa()
        front_za()
        front_conv()
        front_gates()

    @pl.when(t == 0)
    def _():
        h_s[...] = jnp.zeros_like(h_s)
        st_s[...] = jnp.zeros_like(st_s)

    def out_slab(j):
        def run():
            cols = slice(j * quarter, (j + 1) * quarter)
            o = _dot(mix_s[1 - slot], wout_s[:, cols])
            yacc_s[:, cols] = xp_ref[0, :, cols] + adap_ref[0, 2:3, cols] * o
        return run

    def out_norm():
        y_ref[0] = _rms(yacc_s[...]) * gfin_ref[...]

    @pl.when(i < n_tiles)
    def _():
        first = i * s_per_step
        grp = pl.ds(pl.multiple_of((first // SUBLANES) * SUBLANES, SUBLANES), SUBLANES)
        r0 = lax.rem(first, SUBLANES)
        sub = lax.broadcasted_iota(jnp.int32, (SUBLANES, 1), 0)
        g_qs, g_k, g_eg, g_v = sqs_ref[grp, :], sk_ref[grp, :], seg_ref[grp, :], sv_ref[grp, :]
        g_o = so_s[grp, :]
        for j in range(s_per_step):
            mine = sub == r0 + j
            pick = lambda a: jnp.sum(jnp.where(mine, a, 0.0), axis=0, keepdims=True)
            qs_j, k_j, eg_j, v_j = pick(g_qs), pick(g_k), pick(g_eg), pick(g_v)
            o_heads = []
            for h in range(HEADS):
                lk = slice(h * DK, (h + 1) * DK)
                lv = slice(h * DV, (h + 1) * DV)
                o, snew = _gla_token_step(qs_j[:, lk], k_j[:, lk], eg_j[:, lk], v_j[:, lv],
                                          ss0_ref[j, h])
                o_heads.append(o)
                ssnew_ref[j, h] = snew
            g_o = jnp.where(mine, jnp.concatenate(o_heads, axis=-1), g_o)
        so_s[grp, :] = g_o

        def proj(dst, dst_cols, group, w_cols):
            def run():
                dst[:, dst_cols] = _dot(hn_s[...], wmain_s[group, :, w_cols])
            return run

        def decay():
            glr = lax.dot_general(hn_s[...], wglr_s[...], _NT,
                                  preferred_element_type=_F32)
            la = _log_sigmoid(_dot(glr.astype(_BF), wg2_s[...]) + bg2_ref[...]) / TAU
            b_s[...] = _chunk_cumsum(la)

        lo, hi, full = slice(0, half), slice(half, D), slice(0, D_QK)
        tasks = [
            (out_slab(0), proj(q_s, full, 2, lo)),
            (out_slab(1), proj(k_s, full, 2, hi)),
            (out_slab(2), decay),
            (out_slab(3), out_norm),
            (proj(v_s, lo, 3, lo),),
            (proj(v_s, hi, 3, hi),),
            (proj(zg_s, lo, 4, lo),),
            (proj(zg_s, hi, 4, hi),),
        ]

        logsig = LRU_C * _log_sigmoid(lam_ref[...])
        for n in range(LRU_BLOCKS):
            ln = slice(n * LRU_BW, (n + 1) * LRU_BW)
            a, u = _lru_block(xc_s[:, ln], pre_s[:, 2 * n * LRU_BW:2 * (n + 1) * LRU_BW],
                              bgx_ref[:, ln], bga_ref[:, ln], logsig[:, ln])
            h = _linear_scan(a, u, h_s[:, ln])
            h_s[:, ln] = h[tm - 1:tm]
            mix_s[slot, :, ln] = (h * _silu(za_s[:, ln])).astype(_BF)
            for task in tasks[n]:
                task()

        tril = (lax.broadcasted_iota(jnp.int32, (CHUNK, CHUNK), 0)
                >= lax.broadcasted_iota(jnp.int32, (CHUNK, CHUNK), 1))

        def gla_chunk(c):
            rows = slice(c * CHUNK, (c + 1) * CHUNK)
            for h in range(HEADS):
                lk = slice(h * DK, (h + 1) * DK)
                lv = slice(h * DV, (h + 1) * DV)
                b = b_s[rows, lk]
                bl = b[CHUNK - 1:CHUNK]
                q = q_s[rows, lk] * (DK ** -0.5)
                k = k_s[rows, lk]
                v = v_s[rows, lv]
                qi = (q * jnp.exp(b)).astype(_BF)
                ki = (k * jnp.exp(-b)).astype(_BF)
                kd = (k * jnp.exp(bl - b)).astype(_BF)
                vb = v.astype(_BF)
                att = lax.dot_general(qi, ki, _NT, preferred_element_type=_F32)
                att = jnp.where(tril, att, 0.0).astype(_BF)
                st = st_s[h]
                o = _dot(att, vb) + lax.dot_general(qi, st.astype(_BF), _NT,
                                                    preferred_element_type=_F32)
                st_s[h] = st * jnp.exp(bl) + _dot(v.T.astype(_BF), kd)
                o = _rms(o) * ggla_ref[:, lv]
                mix_s[slot, rows, D + h * DV:D + (h + 1) * DV] = (
                    o * _silu(zg_s[rows, lv])).astype(_BF)

        def next_tail():
            tail = xa_s[tm:tm + SUBLANES, :]
            ctail_s[...] = tail
            xa_s[0:SUBLANES, :] = jnp.where(t_next != 0, tail, 0.0)

        next_tail()
        front_norm(xn_ref, adan_ref)
        gla_chunk(0)
        front_xa()
        gla_chunk(1)
        front_za()
        front_conv()
        gla_chunk(2)
        front_gates()
        gla_chunk(3)

    @pl.when(i == n_tiles)
    def _():
        for j in range(D // quarter):
            out_slab(j)()
        out_norm()
        parts = [sylru_ref[...].astype(_BF)]
        for h in range(HEADS):
            lv = slice(h * DV, (h + 1) * DV)
            o = _rms(so_s[:, lv]) * ggla_ref[:, lv]
            parts.append((o * _silu(szg_ref[:, lv])).astype(_BF))
        ys = sx_ref[...] + sgate_ref[...] * _dot(jnp.concatenate(parts, axis=-1), wout_s[...])
        ys_ref[...] = _rms(ys) * gfin_ref[...]

    @pl.when((t == nt - 1) & (i < n_tiles))
    def _():
        hlast_ref[0] = h_s[...]
        convout_ref[0] = ctail_s[SUBLANES - (CONV_W - 1):SUBLANES, :]
        for h in range(HEADS):
            sout_ref[0, h] = st_s[h].T


def _const_spec(shape):
    nd = len(shape)
    return pl.BlockSpec(shape, lambda i: (0,) * nd, pipeline_mode=pl.Buffered(1))


def _prompt(x, c_all, w_ada, b_ada, gnorm, w_in_t, w_out2d, convw, convb, wgx, wga, bgx, bga,
            lam, wg2, bg2, ggla, gfin, ss0, sx, h0, c0, c1, c2, tm=TM):
    bsz, seq, _ = x.shape
    assert bsz % SUBLANES == 0, "decode rows must start on a sublane-tile boundary of the adaLN rows"
    nt = seq // tm
    n_tiles = bsz * nt
    ns = ss0.shape[0]
    s_per_step = -(-ns // n_tiles)
    s_blocks = ns // s_per_step
    assert tm // CHUNK == 4, "the step body spreads the next tile's input side over 4 GLA chunks"
    assert s_blocks * s_per_step == ns and s_blocks <= n_tiles and SUBLANES % s_per_step == 0
    kern = functools.partial(_prompt_kernel, tm=tm, nt=nt, n_tiles=n_tiles,
                             s_per_step=s_per_step, s_blocks=s_blocks)
    out_shape = (
        jax.ShapeDtypeStruct((bsz, seq, D), _F32),
        jax.ShapeDtypeStruct((bsz, 1, D), _F32),
        jax.ShapeDtypeStruct((bsz, CONV_W - 1, D), _F32),
        jax.ShapeDtypeStruct((bsz, HEADS, DK, DV), _F32),
        jax.ShapeDtypeStruct((ns, D), _F32),
        jax.ShapeDtypeStruct(ss0.shape, _F32),
        jax.ShapeDtypeStruct((ns, D), _F32),
        jax.ShapeDtypeStruct((ns, D), _F32),
    )
    s_block = lambda i: (jnp.minimum(i, s_blocks - 1), 0, 0, 0)
    cur = lambda i: jnp.minimum(i, n_tiles - 1)
    prev = lambda i: jnp.maximum(i - 1, 0)
    nxt = lambda i: jnp.minimum(i + 1, n_tiles - 1)
    tile = lambda f: (lambda i: (f(i) // nt, lax.rem(f(i), nt), 0))
    seq_of = lambda f: (lambda i: (f(i) // nt, 0, 0))
    hbm = pl.BlockSpec(memory_space=pl.ANY)
    in_specs = [
        pl.BlockSpec((1, tm, D), tile(nxt)),
        pl.BlockSpec((1, tm, D), tile(prev)),
        _const_spec((bsz + ns, D)),
        hbm,
        _const_spec((1, 3 * D)),
        _const_spec((1, D)),
        hbm, hbm,
        _const_spec((CONV_W, D)),
        _const_spec((1, D)),
        _const_spec((LRU_BLOCKS, LRU_BW, LRU_BW)),
        _const_spec((LRU_BLOCKS, LRU_BW, LRU_BW)),
        _const_spec((1, D)),
        _const_spec((1, D)),
        _const_spec((1, D)),
        _const_spec((RANK, D_QK)),
        _const_spec((1, D_QK)),
        _const_spec((1, D)),
        _const_spec((1, D)),
        pl.BlockSpec((s_per_step, HEADS, DK, DV), s_block),
        _const_spec((ns, D)),
        _const_spec((ns, D)),
        _const_spec((ns, D)),
        _const_spec((ns, D)),
        _const_spec((ns, D)),
    ]
    out_specs = (
        pl.BlockSpec((1, tm, D), tile(prev)),
        pl.BlockSpec((1, 1, D), seq_of(cur)),
        pl.BlockSpec((1, CONV_W - 1, D), seq_of(cur)),
        pl.BlockSpec((1, HEADS, DK, DV), lambda i: (cur(i) // nt, 0, 0, 0)),
        pl.BlockSpec((ns, D), lambda i: (0, 0)),
        pl.BlockSpec((s_per_step, HEADS, DK, DV), s_block),
        pl.BlockSpec((ns, D), lambda i: (0, 0)),
        pl.BlockSpec((ns, D), lambda i: (0, 0)),
    )
    scratch = [
        pltpu.VMEM((N_GROUPS, D, D), _BF),
        pltpu.VMEM((D_MIX, D), _BF),
        pltpu.VMEM((RANK_PAD, D), _BF),
        pltpu.VMEM((RANK_PAD, D_QK), _BF),
        pltpu.VMEM((LRU_BLOCKS, LRU_BW, 2 * LRU_BW), _BF),
        pltpu.VMEM((2, STAGE_ROWS, D), _F32),
        pltpu.SemaphoreType.DMA((2,)),
        pltpu.VMEM((tm, D), _BF),
        pltpu.VMEM((tm + SUBLANES, D), _F32),
        pltpu.VMEM((tm, D), _F32),
        pltpu.VMEM((tm, D), _F32),
        pltpu.VMEM((tm, 2 * D), _F32),
        pltpu.VMEM((tm, D_QK), _F32),
        pltpu.VMEM((tm, D_QK), _F32),
        pltpu.VMEM((tm, D), _F32),
        pltpu.VMEM((tm, D), _F32),
        pltpu.VMEM((tm, D_QK), _F32),
        pltpu.VMEM((2, tm, D_MIX), _BF),
        pltpu.VMEM((tm, D), _F32),
        pltpu.VMEM((SUBLANES, D), _F32),
        pltpu.VMEM((1, D), _F32),
        pltpu.VMEM((HEADS, DV, DK), _F32),
        pltpu.VMEM((ns, D), _F32),
        pltpu.VMEM((bsz, SUBLANES, D), _F32),
        pltpu.VMEM((ns, D_QK), _F32),
        pltpu.VMEM((ns, D_QK), _F32),
        pltpu.VMEM((ns, D_QK), _F32),
        pltpu.VMEM((ns, D), _F32),
        pltpu.VMEM((ns, D), _F32),
        pltpu.VMEM((ns, D), _F32),
        pltpu.VMEM((ns, D), _F32),
    ]
    return pl.pallas_call(
        kern,
        out_shape=out_shape,
        grid=(n_tiles + 1,),
        in_specs=in_specs,
        out_specs=out_specs,
        scratch_shapes=scratch,
        compiler_params=pltpu.CompilerParams(
            dimension_semantics=("arbitrary",),
            vmem_limit_bytes=VMEM_LIMIT),
        name="prompt_layer",
    )(x, x, c_all, w_ada, b_ada, gnorm, w_in_t, w_out2d, convw, convb, wgx, wga, bgx, bga, lam,
      wg2, bg2, ggla, gfin, ss0, sx, h0, c0, c1, c2)


def kernel(x_prompt, x_sample, state_lru_h, state_lru_conv, state_gla, c_prompt, c_sample,
           g_norm, w_ada, b_ada, w_in, conv_w, conv_b, w_gate_x, b_gate_x, w_gate_a, b_gate_a,
           lru_lambda, w_gla_g2, b_gla_g2, g_gla_norm, w_out, g_final):
    depth = g_norm.shape[0]
    assert depth == 1, "single-layer trunk"
    bp = x_prompt.shape[0]
    ns = x_sample.shape[0]

    w_in_t = jnp.swapaxes(w_in, 1, 2).reshape(w_in.shape[-1], D)
    w_out2d = w_out.reshape(D_MIX, D)
    wg2 = w_gla_g2.reshape(RANK, D_QK)
    wgx = w_gate_x.reshape(LRU_BLOCKS, LRU_BW, LRU_BW)
    wga = w_gate_a.reshape(LRU_BLOCKS, LRU_BW, LRU_BW)
    row = lambda a: a.reshape(1, -1)
    gnorm, convb = row(g_norm), row(conv_b)
    bgx, bga, lam = row(b_gate_x), row(b_gate_a), row(lru_lambda)
    bg2, ggla, gfin = row(b_gla_g2), row(g_gla_norm), row(g_final)
    convw = conv_w.reshape(CONV_W, D)

    xs = x_sample.reshape(ns, D)
    conv0 = state_lru_conv.reshape(ns, CONV_W - 1, D)
    yp, hp, cp, sp, ys, ss, hs, xa = _prompt(
        x_prompt, jnp.concatenate([c_prompt, c_sample], axis=0), w_ada.reshape(D, 3 * D),
        row(b_ada), gnorm, w_in_t, w_out2d, convw, convb, wgx, wga, bgx, bga, lam, wg2, bg2,
        ggla, gfin, state_gla.reshape(ns, HEADS, DK, DV), xs, state_lru_h.reshape(ns, D),
        conv0[:, 0], conv0[:, 1], conv0[:, 2])
    cs = jnp.stack([conv0[:, 1], conv0[:, 2], xa], axis=1)

    return (yp, ys.reshape(ns, 1, D), hp.reshape(1, bp, D), cp[None], sp[None],
            hs[None], cs[None], ss[None])
```

```python
import functools

import jax
import jax.numpy as jnp
from jax import lax
from jax.experimental import pallas as pl
from jax.experimental.pallas import tpu as pltpu

EPS = 1e-6
D = 1024
LRU_BLOCKS = 8
LRU_BW = D // LRU_BLOCKS
LRU_C = 8.0
CONV_W = 4
HEADS = 4
DV = D // HEADS
DK = DV // 2
RANK = 16
RANK_PAD = 128
TAU = 16.0
CHUNK = 64
D_QK = HEADS * DK
D_MIX = 2 * D
O_GLR = 4 * D
N_GROUPS = 5

SUBLANES = 8
TM = 256
STAGE_ROWS = 256
VMEM_LIMIT = 56 * 1024 * 1024
NEG_LOG2E = -1.4426950408889634

_BF = jnp.bfloat16
_F32 = jnp.float32

_NT = (((1,), (1,)), ((), ()))


def _sigmoid(x):
    return 1.0 / (1.0 + jnp.exp2(x * NEG_LOG2E))


def _silu(x):
    return x * _sigmoid(x)


def _log_sigmoid(x):
    return jnp.minimum(x, 0.0) - jnp.log(1.0 + jnp.exp2(jnp.abs(x) * NEG_LOG2E))


def _rms(x):
    return x * lax.rsqrt(jnp.mean(x * x, axis=-1, keepdims=True) + EPS)


def _dot(a, b):
    return jnp.dot(a, b, preferred_element_type=_F32)


def _group_scan(a3, u3):
    row = lax.broadcasted_iota(jnp.int32, a3.shape, 1)
    for s in (1, 2, 4):
        a_sh = pltpu.roll(a3, s, axis=1)
        u_sh = pltpu.roll(u3, s, axis=1)
        m = row >= s
        u3 = jnp.where(m, a3 * u_sh + u3, u3)
        a3 = jnp.where(m, a3 * a_sh, a3)
    return a3, u3


def _group_cumsum(x3):
    row = lax.broadcasted_iota(jnp.int32, x3.shape, 1)
    for s in (1, 2, 4):
        x3 = jnp.where(row >= s, x3 + pltpu.roll(x3, s, axis=1), x3)
    return x3


def _linear_scan(a, u, h0):
    r, l = a.shape
    g = r // SUBLANES
    a3, u3 = _group_scan(a.reshape(g, SUBLANES, l), u.reshape(g, SUBLANES, l))
    carry = h0
    out = []
    for i in range(g):
        hg = a3[i] * carry + u3[i]
        out.append(hg)
        carry = hg[SUBLANES - 1:SUBLANES]
    return jnp.concatenate(out, axis=0)


def _chunk_cumsum(x):
    r, l = x.shape
    g = r // SUBLANES
    per_chunk = CHUNK // SUBLANES
    x3 = _group_cumsum(x.reshape(g, SUBLANES, l))
    out = []
    carry = None
    for i in range(g):
        xg = x3[i] if i % per_chunk == 0 else x3[i] + carry
        out.append(xg)
        carry = xg[SUBLANES - 1:SUBLANES]
    return jnp.concatenate(out, axis=0)


def _lru_block(xc_n, pre, bgx, bga, c_logsig):
    gx = _sigmoid(pre[:, :LRU_BW] + bgx)
    ga = _sigmoid(pre[:, LRU_BW:] + bga)
    log_a = ga * c_logsig
    a = jnp.exp(log_a)
    om = 1.0 - a * a
    mult = jnp.where(om > 0.0, om * lax.rsqrt(om), 0.0)
    u = mult * (gx * xc_n)
    return a, u


def _gla_token_step(qs, k, eg, v, s0):
    eye = (lax.broadcasted_iota(jnp.int32, (DK, DK), 0)
           == lax.broadcasted_iota(jnp.int32, (DK, DK), 1))

    def to_col(row):
        return jnp.sum(jnp.where(eye, jnp.broadcast_to(row, (DK, DK)), 0.0),
                       axis=-1, keepdims=True)

    qk = jnp.sum(qs * k, axis=-1, keepdims=True)
    qg = jnp.broadcast_to(qs * eg, (SUBLANES, DK)).astype(_BF)
    o = _dot(qg, s0.astype(_BF))[0:1] + qk * v
    return o, to_col(eg) * s0 + to_col(k) * v


def _stage_blocks(n, src_block, store_block, stage, sem):
    def copy(i, slot):
        return pltpu.make_async_copy(src_block(i), stage.at[slot], sem.at[slot])

    copy(0, 0).start()

    def body(i, carry):
        slot = lax.rem(i, 2)

        @pl.when(i + 1 < n)
        def _():
            copy(i + 1, 1 - slot).start()

        copy(i, slot).wait()
        store_block(i, stage[slot].astype(_BF))
        return carry

    lax.fori_loop(0, n, body, 0)


def _stage_in_proj(wt_hbm, wmain_s, stage, sem):
    per_group = D // STAGE_ROWS

    def block(g, c):
        row0 = g * D + (g // (N_GROUPS - 1)) * RANK + c * STAGE_ROWS
        return wt_hbm.at[pl.ds(pl.multiple_of(row0, SUBLANES), STAGE_ROWS), :]

    def copy(g, c):
        return pltpu.make_async_copy(block(g, c), stage.at[c % 2], sem.at[c % 2])

    copy(0, 0).start()

    def body(g, carry):
        for c in range(per_group):
            if c + 1 < per_group:
                copy(g, c + 1).start()
            else:
                @pl.when(g + 1 < N_GROUPS)
                def _():
                    copy(g + 1, 0).start()
            copy(g, c).wait()
            wmain_s[g, :, c * STAGE_ROWS:(c + 1) * STAGE_ROWS] = stage[c % 2].T.astype(_BF)
        return carry

    lax.fori_loop(0, N_GROUPS, body, 0)


def _prompt_kernel(xn_ref, xp_ref, cp_ref, cs_ref, wada_hbm, bada_ref, gnorm_ref, wt_hbm, wout_hbm,
                   convw_ref, convb_ref, wgx_ref, wga_ref, bgx_ref, bga_ref, lam_ref, wg2_ref,
                   bg2_ref, ggla_ref, gfin_ref, ss0_ref, sx_ref, h0_ref, conv0_ref,
                   y_ref, hlast_ref, convout_ref, sout_ref, ys_ref, ssnew_ref, hnew_ref, sxa_ref,
                   wmain_s, wout_s, wglr_s, wg2_s, wgate_s, stage, sem,
                   hn_s, xa_s, za_s, xc_s, pre_s, q_s, k_s, v_s, zg_s, b_s, mix_s, yacc_s, ctail_s,
                   h_s, st_s, so_s, ada_s, sqs_ref, sk_ref, seg_ref, sv_ref, sgate_ref, szg_ref,
                   sylru_ref, *, tm, nt, n_tiles, s_per_step, s_blocks):
    i = pl.program_id(0)
    bp = n_tiles // nt
    adan_ref = ada_s.at[pl.ds(jnp.minimum(i + 1, n_tiles - 1) // nt, 1)]
    adap_ref = ada_s.at[pl.ds(jnp.maximum(i - 1, 0) // nt, 1)]
    t = lax.rem(jnp.minimum(i, n_tiles - 1), nt)
    t_next = lax.rem(jnp.minimum(i + 1, n_tiles - 1), nt)
    slot = lax.rem(i, 2)
    half = D // 2
    quarter = D // 4

    def front_norm(x_ref, ada_ref):
        x = x_ref[0]
        gain = gnorm_ref[...] * (1.0 + ada_ref[0, 1:2, :])
        hn_s[...] = (_rms(x) * gain + ada_ref[0, 0:1, :]).astype(_BF)

    def front_xa():
        xa_s[SUBLANES:SUBLANES + tm, :] = _dot(hn_s[...], wmain_s[0])

    def front_za():
        za_s[...] = _dot(hn_s[...], wmain_s[1])

    def front_conv():
        xe = xa_s[...]
        xc = convb_ref[...]
        for j in range(CONV_W):
            lag = CONV_W - 1 - j
            xj = xe if lag == 0 else pltpu.roll(xe, lag, axis=0)
            xc = xc + xj[SUBLANES:, :] * convw_ref[j:j + 1, :]
        xc_s[...] = xc

    def front_gates():
        for n in range(LRU_BLOCKS):
            pre_s[:, 2 * n * LRU_BW:2 * (n + 1) * LRU_BW] = _dot(
                xc_s[:, n * LRU_BW:(n + 1) * LRU_BW].astype(_BF), wgate_s[n])

    def decode_front():
        sc = jnp.concatenate([_silu(cp_ref[...]), _silu(cs_ref[...])], axis=0).astype(_BF)
        blocks = [(j, kb) for j in range(3) for kb in range(D // STAGE_ROWS)]

        def ada_copy(n):
            j, kb = blocks[n]
            return pltpu.make_async_copy(
                wada_hbm.at[pl.ds(kb * STAGE_ROWS, STAGE_ROWS), pl.ds(j * D, D)],
                stage.at[n % 2], sem.at[n % 2])

        ada_copy(0).start()
        ada = []
        acc = None
        for n, (j, kb) in enumerate(blocks):
            if n + 1 < len(blocks):
                ada_copy(n + 1).start()
            ada_copy(n).wait()
            part = _dot(sc[:, kb * STAGE_ROWS:(kb + 1) * STAGE_ROWS], stage[n % 2].astype(_BF))
            acc = part if kb == 0 else acc + part
            if kb == D // STAGE_ROWS - 1:
                ada.append(acc + bada_ref[:, j * D:(j + 1) * D])
        for j in range(3):
            for b in range(bp):
                ada_s[b, j:j + 1, :] = ada[j][b:b + 1, :]
        shift, scale = ada[0][bp:, :], ada[1][bp:, :]
        sgate_ref[...] = ada[2][bp:, :]

        hnb = (_rms(sx_ref[...]) * (gnorm_ref[...] * (1.0 + scale)) + shift).astype(_BF)
        xa = _dot(hnb, wmain_s[0])
        za = _dot(hnb, wmain_s[1])
        qk = _dot(hnb, wmain_s[2])
        sqs_ref[...] = qk[:, :D_QK] * (DK ** -0.5)
        sk_ref[...] = qk[:, D_QK:]
        sv_ref[...] = _dot(hnb, wmain_s[3])
        szg_ref[...] = _dot(hnb, wmain_s[4])
        glr = lax.dot_general(hnb, wglr_s[...], _NT, preferred_element_type=_F32)
        la = _log_sigmoid(_dot(glr.astype(_BF), wg2_s[...]) + bg2_ref[...]) / TAU
        seg_ref[...] = jnp.exp(la)
        sxa_ref[...] = xa

        xc = convb_ref[...]
        for j in range(CONV_W):
            cj = conv0_ref[:, j * D:(j + 1) * D] if j < CONV_W - 1 else xa
            xc = xc + cj * convw_ref[j:j + 1, :]
        xcb = xc.astype(_BF)
        logsig = LRU_C * _log_sigmoid(lam_ref[...])
        for n in range(LRU_BLOCKS):
            ln = slice(n * LRU_BW, (n + 1) * LRU_BW)
            a, u = _lru_block(xc[:, ln], _dot(xcb[:, ln], wgate_s[n]), bgx_ref[:, ln],
                              bga_ref[:, ln], logsig[:, ln])
            h = a * h0_ref[:, ln] + u
            hnew_ref[:, ln] = h
            sylru_ref[:, ln] = h * _silu(za[:, ln])

    @pl.when(i == 0)
    def _():
        def out_rows(j):
            return pl.ds(pl.multiple_of(j * STAGE_ROWS, STAGE_ROWS), STAGE_ROWS)

        def store_out(j, val):
            wout_s[out_rows(j), :] = val

        _stage_in_proj(wt_hbm, wmain_s, stage, sem)
        _stage_blocks(D_MIX // STAGE_ROWS, lambda j: wout_hbm.at[out_rows(j), :], store_out,
                      stage, sem)
        glr_rows = pltpu.make_async_copy(wt_hbm.at[pl.ds(O_GLR, RANK), :],
                                         stage.at[0, pl.ds(0, RANK), :], sem.at[0])
        glr_rows.start()
        wglr_s[...] = jnp.zeros_like(wglr_s)
        wg2_s[...] = jnp.zeros_like(wg2_s)
        wg2_s[0:RANK, :] = wg2_ref[...].astype(_BF)
        wgate_s[:, :, 0:LRU_BW] = wgx_ref[...].astype(_BF)
        wgate_s[:, :, LRU_BW:2 * LRU_BW] = wga_ref[...].astype(_BF)
        glr_rows.wait()
        wglr_s[0:RANK, :] = stage[0, 0:RANK, :].astype(_BF)
        decode_front()
        so_s[...] = jnp.zeros_like(so_s)
        mix_s[...] = jnp.zeros_like(mix_s)
        xa_s[0:SUBLANES, :] = jnp.zeros((SUBLANES, D), _F32)
        front_norm(xp_ref, adap_ref)
        front_xa()
        front_za()
        front_conv()
        front_gates()

    @pl.when(t == 0)
    def _():
        h_s[...] = jnp.zeros_like(h_s)
        st_s[...] = jnp.zeros_like(st_s)

    def out_slab(j):
        def run():
            cols = slice(j * quarter, (j + 1) * quarter)
            o = _dot(mix_s[1 - slot], wout_s[:, cols])
            yacc_s[:, cols] = xp_ref[0, :, cols] + adap_ref[0, 2:3, cols] * o
        return run

    def out_norm():
        y_ref[0] = _rms(yacc_s[...]) * gfin_ref[...]

    @pl.when(i < n_tiles)
    def _():
        first = i * s_per_step
        grp = pl.ds(pl.multiple_of((first // SUBLANES) * SUBLANES, SUBLANES), SUBLANES)
        r0 = lax.rem(first, SUBLANES)
        sub = lax.broadcasted_iota(jnp.int32, (SUBLANES, 1), 0)
        g_qs, g_k, g_eg, g_v = sqs_ref[grp, :], sk_ref[grp, :], seg_ref[grp, :], sv_ref[grp, :]
        g_o = so_s[grp, :]
        for j in range(s_per_step):
            mine = sub == r0 + j
            pick = lambda a: jnp.sum(jnp.where(mine, a, 0.0), axis=0, keepdims=True)
            qs_j, k_j, eg_j, v_j = pick(g_qs), pick(g_k), pick(g_eg), pick(g_v)
            o_heads = []
            for h in range(HEADS):
                lk = slice(h * DK, (h + 1) * DK)
                lv = slice(h * DV, (h + 1) * DV)
                o, snew = _gla_token_step(qs_j[:, lk], k_j[:, lk], eg_j[:, lk], v_j[:, lv],
                                          ss0_ref[j, h])
                o_heads.append(o)
                ssnew_ref[j, h] = snew
            g_o = jnp.where(mine, jnp.concatenate(o_heads, axis=-1), g_o)
        so_s[grp, :] = g_o

        def proj(dst, dst_cols, group, w_cols):
            def run():
                dst[:, dst_cols] = _dot(hn_s[...], wmain_s[group, :, w_cols])
            return run

        def decay():
            glr = lax.dot_general(hn_s[...], wglr_s[...], _NT,
                                  preferred_element_type=_F32)
            la = _log_sigmoid(_dot(glr.astype(_BF), wg2_s[...]) + bg2_ref[...]) / TAU
            b_s[...] = _chunk_cumsum(la)

        lo, hi, full = slice(0, half), slice(half, D), slice(0, D_QK)
        tasks = [
            (out_slab(0), proj(q_s, full, 2, lo)),
            (out_slab(1), proj(k_s, full, 2, hi)),
            (out_slab(2), decay),
            (out_slab(3), out_norm),
            (proj(v_s, lo, 3, lo),),
            (proj(v_s, hi, 3, hi),),
            (proj(zg_s, lo, 4, lo),),
            (proj(zg_s, hi, 4, hi),),
        ]

        logsig = LRU_C * _log_sigmoid(lam_ref[...])
        for n in range(LRU_BLOCKS):
            ln = slice(n * LRU_BW, (n + 1) * LRU_BW)
            a, u = _lru_block(xc_s[:, ln], pre_s[:, 2 * n * LRU_BW:2 * (n + 1) * LRU_BW],
                              bgx_ref[:, ln], bga_ref[:, ln], logsig[:, ln])
            h = _linear_scan(a, u, h_s[:, ln])
            h_s[:, ln] = h[tm - 1:tm]
            mix_s[slot, :, ln] = (h * _silu(za_s[:, ln])).astype(_BF)
            for task in tasks[n]:
                task()

        tril = (lax.broadcasted_iota(jnp.int32, (CHUNK, CHUNK), 0)
                >= lax.broadcasted_iota(jnp.int32, (CHUNK, CHUNK), 1))

        def gla_chunk(c):
            rows = slice(c * CHUNK, (c + 1) * CHUNK)
            for h in range(HEADS):
                lk = slice(h * DK, (h + 1) * DK)
                lv = slice(h * DV, (h + 1) * DV)
                b = b_s[rows, lk]
                bl = b[CHUNK - 1:CHUNK]
                q = q_s[rows, lk] * (DK ** -0.5)
                k = k_s[rows, lk]
                v = v_s[rows, lv]
                qi = (q * jnp.exp(b)).astype(_BF)
                ki = (k * jnp.exp(-b)).astype(_BF)
                kd = (k * jnp.exp(bl - b)).astype(_BF)
                vb = v.astype(_BF)
                att = lax.dot_general(qi, ki, _NT, preferred_element_type=_F32)
                att = jnp.where(tril, att, 0.0).astype(_BF)
                st = st_s[h]
                o = _dot(att, vb) + lax.dot_general(qi, st.astype(_BF), _NT,
                                                    preferred_element_type=_F32)
                st_s[h] = st * jnp.exp(bl) + _dot(v.T.astype(_BF), kd)
                o = _rms(o) * ggla_ref[:, lv]
                mix_s[slot, rows, D + h * DV:D + (h + 1) * DV] = (
                    o * _silu(zg_s[rows, lv])).astype(_BF)

        def next_tail():
            tail = xa_s[tm:tm + SUBLANES, :]
            ctail_s[...] = tail
            xa_s[0:SUBLANES, :] = jnp.where(t_next != 0, tail, 0.0)

        next_tail()
        front_norm(xn_ref, adan_ref)
        gla_chunk(0)
        front_xa()
        gla_chunk(1)
        front_za()
        front_conv()
        gla_chunk(2)
        front_gates()
        gla_chunk(3)

    @pl.when(i == n_tiles)
    def _():
        for j in range(D // quarter):
            out_slab(j)()
        out_norm()
        parts = [sylru_ref[...].astype(_BF)]
        for h in range(HEADS):
            lv = slice(h * DV, (h + 1) * DV)
            o = _rms(so_s[:, lv]) * ggla_ref[:, lv]
            parts.append((o * _silu(szg_ref[:, lv])).astype(_BF))
        ys = sx_ref[...] + sgate_ref[...] * _dot(jnp.concatenate(parts, axis=-1), wout_s[...])
        ys_ref[...] = _rms(ys) * gfin_ref[...]

    @pl.when((t == nt - 1) & (i < n_tiles))
    def _():
        hlast_ref[0] = h_s[...]
        convout_ref[0] = ctail_s[SUBLANES - (CONV_W - 1):SUBLANES, :]
        for h in range(HEADS):
            sout_ref[0, h] = st_s[h].T


def _const_spec(shape):
    nd = len(shape)
    return pl.BlockSpec(shape, lambda i: (0,) * nd, pipeline_mode=pl.Buffered(1))


def _prompt(x, c_prompt, c_sample, w_ada, b_ada, gnorm, w_in_t, w_out2d, convw, convb, wgx, wga,
            bgx, bga, lam, wg2, bg2, ggla, gfin, ss0, sx, h0, conv0, tm=TM):
    bsz, seq, _ = x.shape
    assert bsz % SUBLANES == 0, "decode rows must start on a sublane-tile boundary of the adaLN rows"
    nt = seq // tm
    n_tiles = bsz * nt
    ns = ss0.shape[0]
    s_per_step = -(-ns // n_tiles)
    s_blocks = ns // s_per_step
    assert tm // CHUNK == 4, "the step body spreads the next tile's input side over 4 GLA chunks"
    assert s_blocks * s_per_step == ns and s_blocks <= n_tiles and SUBLANES % s_per_step == 0
    kern = functools.partial(_prompt_kernel, tm=tm, nt=nt, n_tiles=n_tiles,
                             s_per_step=s_per_step, s_blocks=s_blocks)
    out_shape = (
        jax.ShapeDtypeStruct((bsz, seq, D), _F32),
        jax.ShapeDtypeStruct((bsz, 1, D), _F32),
        jax.ShapeDtypeStruct((bsz, CONV_W - 1, D), _F32),
        jax.ShapeDtypeStruct((bsz, HEADS, DK, DV), _F32),
        jax.ShapeDtypeStruct((ns, D), _F32),
        jax.ShapeDtypeStruct(ss0.shape, _F32),
        jax.ShapeDtypeStruct((ns, D), _F32),
        jax.ShapeDtypeStruct((ns, D), _F32),
    )
    s_block = lambda i: (jnp.minimum(i, s_blocks - 1), 0, 0, 0)
    cur = lambda i: jnp.minimum(i, n_tiles - 1)
    prev = lambda i: jnp.maximum(i - 1, 0)
    nxt = lambda i: jnp.minimum(i + 1, n_tiles - 1)
    tile = lambda f: (lambda i: (f(i) // nt, lax.rem(f(i), nt), 0))
    seq_of = lambda f: (lambda i: (f(i) // nt, 0, 0))
    hbm = pl.BlockSpec(memory_space=pl.ANY)
    in_specs = [
        pl.BlockSpec((1, tm, D), tile(nxt)),
        pl.BlockSpec((1, tm, D), tile(prev)),
        _const_spec((bsz, D)),
        _const_spec((ns, D)),
        hbm,
        _const_spec((1, 3 * D)),
        _const_spec((1, D)),
        hbm, hbm,
        _const_spec((CONV_W, D)),
        _const_spec((1, D)),
        _const_spec((LRU_BLOCKS, LRU_BW, LRU_BW)),
        _const_spec((LRU_BLOCKS, LRU_BW, LRU_BW)),
        _const_spec((1, D)),
        _const_spec((1, D)),
        _const_spec((1, D)),
        _const_spec((RANK, D_QK)),
        _const_spec((1, D_QK)),
        _const_spec((1, D)),
        _const_spec((1, D)),
        pl.BlockSpec((s_per_step, HEADS, DK, DV), s_block),
        _const_spec((ns, D)),
        _const_spec((ns, D)),
        _const_spec((ns, (CONV_W - 1) * D)),
    ]
    out_specs = (
        pl.BlockSpec((1, tm, D), tile(prev)),
        pl.BlockSpec((1, 1, D), seq_of(cur)),
        pl.BlockSpec((1, CONV_W - 1, D), seq_of(cur)),
        pl.BlockSpec((1, HEADS, DK, DV), lambda i: (cur(i) // nt, 0, 0, 0)),
        pl.BlockSpec((ns, D), lambda i: (0, 0)),
        pl.BlockSpec((s_per_step, HEADS, DK, DV), s_block),
        pl.BlockSpec((ns, D), lambda i: (0, 0)),
        pl.BlockSpec((ns, D), lambda i: (0, 0)),
    )
    scratch = [
        pltpu.VMEM((N_GROUPS, D, D), _BF),
        pltpu.VMEM((D_MIX, D), _BF),
        pltpu.VMEM((RANK_PAD, D), _BF),
        pltpu.VMEM((RANK_PAD, D_QK), _BF),
        pltpu.VMEM((LRU_BLOCKS, LRU_BW, 2 * LRU_BW), _BF),
        pltpu.VMEM((2, STAGE_ROWS, D), _F32),
        pltpu.SemaphoreType.DMA((2,)),
        pltpu.VMEM((tm, D), _BF),
        pltpu.VMEM((tm + SUBLANES, D), _F32),
        pltpu.VMEM((tm, D), _F32),
        pltpu.VMEM((tm, D), _F32),
        pltpu.VMEM((tm, 2 * D), _F32),
        pltpu.VMEM((tm, D_QK), _F32),
        pltpu.VMEM((tm, D_QK), _F32),
        pltpu.VMEM((tm, D), _F32),
        pltpu.VMEM((tm, D), _F32),
        pltpu.VMEM((tm, D_QK), _F32),
        pltpu.VMEM((2, tm, D_MIX), _BF),
        pltpu.VMEM((tm, D), _F32),
        pltpu.VMEM((SUBLANES, D), _F32),
        pltpu.VMEM((1, D), _F32),
        pltpu.VMEM((HEADS, DV, DK), _F32),
        pltpu.VMEM((ns, D), _F32),
        pltpu.VMEM((bsz, SUBLANES, D), _F32),
        pltpu.VMEM((ns, D_QK), _F32),
        pltpu.VMEM((ns, D_QK), _F32),
        pltpu.VMEM((ns, D_QK), _F32),
        pltpu.VMEM((ns, D), _F32),
        pltpu.VMEM((ns, D), _F32),
        pltpu.VMEM((ns, D), _F32),
        pltpu.VMEM((ns, D), _F32),
    ]
    return pl.pallas_call(
        kern,
        out_shape=out_shape,
        grid=(n_tiles + 1,),
        in_specs=in_specs,
        out_specs=out_specs,
        scratch_shapes=scratch,
        compiler_params=pltpu.CompilerParams(
            dimension_semantics=("arbitrary",),
            vmem_limit_bytes=VMEM_LIMIT),
        name="prompt_layer",
    )(x, x, c_prompt, c_sample, w_ada, b_ada, gnorm, w_in_t, w_out2d, convw, convb, wgx, wga, bgx,
      bga, lam, wg2, bg2, ggla, gfin, ss0, sx, h0, conv0)


def kernel(x_prompt, x_sample, state_lru_h, state_lru_conv, state_gla, c_prompt, c_sample,
           g_norm, w_ada, b_ada, w_in, conv_w, conv_b, w_gate_x, b_gate_x, w_gate_a, b_gate_a,
           lru_lambda, w_gla_g2, b_gla_g2, g_gla_norm, w_out, g_final):
    depth = g_norm.shape[0]
    assert depth == 1, "single-layer trunk"
    bp = x_prompt.shape[0]
    ns = x_sample.shape[0]

    w_in_t = jnp.swapaxes(w_in, 1, 2).reshape(w_in.shape[-1], D)
    w_out2d = w_out.reshape(D_MIX, D)
    wg2 = w_gla_g2.reshape(RANK, D_QK)
    wgx = w_gate_x.reshape(LRU_BLOCKS, LRU_BW, LRU_BW)
    wga = w_gate_a.reshape(LRU_BLOCKS, LRU_BW, LRU_BW)
    row = lambda a: a.reshape(1, -1)
    gnorm, convb = row(g_norm), row(conv_b)
    bgx, bga, lam = row(b_gate_x), row(b_gate_a), row(lru_lambda)
    bg2, ggla, gfin = row(b_gla_g2), row(g_gla_norm), row(g_final)
    convw = conv_w.reshape(CONV_W, D)

    xs = x_sample.reshape(ns, D)
    conv0 = state_lru_conv.reshape(ns, CONV_W - 1, D)
    yp, hp, cp, sp, ys, ss, hs, xa = _prompt(
        x_prompt, c_prompt, c_sample, w_ada.reshape(D, 3 * D),
        row(b_ada), gnorm, w_in_t, w_out2d, convw, convb, wgx, wga, bgx, bga, lam, wg2, bg2,
        ggla, gfin, state_gla.reshape(ns, HEADS, DK, DV), xs, state_lru_h.reshape(ns, D),
        state_lru_conv.reshape(ns, (CONV_W - 1) * D))
    cs = jnp.stack([conv0[:, 1], conv0[:, 2], xa], axis=1)

    return (yp, ys.reshape(ns, 1, D), hp.reshape(1, bp, D), cp[None], sp[None],
            hs[None], cs[None], ss[None])
```

```python
import functools

import jax
import jax.numpy as jnp
from jax import lax
from jax.experimental import pallas as pl
from jax.experimental.pallas import tpu as pltpu

EPS = 1e-6
D = 1024
LRU_BLOCKS = 8
LRU_BW = D // LRU_BLOCKS
LRU_C = 8.0
CONV_W = 4
HEADS = 4
DV = D // HEADS
DK = DV // 2
RANK = 16
RANK_PAD = 128
TAU = 16.0
CHUNK = 64
D_QK = HEADS * DK
D_MIX = 2 * D
O_GLR = 4 * D
N_GROUPS = 5

SUBLANES = 8
TM = 256
STAGE_ROWS = 256
VMEM_LIMIT = 56 * 1024 * 1024
NEG_LOG2E = -1.4426950408889634

_BF = jnp.bfloat16
_F32 = jnp.float32

_NT = (((1,), (1,)), ((), ()))


def _sigmoid(x):
    return 1.0 / (1.0 + jnp.exp2(x * NEG_LOG2E))


def _silu(x):
    return x * _sigmoid(x)


def _log_sigmoid(x):
    return jnp.minimum(x, 0.0) - jnp.log(1.0 + jnp.exp2(jnp.abs(x) * NEG_LOG2E))


def _rms(x):
    return x * lax.rsqrt(jnp.mean(x * x, axis=-1, keepdims=True) + EPS)


def _dot(a, b):
    return jnp.dot(a, b, preferred_element_type=_F32)


def _group_scan(a3, u3):
    row = lax.broadcasted_iota(jnp.int32, a3.shape, 1)
    for s in (1, 2, 4):
        a_sh = pltpu.roll(a3, s, axis=1)
        u_sh = pltpu.roll(u3, s, axis=1)
        m = row >= s
        u3 = jnp.where(m, a3 * u_sh + u3, u3)
        a3 = jnp.where(m, a3 * a_sh, a3)
    return a3, u3


def _group_cumsum(x3):
    row = lax.broadcasted_iota(jnp.int32, x3.shape, 1)
    for s in (1, 2, 4):
        x3 = jnp.where(row >= s, x3 + pltpu.roll(x3, s, axis=1), x3)
    return x3


def _linear_scan(a, u, h0):
    r, l = a.shape
    g = r // SUBLANES
    a3, u3 = _group_scan(a.reshape(g, SUBLANES, l), u.reshape(g, SUBLANES, l))
    carry = h0
    out = []
    for i in range(g):
        hg = a3[i] * carry + u3[i]
        out.append(hg)
        carry = hg[SUBLANES - 1:SUBLANES]
    return jnp.concatenate(out, axis=0)


def _chunk_cumsum(x):
    r, l = x.shape
    g = r // SUBLANES
    per_chunk = CHUNK // SUBLANES
    x3 = _group_cumsum(x.reshape(g, SUBLANES, l))
    out = []
    carry = None
    for i in range(g):
        xg = x3[i] if i % per_chunk == 0 else x3[i] + carry
        out.append(xg)
        carry = xg[SUBLANES - 1:SUBLANES]
    return jnp.concatenate(out, axis=0)


def _lru_block(xc_n, pre, bgx, bga, c_logsig):
    gx = _sigmoid(pre[:, :LRU_BW] + bgx)
    ga = _sigmoid(pre[:, LRU_BW:] + bga)
    log_a = ga * c_logsig
    a = jnp.exp(log_a)
    om = 1.0 - a * a
    mult = jnp.where(om > 0.0, om * lax.rsqrt(om), 0.0)
    u = mult * (gx * xc_n)
    return a, u


def _gla_token_step(qs, k, eg, v, s0):
    eye = (lax.broadcasted_iota(jnp.int32, (DK, DK), 0)
           == lax.broadcasted_iota(jnp.int32, (DK, DK), 1))

    def to_col(row):
        return jnp.sum(jnp.where(eye, jnp.broadcast_to(row, (DK, DK)), 0.0),
                       axis=-1, keepdims=True)

    qk = jnp.sum(qs * k, axis=-1, keepdims=True)
    qg = jnp.broadcast_to(qs * eg, (SUBLANES, DK)).astype(_BF)
    o = _dot(qg, s0.astype(_BF))[0:1] + qk * v
    return o, to_col(eg) * s0 + to_col(k) * v


def _stage_blocks(n, src_block, store_block, stage, sem):
    def copy(i, slot):
        return pltpu.make_async_copy(src_block(i), stage.at[slot], sem.at[slot])

    copy(0, 0).start()

    def body(i, carry):
        slot = lax.rem(i, 2)

        @pl.when(i + 1 < n)
        def _():
            copy(i + 1, 1 - slot).start()

        copy(i, slot).wait()
        store_block(i, stage[slot].astype(_BF))
        return carry

    lax.fori_loop(0, n, body, 0)


def _stage_in_proj(wt_hbm, wmain_s, stage, sem):
    per_group = D // STAGE_ROWS

    def block(g, c):
        row0 = g * D + (g // (N_GROUPS - 1)) * RANK + c * STAGE_ROWS
        return wt_hbm.at[pl.ds(pl.multiple_of(row0, SUBLANES), STAGE_ROWS), :]

    def copy(g, c):
        return pltpu.make_async_copy(block(g, c), stage.at[c % 2], sem.at[c % 2])

    copy(0, 0).start()

    def body(g, carry):
        for c in range(per_group):
            if c + 1 < per_group:
                copy(g, c + 1).start()
            else:
                @pl.when(g + 1 < N_GROUPS)
                def _():
                    copy(g + 1, 0).start()
            copy(g, c).wait()
            wmain_s[g, :, c * STAGE_ROWS:(c + 1) * STAGE_ROWS] = stage[c % 2].T.astype(_BF)
        return carry

    lax.fori_loop(0, N_GROUPS, body, 0)


def _prompt_kernel(xn_ref, xp_ref, cp_ref, cs_ref, wada_hbm, bada_ref, gnorm_ref, wt_hbm, wout_hbm,
                   convw_ref, convb_ref, wgx_ref, wga_ref, bgx_ref, bga_ref, lam_ref, wg2_ref,
                   bg2_ref, ggla_ref, gfin_ref, ss0_ref, sx_ref, h0_ref, conv0_ref,
                   y_ref, hlast_ref, convout_ref, sout_ref, ys_ref, ssnew_ref, hnew_ref, cnew_ref,
                   wmain_s, wout_s, wglr_s, wg2_s, wgate_s, stage, sem,
                   hn_s, xa_s, za_s, xc_s, pre_s, q_s, k_s, v_s, zg_s, b_s, mix_s, yacc_s, ctail_s,
                   h_s, st_s, so_s, ada_s, sqs_ref, sk_ref, seg_ref, sv_ref, sgate_ref, szg_ref,
                   sylru_ref, *, tm, nt, n_tiles, s_per_step, s_blocks):
    i = pl.program_id(0)
    bp = n_tiles // nt
    adan_ref = ada_s.at[pl.ds(jnp.minimum(i + 1, n_tiles - 1) // nt, 1)]
    adap_ref = ada_s.at[pl.ds(jnp.maximum(i - 1, 0) // nt, 1)]
    t = lax.rem(jnp.minimum(i, n_tiles - 1), nt)
    t_next = lax.rem(jnp.minimum(i + 1, n_tiles - 1), nt)
    slot = lax.rem(i, 2)
    half = D // 2
    quarter = D // 4

    def front_norm(x_ref, ada_ref):
        x = x_ref[0]
        gain = gnorm_ref[...] * (1.0 + ada_ref[0, 1:2, :])
        hn_s[...] = (_rms(x) * gain + ada_ref[0, 0:1, :]).astype(_BF)

    def front_xa():
        xa_s[SUBLANES:SUBLANES + tm, :] = _dot(hn_s[...], wmain_s[0])

    def front_za():
        za_s[...] = _dot(hn_s[...], wmain_s[1])

    def front_conv():
        xe = xa_s[...]
        xc = convb_ref[...]
        for j in range(CONV_W):
            lag = CONV_W - 1 - j
            xj = xe if lag == 0 else pltpu.roll(xe, lag, axis=0)
            xc = xc + xj[SUBLANES:, :] * convw_ref[j:j + 1, :]
        xc_s[...] = xc

    def front_gates():
        for n in range(LRU_BLOCKS):
            pre_s[:, 2 * n * LRU_BW:2 * (n + 1) * LRU_BW] = _dot(
                xc_s[:, n * LRU_BW:(n + 1) * LRU_BW].astype(_BF), wgate_s[n])

    def decode_front():
        sc = jnp.concatenate([_silu(cp_ref[...]), _silu(cs_ref[...])], axis=0).astype(_BF)
        blocks = [(j, kb) for j in range(3) for kb in range(D // STAGE_ROWS)]

        def ada_copy(n):
            j, kb = blocks[n]
            return pltpu.make_async_copy(
                wada_hbm.at[pl.ds(kb * STAGE_ROWS, STAGE_ROWS), pl.ds(j * D, D)],
                stage.at[n % 2], sem.at[n % 2])

        ada_copy(0).start()
        ada = []
        acc = None
        for n, (j, kb) in enumerate(blocks):
            if n + 1 < len(blocks):
                ada_copy(n + 1).start()
            ada_copy(n).wait()
            part = _dot(sc[:, kb * STAGE_ROWS:(kb + 1) * STAGE_ROWS], stage[n % 2].astype(_BF))
            acc = part if kb == 0 else acc + part
            if kb == D // STAGE_ROWS - 1:
                ada.append(acc + bada_ref[:, j * D:(j + 1) * D])
        for j in range(3):
            for b in range(bp):
                ada_s[b, j:j + 1, :] = ada[j][b:b + 1, :]
        shift, scale = ada[0][bp:, :], ada[1][bp:, :]
        sgate_ref[...] = ada[2][bp:, :]

        hnb = (_rms(sx_ref[...]) * (gnorm_ref[...] * (1.0 + scale)) + shift).astype(_BF)
        xa = _dot(hnb, wmain_s[0])
        za = _dot(hnb, wmain_s[1])
        qk = _dot(hnb, wmain_s[2])
        sqs_ref[...] = qk[:, :D_QK] * (DK ** -0.5)
        sk_ref[...] = qk[:, D_QK:]
        sv_ref[...] = _dot(hnb, wmain_s[3])
        szg_ref[...] = _dot(hnb, wmain_s[4])
        glr = lax.dot_general(hnb, wglr_s[...], _NT, preferred_element_type=_F32)
        la = _log_sigmoid(_dot(glr.astype(_BF), wg2_s[...]) + bg2_ref[...]) / TAU
        seg_ref[...] = jnp.exp(la)
        cnew_ref[:, 0:(CONV_W - 2) * D] = conv0_ref[:, D:(CONV_W - 1) * D]
        cnew_ref[:, (CONV_W - 2) * D:(CONV_W - 1) * D] = xa

        xc = convb_ref[...]
        for j in range(CONV_W):
            cj = conv0_ref[:, j * D:(j + 1) * D] if j < CONV_W - 1 else xa
            xc = xc + cj * convw_ref[j:j + 1, :]
        xcb = xc.astype(_BF)
        logsig = LRU_C * _log_sigmoid(lam_ref[...])
        for n in range(LRU_BLOCKS):
            ln = slice(n * LRU_BW, (n + 1) * LRU_BW)
            a, u = _lru_block(xc[:, ln], _dot(xcb[:, ln], wgate_s[n]), bgx_ref[:, ln],
                              bga_ref[:, ln], logsig[:, ln])
            h = a * h0_ref[:, ln] + u
            hnew_ref[:, ln] = h
            sylru_ref[:, ln] = h * _silu(za[:, ln])

    @pl.when(i == 0)
    def _():
        def out_rows(j):
            return pl.ds(pl.multiple_of(j * STAGE_ROWS, STAGE_ROWS), STAGE_ROWS)

        def store_out(j, val):
            wout_s[out_rows(j), :] = val

        _stage_in_proj(wt_hbm, wmain_s, stage, sem)
        _stage_blocks(D_MIX // STAGE_ROWS, lambda j: wout_hbm.at[out_rows(j), :], store_out,
                      stage, sem)
        glr_rows = pltpu.make_async_copy(wt_hbm.at[pl.ds(O_GLR, RANK), :],
                                         stage.at[0, pl.ds(0, RANK), :], sem.at[0])
        glr_rows.start()
        wglr_s[...] = jnp.zeros_like(wglr_s)
        wg2_s[...] = jnp.zeros_like(wg2_s)
        wg2_s[0:RANK, :] = wg2_ref[...].astype(_BF)
        wgate_s[:, :, 0:LRU_BW] = wgx_ref[...].astype(_BF)
        wgate_s[:, :, LRU_BW:2 * LRU_BW] = wga_ref[...].astype(_BF)
        glr_rows.wait()
        wglr_s[0:RANK, :] = stage[0, 0:RANK, :].astype(_BF)
        decode_front()
        so_s[...] = jnp.zeros_like(so_s)
        mix_s[...] = jnp.zeros_like(mix_s)
        xa_s[0:SUBLANES, :] = jnp.zeros((SUBLANES, D), _F32)
        front_norm(xp_ref, adap_ref)
        front_xa()
        front_za()
        front_conv()
        front_gates()

    @pl.when(t == 0)
    def _():
        h_s[...] = jnp.zeros_like(h_s)
        st_s[...] = jnp.zeros_like(st_s)

    def out_slab(j):
        def run():
            cols = slice(j * quarter, (j + 1) * quarter)
            o = _dot(mix_s[1 - slot], wout_s[:, cols])
            yacc_s[:, cols] = xp_ref[0, :, cols] + adap_ref[0, 2:3, cols] * o
        return run

    def out_norm():
        y_ref[0] = _rms(yacc_s[...]) * gfin_ref[...]

    @pl.when(i < n_tiles)
    def _():
        first = i * s_per_step
        grp = pl.ds(pl.multiple_of((first // SUBLANES) * SUBLANES, SUBLANES), SUBLANES)
        r0 = lax.rem(first, SUBLANES)
        sub = lax.broadcasted_iota(jnp.int32, (SUBLANES, 1), 0)
        g_qs, g_k, g_eg, g_v = sqs_ref[grp, :], sk_ref[grp, :], seg_ref[grp, :], sv_ref[grp, :]
        g_o = so_s[grp, :]
        for j in range(s_per_step):
            mine = sub == r0 + j
            pick = lambda a: jnp.sum(jnp.where(mine, a, 0.0), axis=0, keepdims=True)
            qs_j, k_j, eg_j, v_j = pick(g_qs), pick(g_k), pick(g_eg), pick(g_v)
            o_heads = []
            for h in range(HEADS):
                lk = slice(h * DK, (h + 1) * DK)
                lv = slice(h * DV, (h + 1) * DV)
                o, snew = _gla_token_step(qs_j[:, lk], k_j[:, lk], eg_j[:, lk], v_j[:, lv],
                                          ss0_ref[j, h])
                o_heads.append(o)
                ssnew_ref[j, h] = snew
            g_o = jnp.where(mine, jnp.concatenate(o_heads, axis=-1), g_o)
        so_s[grp, :] = g_o

        def proj(dst, dst_cols, group, w_cols):
            def run():
                dst[:, dst_cols] = _dot(hn_s[...], wmain_s[group, :, w_cols])
            return run

        def decay():
            glr = lax.dot_general(hn_s[...], wglr_s[...], _NT,
                                  preferred_element_type=_F32)
            la = _log_sigmoid(_dot(glr.astype(_BF), wg2_s[...]) + bg2_ref[...]) / TAU
            b_s[...] = _chunk_cumsum(la)

        lo, hi, full = slice(0, half), slice(half, D), slice(0, D_QK)
        tasks = [
            (out_slab(0), proj(q_s, full, 2, lo)),
            (out_slab(1), proj(k_s, full, 2, hi)),
            (out_slab(2), decay),
            (out_slab(3), out_norm),
            (proj(v_s, lo, 3, lo),),
            (proj(v_s, hi, 3, hi),),
            (proj(zg_s, lo, 4, lo),),
            (proj(zg_s, hi, 4, hi),),
        ]

        logsig = LRU_C * _log_sigmoid(lam_ref[...])
        for n in range(LRU_BLOCKS):
            ln = slice(n * LRU_BW, (n + 1) * LRU_BW)
            a, u = _lru_block(xc_s[:, ln], pre_s[:, 2 * n * LRU_BW:2 * (n + 1) * LRU_BW],
                              bgx_ref[:, ln], bga_ref[:, ln], logsig[:, ln])
            h = _linear_scan(a, u, h_s[:, ln])
            h_s[:, ln] = h[tm - 1:tm]
            mix_s[slot, :, ln] = (h * _silu(za_s[:, ln])).astype(_BF)
            for task in tasks[n]:
                task()

        tril = (lax.broadcasted_iota(jnp.int32, (CHUNK, CHUNK), 0)
                >= lax.broadcasted_iota(jnp.int32, (CHUNK, CHUNK), 1))

        def gla_chunk(c):
            rows = slice(c * CHUNK, (c + 1) * CHUNK)
            for h in range(HEADS):
                lk = slice(h * DK, (h + 1) * DK)
                lv = slice(h * DV, (h + 1) * DV)
                b = b_s[rows, lk]
                bl = b[CHUNK - 1:CHUNK]
                q = q_s[rows, lk] * (DK ** -0.5)
                k = k_s[rows, lk]
                v = v_s[rows, lv]
                qi = (q * jnp.exp(b)).astype(_BF)
                ki = (k * jnp.exp(-b)).astype(_BF)
                kd = (k * jnp.exp(bl - b)).astype(_BF)
                vb = v.astype(_BF)
                att = lax.dot_general(qi, ki, _NT, preferred_element_type=_F32)
                att = jnp.where(tril, att, 0.0).astype(_BF)
                st = st_s[h]
                o = _dot(att, vb) + lax.dot_general(qi, st.astype(_BF), _NT,
                                                    preferred_element_type=_F32)
                st_s[h] = st * jnp.exp(bl) + _dot(v.T.astype(_BF), kd)
                o = _rms(o) * ggla_ref[:, lv]
                mix_s[slot, rows, D + h * DV:D + (h + 1) * DV] = (
                    o * _silu(zg_s[rows, lv])).astype(_BF)

        def next_tail():
            tail = xa_s[tm:tm + SUBLANES, :]
            ctail_s[...] = tail
            xa_s[0:SUBLANES, :] = jnp.where(t_next != 0, tail, 0.0)

        next_tail()
        front_norm(xn_ref, adan_ref)
        gla_chunk(0)
        front_xa()
        gla_chunk(1)
        front_za()
        front_conv()
        gla_chunk(2)
        front_gates()
        gla_chunk(3)

    @pl.when(i == n_tiles)
    def _():
        for j in range(D // quarter):
            out_slab(j)()
        out_norm()
        parts = [sylru_ref[...].astype(_BF)]
        for h in range(HEADS):
            lv = slice(h * DV, (h + 1) * DV)
            o = _rms(so_s[:, lv]) * ggla_ref[:, lv]
            parts.append((o * _silu(szg_ref[:, lv])).astype(_BF))
        ys = sx_ref[...] + sgate_ref[...] * _dot(jnp.concatenate(parts, axis=-1), wout_s[...])
        ys_ref[...] = _rms(ys) * gfin_ref[...]

    @pl.when((t == nt - 1) & (i < n_tiles))
    def _():
        hlast_ref[0] = h_s[...]
        convout_ref[0] = ctail_s[SUBLANES - (CONV_W - 1):SUBLANES, :]
        for h in range(HEADS):
            sout_ref[0, h] = st_s[h].T


def _const_spec(shape):
    nd = len(shape)
    return pl.BlockSpec(shape, lambda i: (0,) * nd, pipeline_mode=pl.Buffered(1))


def _prompt(x, c_prompt, c_sample, w_ada, b_ada, gnorm, w_in_t, w_out2d, convw, convb, wgx, wga,
            bgx, bga, lam, wg2, bg2, ggla, gfin, ss0, sx, h0, conv0, tm=TM):
    bsz, seq, _ = x.shape
    assert bsz % SUBLANES == 0, "decode rows must start on a sublane-tile boundary of the adaLN rows"
    nt = seq // tm
    n_tiles = bsz * nt
    ns = ss0.shape[0]
    s_per_step = -(-ns // n_tiles)
    s_blocks = ns // s_per_step
    assert tm // CHUNK == 4, "the step body spreads the next tile's input side over 4 GLA chunks"
    assert s_blocks * s_per_step == ns and s_blocks <= n_tiles and SUBLANES % s_per_step == 0
    kern = functools.partial(_prompt_kernel, tm=tm, nt=nt, n_tiles=n_tiles,
                             s_per_step=s_per_step, s_blocks=s_blocks)
    out_shape = (
        jax.ShapeDtypeStruct((bsz, seq, D), _F32),
        jax.ShapeDtypeStruct((bsz, 1, D), _F32),
        jax.ShapeDtypeStruct((bsz, CONV_W - 1, D), _F32),
        jax.ShapeDtypeStruct((bsz, HEADS, DK, DV), _F32),
        jax.ShapeDtypeStruct((ns, D), _F32),
        jax.ShapeDtypeStruct(ss0.shape, _F32),
        jax.ShapeDtypeStruct((ns, D), _F32),
        jax.ShapeDtypeStruct((ns, (CONV_W - 1) * D), _F32),
    )
    s_block = lambda i: (jnp.minimum(i, s_blocks - 1), 0, 0, 0)
    cur = lambda i: jnp.minimum(i, n_tiles - 1)
    prev = lambda i: jnp.maximum(i - 1, 0)
    nxt = lambda i: jnp.minimum(i + 1, n_tiles - 1)
    tile = lambda f: (lambda i: (f(i) // nt, lax.rem(f(i), nt), 0))
    seq_of = lambda f: (lambda i: (f(i) // nt, 0, 0))
    hbm = pl.BlockSpec(memory_space=pl.ANY)
    in_specs = [
        pl.BlockSpec((1, tm, D), tile(nxt)),
        pl.BlockSpec((1, tm, D), tile(prev)),
        _const_spec((bsz, D)),
        _const_spec((ns, D)),
        hbm,
        _const_spec((1, 3 * D)),
        _const_spec((1, D)),
        hbm, hbm,
        _const_spec((CONV_W, D)),
        _const_spec((1, D)),
        _const_spec((LRU_BLOCKS, LRU_BW, LRU_BW)),
        _const_spec((LRU_BLOCKS, LRU_BW, LRU_BW)),
        _const_spec((1, D)),
        _const_spec((1, D)),
        _const_spec((1, D)),
        _const_spec((RANK, D_QK)),
        _const_spec((1, D_QK)),
        _const_spec((1, D)),
        _const_spec((1, D)),
        pl.BlockSpec((s_per_step, HEADS, DK, DV), s_block),
        _const_spec((ns, D)),
        _const_spec((ns, D)),
        _const_spec((ns, (CONV_W - 1) * D)),
    ]
    out_specs = (
        pl.BlockSpec((1, tm, D), tile(prev)),
        pl.BlockSpec((1, 1, D), seq_of(cur)),
        pl.BlockSpec((1, CONV_W - 1, D), seq_of(cur)),
        pl.BlockSpec((1, HEADS, DK, DV), lambda i: (cur(i) // nt, 0, 0, 0)),
        pl.BlockSpec((ns, D), lambda i: (0, 0)),
        pl.BlockSpec((s_per_step, HEADS, DK, DV), s_block),
        pl.BlockSpec((ns, D), lambda i: (0, 0)),
        pl.BlockSpec((ns, (CONV_W - 1) * D), lambda i: (0, 0)),
    )
    scratch = [
        pltpu.VMEM((N_GROUPS, D, D), _BF),
        pltpu.VMEM((D_MIX, D), _BF),
        pltpu.VMEM((RANK_PAD, D), _BF),
        pltpu.VMEM((RANK_PAD, D_QK), _BF),
        pltpu.VMEM((LRU_BLOCKS, LRU_BW, 2 * LRU_BW), _BF),
        pltpu.VMEM((2, STAGE_ROWS, D), _F32),
        pltpu.SemaphoreType.DMA((2,)),
        pltpu.VMEM((tm, D), _BF),
        pltpu.VMEM((tm + SUBLANES, D), _F32),
        pltpu.VMEM((tm, D), _F32),
        pltpu.VMEM((tm, D), _F32),
        pltpu.VMEM((tm, 2 * D), _F32),
        pltpu.VMEM((tm, D_QK), _F32),
        pltpu.VMEM((tm, D_QK), _F32),
        pltpu.VMEM((tm, D), _F32),
        pltpu.VMEM((tm, D), _F32),
        pltpu.VMEM((tm, D_QK), _F32),
        pltpu.VMEM((2, tm, D_MIX), _BF),
        pltpu.VMEM((tm, D), _F32),
        pltpu.VMEM((SUBLANES, D), _F32),
        pltpu.VMEM((1, D), _F32),
        pltpu.VMEM((HEADS, DV, DK), _F32),
        pltpu.VMEM((ns, D), _F32),
        pltpu.VMEM((bsz, SUBLANES, D), _F32),
        pltpu.VMEM((ns, D_QK), _F32),
        pltpu.VMEM((ns, D_QK), _F32),
        pltpu.VMEM((ns, D_QK), _F32),
        pltpu.VMEM((ns, D), _F32),
        pltpu.VMEM((ns, D), _F32),
        pltpu.VMEM((ns, D), _F32),
        pltpu.VMEM((ns, D), _F32),
    ]
    return pl.pallas_call(
        kern,
        out_shape=out_shape,
        grid=(n_tiles + 1,),
        in_specs=in_specs,
        out_specs=out_specs,
        scratch_shapes=scratch,
        compiler_params=pltpu.CompilerParams(
            dimension_semantics=("arbitrary",),
            vmem_limit_bytes=VMEM_LIMIT),
        name="prompt_layer",
    )(x, x, c_prompt, c_sample, w_ada, b_ada, gnorm, w_in_t, w_out2d, convw, convb, wgx, wga, bgx,
      bga, lam, wg2, bg2, ggla, gfin, ss0, sx, h0, conv0)


def kernel(x_prompt, x_sample, state_lru_h, state_lru_conv, state_gla, c_prompt, c_sample,
           g_norm, w_ada, b_ada, w_in, conv_w, conv_b, w_gate_x, b_gate_x, w_gate_a, b_gate_a,
           lru_lambda, w_gla_g2, b_gla_g2, g_gla_norm, w_out, g_final):
    depth = g_norm.shape[0]
    assert depth == 1, "single-layer trunk"
    bp = x_prompt.shape[0]
    ns = x_sample.shape[0]

    w_in_t = jnp.swapaxes(w_in, 1, 2).reshape(w_in.shape[-1], D)
    w_out2d = w_out.reshape(D_MIX, D)
    wg2 = w_gla_g2.reshape(RANK, D_QK)
    wgx = w_gate_x.reshape(LRU_BLOCKS, LRU_BW, LRU_BW)
    wga = w_gate_a.reshape(LRU_BLOCKS, LRU_BW, LRU_BW)
    row = lambda a: a.reshape(1, -1)
    gnorm, convb = row(g_norm), row(conv_b)
    bgx, bga, lam = row(b_gate_x), row(b_gate_a), row(lru_lambda)
    bg2, ggla, gfin = row(b_gla_g2), row(g_gla_norm), row(g_final)
    convw = conv_w.reshape(CONV_W, D)

    xs = x_sample.reshape(ns, D)
    yp, hp, cp, sp, ys, ss, hs, cs = _prompt(
        x_prompt, c_prompt, c_sample, w_ada.reshape(D, 3 * D),
        row(b_ada), gnorm, w_in_t, w_out2d, convw, convb, wgx, wga, bgx, bga, lam, wg2, bg2,
        ggla, gfin, state_gla.reshape(ns, HEADS, DK, DV), xs, state_lru_h.reshape(ns, D),
        state_lru_conv.reshape(ns, (CONV_W - 1) * D))

    return (yp, ys.reshape(ns, 1, D), hp.reshape(1, bp, D), cp[None], sp[None],
            hs[None], cs.reshape(1, ns, CONV_W - 1, D), ss[None])
```

```python
import functools

import jax
import jax.numpy as jnp
from jax import lax
from jax.experimental import pallas as pl
from jax.experimental.pallas import tpu as pltpu

EPS = 1e-6
D = 1024
LRU_BLOCKS = 8
LRU_BW = D // LRU_BLOCKS
LRU_C = 8.0
CONV_W = 4
HEADS = 4
DV = D // HEADS
DK = DV // 2
RANK = 16
RANK_PAD = 128
TAU = 16.0
CHUNK = 64
D_QK = HEADS * DK
D_MIX = 2 * D
O_GLR = 4 * D
N_GROUPS = 5

SUBLANES = 8
TM = 256
STAGE_ROWS = 256
VMEM_LIMIT = 56 * 1024 * 1024
NEG_LOG2E = -1.4426950408889634

_BF = jnp.bfloat16
_F32 = jnp.float32

_NT = (((1,), (1,)), ((), ()))


def _sigmoid(x):
    return 1.0 / (1.0 + jnp.exp2(x * NEG_LOG2E))


def _silu(x):
    return x * _sigmoid(x)


def _log_sigmoid(x):
    return jnp.minimum(x, 0.0) - jnp.log(1.0 + jnp.exp2(jnp.abs(x) * NEG_LOG2E))


def _rms(x):
    return x * lax.rsqrt(jnp.mean(x * x, axis=-1, keepdims=True) + EPS)


def _dot(a, b):
    return jnp.dot(a, b, preferred_element_type=_F32)


def _group_scan(a3, u3):
    row = lax.broadcasted_iota(jnp.int32, a3.shape, 1)
    for s in (1, 2, 4):
        a_sh = pltpu.roll(a3, s, axis=1)
        u_sh = pltpu.roll(u3, s, axis=1)
        m = row >= s
        u3 = jnp.where(m, a3 * u_sh + u3, u3)
        a3 = jnp.where(m, a3 * a_sh, a3)
    return a3, u3


def _group_cumsum(x3):
    row = lax.broadcasted_iota(jnp.int32, x3.shape, 1)
    for s in (1, 2, 4):
        x3 = jnp.where(row >= s, x3 + pltpu.roll(x3, s, axis=1), x3)
    return x3


def _linear_scan(a, u, h0):
    r, l = a.shape
    g = r // SUBLANES
    a3, u3 = _group_scan(a.reshape(g, SUBLANES, l), u.reshape(g, SUBLANES, l))
    carry = h0
    out = []
    for i in range(g):
        hg = a3[i] * carry + u3[i]
        out.append(hg)
        carry = hg[SUBLANES - 1:SUBLANES]
    return jnp.concatenate(out, axis=0)


def _chunk_cumsum(x):
    r, l = x.shape
    g = r // SUBLANES
    per_chunk = CHUNK // SUBLANES
    x3 = _group_cumsum(x.reshape(g, SUBLANES, l))
    out = []
    carry = None
    for i in range(g):
        xg = x3[i] if i % per_chunk == 0 else x3[i] + carry
        out.append(xg)
        carry = xg[SUBLANES - 1:SUBLANES]
    return jnp.concatenate(out, axis=0)


def _lru_block(xc_n, pre, bgx, bga, c_logsig):
    gx = _sigmoid(pre[:, :LRU_BW] + bgx)
    ga = _sigmoid(pre[:, LRU_BW:] + bga)
    log_a = ga * c_logsig
    a = jnp.exp(log_a)
    om = 1.0 - a * a
    mult = jnp.where(om > 0.0, om * lax.rsqrt(om), 0.0)
    u = mult * (gx * xc_n)
    return a, u


def _gla_token_step(qs, k, eg, v, s0):
    eye = (lax.broadcasted_iota(jnp.int32, (DK, DK), 0)
           == lax.broadcasted_iota(jnp.int32, (DK, DK), 1))

    def to_col(row):
        return jnp.sum(jnp.where(eye, jnp.broadcast_to(row, (DK, DK)), 0.0),
                       axis=-1, keepdims=True)

    qk = jnp.sum(qs * k, axis=-1, keepdims=True)
    qg = jnp.broadcast_to(qs * eg, (SUBLANES, DK)).astype(_BF)
    o = _dot(qg, s0.astype(_BF))[0:1] + qk * v
    return o, to_col(eg) * s0 + to_col(k) * v


def _stage_blocks(n, src_block, store_block, stage, sem):
    def copy(i, slot):
        return pltpu.make_async_copy(src_block(i), stage.at[slot], sem.at[slot])

    copy(0, 0).start()

    def body(i, carry):
        slot = lax.rem(i, 2)

        @pl.when(i + 1 < n)
        def _():
            copy(i + 1, 1 - slot).start()

        copy(i, slot).wait()
        store_block(i, stage[slot].astype(_BF))
        return carry

    lax.fori_loop(0, n, body, 0)


def _stage_in_proj(wt_hbm, wmain_s, stage, sem):
    per_group = D // STAGE_ROWS

    def block(g, c):
        row0 = g * D + (g // (N_GROUPS - 1)) * RANK + c * STAGE_ROWS
        return wt_hbm.at[pl.ds(pl.multiple_of(row0, SUBLANES), STAGE_ROWS), :]

    def copy(g, c):
        return pltpu.make_async_copy(block(g, c), stage.at[c % 2], sem.at[c % 2])

    copy(0, 0).start()

    def body(g, carry):
        for c in range(per_group):
            if c + 1 < per_group:
                copy(g, c + 1).start()
            else:
                @pl.when(g + 1 < N_GROUPS)
                def _():
                    copy(g + 1, 0).start()
            copy(g, c).wait()
            wmain_s[g, :, c * STAGE_ROWS:(c + 1) * STAGE_ROWS] = stage[c % 2].T.astype(_BF)
        return carry

    lax.fori_loop(0, N_GROUPS, body, 0)


def _prompt_kernel(xn_ref, xp_ref, cp_ref, cs_ref, wada_hbm, bada_ref, gnorm_ref, wt_hbm, wout_hbm,
                   convw_ref, convb_ref, wgx_ref, wga_ref, bgx_ref, bga_ref, lam_ref, wg2_ref,
                   bg2_ref, ggla_ref, gfin_ref, ss0_ref, sx_ref, h0_ref, conv0_ref,
                   y_ref, hlast_ref, convout_ref, sout_ref, ys_ref, ssnew_ref, hnew_ref, cnew_ref,
                   wmain_s, wout_s, wglr_s, wg2_s, wgate_s, stage, sem,
                   hn_s, xa_s, za_s, xc_s, pre_s, q_s, k_s, v_s, zg_s, b_s, mix_s, yacc_s, ctail_s,
                   h_s, st_s, so_s, ada_s, sqs_ref, sk_ref, seg_ref, sv_ref, sgate_ref, szg_ref,
                   sylru_ref, *, tm, nt, n_tiles, s_per_step, s_blocks):
    i = pl.program_id(0)
    bp = n_tiles // nt
    adan_ref = ada_s.at[pl.ds(jnp.minimum(i + 1, n_tiles - 1) // nt, 1)]
    adap_ref = ada_s.at[pl.ds(jnp.maximum(i - 1, 0) // nt, 1)]
    t = lax.rem(jnp.minimum(i, n_tiles - 1), nt)
    t_next = lax.rem(jnp.minimum(i + 1, n_tiles - 1), nt)
    slot = lax.rem(i, 2)
    half = D // 2
    quarter = D // 4

    def front_norm(x_ref, ada_ref):
        x = x_ref[0]
        gain = gnorm_ref[...] * (1.0 + ada_ref[0, 1:2, :])
        hn_s[...] = (_rms(x) * gain + ada_ref[0, 0:1, :]).astype(_BF)

    def front_xa():
        xa_s[SUBLANES:SUBLANES + tm, :] = _dot(hn_s[...], wmain_s[0])

    def front_za():
        za_s[...] = _dot(hn_s[...], wmain_s[1])

    def front_conv():
        xe = xa_s[...]
        xc = convb_ref[...]
        for j in range(CONV_W):
            lag = CONV_W - 1 - j
            xj = xe if lag == 0 else pltpu.roll(xe, lag, axis=0)
            xc = xc + xj[SUBLANES:, :] * convw_ref[j:j + 1, :]
        xc_s[...] = xc

    def front_gates():
        for n in range(LRU_BLOCKS):
            pre_s[:, 2 * n * LRU_BW:2 * (n + 1) * LRU_BW] = _dot(
                xc_s[:, n * LRU_BW:(n + 1) * LRU_BW].astype(_BF), wgate_s[n])

    def decode_front():
        sc = jnp.concatenate([_silu(cp_ref[...]), _silu(cs_ref[...])], axis=0).astype(_BF)
        blocks = [(j, kb) for j in range(3) for kb in range(D // STAGE_ROWS)]

        def ada_copy(n):
            j, kb = blocks[n]
            return pltpu.make_async_copy(
                wada_hbm.at[pl.ds(kb * STAGE_ROWS, STAGE_ROWS), pl.ds(j * D, D)],
                stage.at[n % 2], sem.at[n % 2])

        ada_copy(0).start()
        ada = []
        acc = None
        for n, (j, kb) in enumerate(blocks):
            if n + 1 < len(blocks):
                ada_copy(n + 1).start()
            ada_copy(n).wait()
            part = _dot(sc[:, kb * STAGE_ROWS:(kb + 1) * STAGE_ROWS], stage[n % 2].astype(_BF))
            acc = part if kb == 0 else acc + part
            if kb == D // STAGE_ROWS - 1:
                ada.append(acc + bada_ref[:, j * D:(j + 1) * D])
        for j in range(3):
            for b in range(bp):
                ada_s[b, j:j + 1, :] = ada[j][b:b + 1, :]
        shift, scale = ada[0][bp:, :], ada[1][bp:, :]
        sgate_ref[...] = ada[2][bp:, :]

        hnb = (_rms(sx_ref[...]) * (gnorm_ref[...] * (1.0 + scale)) + shift).astype(_BF)
        xa = _dot(hnb, wmain_s[0])
        za = _dot(hnb, wmain_s[1])
        qk = _dot(hnb, wmain_s[2])
        sqs_ref[...] = qk[:, :D_QK] * (DK ** -0.5)
        sk_ref[...] = qk[:, D_QK:]
        sv_ref[...] = _dot(hnb, wmain_s[3])
        szg_ref[...] = _dot(hnb, wmain_s[4])
        glr = lax.dot_general(hnb, wglr_s[...], _NT, preferred_element_type=_F32)
        la = _log_sigmoid(_dot(glr.astype(_BF), wg2_s[...]) + bg2_ref[...]) / TAU
        seg_ref[...] = jnp.exp(la)
        for j in range(CONV_W - 2):
            cnew_ref[j] = conv0_ref[j + 1]
        cnew_ref[CONV_W - 2] = xa

        xc = convb_ref[...]
        for j in range(CONV_W):
            cj = conv0_ref[j] if j < CONV_W - 1 else xa
            xc = xc + cj * convw_ref[j:j + 1, :]
        xcb = xc.astype(_BF)
        logsig = LRU_C * _log_sigmoid(lam_ref[...])
        for n in range(LRU_BLOCKS):
            ln = slice(n * LRU_BW, (n + 1) * LRU_BW)
            a, u = _lru_block(xc[:, ln], _dot(xcb[:, ln], wgate_s[n]), bgx_ref[:, ln],
                              bga_ref[:, ln], logsig[:, ln])
            h = a * h0_ref[:, ln] + u
            hnew_ref[:, ln] = h
            sylru_ref[:, ln] = h * _silu(za[:, ln])

    @pl.when(i == 0)
    def _():
        def out_rows(j):
            return pl.ds(pl.multiple_of(j * STAGE_ROWS, STAGE_ROWS), STAGE_ROWS)

        def store_out(j, val):
            wout_s[out_rows(j), :] = val

        _stage_in_proj(wt_hbm, wmain_s, stage, sem)
        _stage_blocks(D_MIX // STAGE_ROWS, lambda j: wout_hbm.at[out_rows(j), :], store_out,
                      stage, sem)
        glr_rows = pltpu.make_async_copy(wt_hbm.at[pl.ds(O_GLR, RANK), :],
                                         stage.at[0, pl.ds(0, RANK), :], sem.at[0])
        glr_rows.start()
        wglr_s[...] = jnp.zeros_like(wglr_s)
        wg2_s[...] = jnp.zeros_like(wg2_s)
        wg2_s[0:RANK, :] = wg2_ref[...].astype(_BF)
        wgate_s[:, :, 0:LRU_BW] = wgx_ref[...].astype(_BF)
        wgate_s[:, :, LRU_BW:2 * LRU_BW] = wga_ref[...].astype(_BF)
        glr_rows.wait()
        wglr_s[0:RANK, :] = stage[0, 0:RANK, :].astype(_BF)
        decode_front()
        so_s[...] = jnp.zeros_like(so_s)
        hlast_ref[...] = jnp.zeros_like(hlast_ref)
        convout_ref[...] = jnp.zeros_like(convout_ref)
        mix_s[...] = jnp.zeros_like(mix_s)
        xa_s[0:SUBLANES, :] = jnp.zeros((SUBLANES, D), _F32)
        front_norm(xp_ref, adap_ref)
        front_xa()
        front_za()
        front_conv()
        front_gates()

    @pl.when(t == 0)
    def _():
        h_s[...] = jnp.zeros_like(h_s)
        st_s[...] = jnp.zeros_like(st_s)

    def out_slab(j):
        def run():
            cols = slice(j * quarter, (j + 1) * quarter)
            o = _dot(mix_s[1 - slot], wout_s[:, cols])
            yacc_s[:, cols] = xp_ref[0, :, cols] + adap_ref[0, 2:3, cols] * o
        return run

    def out_norm():
        y_ref[0] = _rms(yacc_s[...]) * gfin_ref[...]

    @pl.when(i < n_tiles)
    def _():
        first = i * s_per_step
        grp = pl.ds(pl.multiple_of((first // SUBLANES) * SUBLANES, SUBLANES), SUBLANES)
        r0 = lax.rem(first, SUBLANES)
        sub = lax.broadcasted_iota(jnp.int32, (SUBLANES, 1), 0)
        g_qs, g_k, g_eg, g_v = sqs_ref[grp, :], sk_ref[grp, :], seg_ref[grp, :], sv_ref[grp, :]
        g_o = so_s[grp, :]
        for j in range(s_per_step):
            mine = sub == r0 + j
            pick = lambda a: jnp.sum(jnp.where(mine, a, 0.0), axis=0, keepdims=True)
            qs_j, k_j, eg_j, v_j = pick(g_qs), pick(g_k), pick(g_eg), pick(g_v)
            o_heads = []
            for h in range(HEADS):
                lk = slice(h * DK, (h + 1) * DK)
                lv = slice(h * DV, (h + 1) * DV)
                o, snew = _gla_token_step(qs_j[:, lk], k_j[:, lk], eg_j[:, lk], v_j[:, lv],
                                          ss0_ref[j, h])
                o_heads.append(o)
                ssnew_ref[j, h] = snew
            g_o = jnp.where(mine, jnp.concatenate(o_heads, axis=-1), g_o)
        so_s[grp, :] = g_o

        def proj(dst, dst_cols, group, w_cols):
            def run():
                dst[:, dst_cols] = _dot(hn_s[...], wmain_s[group, :, w_cols])
            return run

        def decay():
            glr = lax.dot_general(hn_s[...], wglr_s[...], _NT,
                                  preferred_element_type=_F32)
            la = _log_sigmoid(_dot(glr.astype(_BF), wg2_s[...]) + bg2_ref[...]) / TAU
            b_s[...] = _chunk_cumsum(la)

        lo, hi, full = slice(0, half), slice(half, D), slice(0, D_QK)
        tasks = [
            (out_slab(0), proj(q_s, full, 2, lo)),
            (out_slab(1), proj(k_s, full, 2, hi)),
            (out_slab(2), decay),
            (out_slab(3), out_norm),
            (proj(v_s, lo, 3, lo),),
            (proj(v_s, hi, 3, hi),),
            (proj(zg_s, lo, 4, lo),),
            (proj(zg_s, hi, 4, hi),),
        ]

        logsig = LRU_C * _log_sigmoid(lam_ref[...])
        for n in range(LRU_BLOCKS):
            ln = slice(n * LRU_BW, (n + 1) * LRU_BW)
            a, u = _lru_block(xc_s[:, ln], pre_s[:, 2 * n * LRU_BW:2 * (n + 1) * LRU_BW],
                              bgx_ref[:, ln], bga_ref[:, ln], logsig[:, ln])
            h = _linear_scan(a, u, h_s[:, ln])
            h_s[:, ln] = h[tm - 1:tm]
            mix_s[slot, :, ln] = (h * _silu(za_s[:, ln])).astype(_BF)
            for task in tasks[n]:
                task()

        tril = (lax.broadcasted_iota(jnp.int32, (CHUNK, CHUNK), 0)
                >= lax.broadcasted_iota(jnp.int32, (CHUNK, CHUNK), 1))

        def gla_chunk(c):
            rows = slice(c * CHUNK, (c + 1) * CHUNK)
            for h in range(HEADS):
                lk = slice(h * DK, (h + 1) * DK)
                lv = slice(h * DV, (h + 1) * DV)
                b = b_s[rows, lk]
                bl = b[CHUNK - 1:CHUNK]
                q = q_s[rows, lk] * (DK ** -0.5)
                k = k_s[rows, lk]
                v = v_s[rows, lv]
                qi = (q * jnp.exp(b)).astype(_BF)
                ki = (k * jnp.exp(-b)).astype(_BF)
                kd = (k * jnp.exp(bl - b)).astype(_BF)
                vb = v.astype(_BF)
                att = lax.dot_general(qi, ki, _NT, preferred_element_type=_F32)
                att = jnp.where(tril, att, 0.0).astype(_BF)
                st = st_s[h]
                o = _dot(att, vb) + lax.dot_general(qi, st.astype(_BF), _NT,
                                                    preferred_element_type=_F32)
                st_s[h] = st * jnp.exp(bl) + _dot(v.T.astype(_BF), kd)
                o = _rms(o) * ggla_ref[h:h + 1, :]
                mix_s[slot, rows, D + h * DV:D + (h + 1) * DV] = (
                    o * _silu(zg_s[rows, lv])).astype(_BF)

        def next_tail():
            tail = xa_s[tm:tm + SUBLANES, :]
            ctail_s[...] = tail
            xa_s[0:SUBLANES, :] = jnp.where(t_next != 0, tail, 0.0)

        next_tail()
        front_norm(xn_ref, adan_ref)
        gla_chunk(0)
        front_xa()
        gla_chunk(1)
        front_za()
        front_conv()
        gla_chunk(2)
        front_gates()
        gla_chunk(3)

    @pl.when(i == n_tiles)
    def _():
        for j in range(D // quarter):
            out_slab(j)()
        out_norm()
        parts = [sylru_ref[...].astype(_BF)]
        for h in range(HEADS):
            lv = slice(h * DV, (h + 1) * DV)
            o = _rms(so_s[:, lv]) * ggla_ref[h:h + 1, :]
            parts.append((o * _silu(szg_ref[:, lv])).astype(_BF))
        ys = sx_ref[...] + sgate_ref[...] * _dot(jnp.concatenate(parts, axis=-1), wout_s[...])
        ys_ref[...] = _rms(ys) * gfin_ref[...]

    @pl.when((t == nt - 1) & (i < n_tiles))
    def _():
        mine = lax.broadcasted_iota(jnp.int32, (bp, 1), 0) == jnp.minimum(i, n_tiles - 1) // nt
        hlast_ref[...] = jnp.where(mine, h_s[...], hlast_ref[...])
        for m in range(CONV_W - 1):
            row = SUBLANES - (CONV_W - 1) + m
            convout_ref[m] = jnp.where(mine, ctail_s[row:row + 1, :], convout_ref[m])
        for h in range(HEADS):
            sout_ref[0, h] = st_s[h].T


def _const_spec(shape):
    nd = len(shape)
    return pl.BlockSpec(shape, lambda i: (0,) * nd, pipeline_mode=pl.Buffered(1))


def _prompt(x, c_prompt, c_sample, w_ada, b_ada, gnorm, w_in_t, w_out2d, convw, convb, wgx, wga,
            bgx, bga, lam, wg2, bg2, ggla, gfin, ss0, sx, h0, conv0, tm=TM):
    bsz, seq, _ = x.shape
    assert bsz % SUBLANES == 0, "decode rows must start on a sublane-tile boundary of the adaLN rows"
    nt = seq // tm
    n_tiles = bsz * nt
    ns = ss0.shape[0]
    s_per_step = -(-ns // n_tiles)
    s_blocks = ns // s_per_step
    assert tm // CHUNK == 4, "the step body spreads the next tile's input side over 4 GLA chunks"
    assert s_blocks * s_per_step == ns and s_blocks <= n_tiles and SUBLANES % s_per_step == 0
    kern = functools.partial(_prompt_kernel, tm=tm, nt=nt, n_tiles=n_tiles,
                             s_per_step=s_per_step, s_blocks=s_blocks)
    out_shape = (
        jax.ShapeDtypeStruct((bsz, seq, D), _F32),
        jax.ShapeDtypeStruct((bsz, D), _F32),
        jax.ShapeDtypeStruct((CONV_W - 1, bsz, D), _F32),
        jax.ShapeDtypeStruct((bsz, HEADS, DK, DV), _F32),
        jax.ShapeDtypeStruct((ns, D), _F32),
        jax.ShapeDtypeStruct(ss0.shape, _F32),
        jax.ShapeDtypeStruct((ns, D), _F32),
        jax.ShapeDtypeStruct((CONV_W - 1, ns, D), _F32),
    )
    s_block = lambda i: (jnp.minimum(i, s_blocks - 1), 0, 0, 0)
    cur = lambda i: jnp.minimum(i, n_tiles - 1)
    prev = lambda i: jnp.maximum(i - 1, 0)
    nxt = lambda i: jnp.minimum(i + 1, n_tiles - 1)
    tile = lambda f: (lambda i: (f(i) // nt, lax.rem(f(i), nt), 0))
    seq_of = lambda f: (lambda i: (f(i) // nt, 0, 0))
    hbm = pl.BlockSpec(memory_space=pl.ANY)
    in_specs = [
        pl.BlockSpec((1, tm, D), tile(nxt)),
        pl.BlockSpec((1, tm, D), tile(prev)),
        _const_spec((bsz, D)),
        _const_spec((ns, D)),
        hbm,
        _const_spec((1, 3 * D)),
        _const_spec((1, D)),
        hbm, hbm,
        _const_spec((CONV_W, D)),
        _const_spec((1, D)),
        _const_spec((LRU_BLOCKS, LRU_BW, LRU_BW)),
        _const_spec((LRU_BLOCKS, LRU_BW, LRU_BW)),
        _const_spec((1, D)),
        _const_spec((1, D)),
        _const_spec((1, D)),
        _const_spec((RANK, D_QK)),
        _const_spec((1, D_QK)),
        _const_spec((HEADS, DV)),
        _const_spec((1, D)),
        pl.BlockSpec((s_per_step, HEADS, DK, DV), s_block),
        _const_spec((ns, D)),
        _const_spec((ns, D)),
        _const_spec((CONV_W - 1, ns, D)),
    ]
    out_specs = (
        pl.BlockSpec((1, tm, D), tile(prev)),
        pl.BlockSpec((bsz, D), lambda i: (0, 0)),
        pl.BlockSpec((CONV_W - 1, bsz, D), lambda i: (0, 0, 0)),
        pl.BlockSpec((1, HEADS, DK, DV), lambda i: (cur(i) // nt, 0, 0, 0)),
        pl.BlockSpec((ns, D), lambda i: (0, 0)),
        pl.BlockSpec((s_per_step, HEADS, DK, DV), s_block),
        pl.BlockSpec((ns, D), lambda i: (0, 0)),
        pl.BlockSpec((CONV_W - 1, ns, D), lambda i: (0, 0, 0)),
    )
    scratch = [
        pltpu.VMEM((N_GROUPS, D, D), _BF),
        pltpu.VMEM((D_MIX, D), _BF),
        pltpu.VMEM((RANK_PAD, D), _BF),
        pltpu.VMEM((RANK_PAD, D_QK), _BF),
        pltpu.VMEM((LRU_BLOCKS, LRU_BW, 2 * LRU_BW), _BF),
        pltpu.VMEM((2, STAGE_ROWS, D), _F32),
        pltpu.SemaphoreType.DMA((2,)),
        pltpu.VMEM((tm, D), _BF),
        pltpu.VMEM((tm + SUBLANES, D), _F32),
        pltpu.VMEM((tm, D), _F32),
        pltpu.VMEM((tm, D), _F32),
        pltpu.VMEM((tm, 2 * D), _F32),
        pltpu.VMEM((tm, D_QK), _F32),
        pltpu.VMEM((tm, D_QK), _F32),
        pltpu.VMEM((tm, D), _F32),
        pltpu.VMEM((tm, D), _F32),
        pltpu.VMEM((tm, D_QK), _F32),
        pltpu.VMEM((2, tm, D_MIX), _BF),
        pltpu.VMEM((tm, D), _F32),
        pltpu.VMEM((SUBLANES, D), _F32),
        pltpu.VMEM((1, D), _F32),
        pltpu.VMEM((HEADS, DV, DK), _F32),
        pltpu.VMEM((ns, D), _F32),
        pltpu.VMEM((bsz, SUBLANES, D), _F32),
        pltpu.VMEM((ns, D_QK), _F32),
        pltpu.VMEM((ns, D_QK), _F32),
        pltpu.VMEM((ns, D_QK), _F32),
        pltpu.VMEM((ns, D), _F32),
        pltpu.VMEM((ns, D), _F32),
        pltpu.VMEM((ns, D), _F32),
        pltpu.VMEM((ns, D), _F32),
    ]
    return pl.pallas_call(
        kern,
        out_shape=out_shape,
        grid=(n_tiles + 1,),
        in_specs=in_specs,
        out_specs=out_specs,
        scratch_shapes=scratch,
        compiler_params=pltpu.CompilerParams(
            dimension_semantics=("arbitrary",),
            vmem_limit_bytes=VMEM_LIMIT),
        name="prompt_layer",
    )(x, x, c_prompt, c_sample, w_ada, b_ada, gnorm, w_in_t, w_out2d, convw, convb, wgx, wga, bgx,
      bga, lam, wg2, bg2, ggla, gfin, ss0, sx, h0, conv0)


def kernel(x_prompt, x_sample, state_lru_h, state_lru_conv, state_gla, c_prompt, c_sample,
           g_norm, w_ada, b_ada, w_in, conv_w, conv_b, w_gate_x, b_gate_x, w_gate_a, b_gate_a,
           lru_lambda, w_gla_g2, b_gla_g2, g_gla_norm, w_out, g_final):
    depth = g_norm.shape[0]
    assert depth == 1, "single-layer trunk"
    ns = x_sample.shape[0]

    w_in_t = jnp.swapaxes(w_in, 1, 2).reshape(w_in.shape[-1], D)
    w_out2d = w_out.reshape(D_MIX, D)
    wg2 = w_gla_g2.reshape(RANK, D_QK)
    wgx = w_gate_x.reshape(LRU_BLOCKS, LRU_BW, LRU_BW)
    wga = w_gate_a.reshape(LRU_BLOCKS, LRU_BW, LRU_BW)
    row = lambda a: a.reshape(1, -1)
    gnorm, convb = row(g_norm), row(conv_b)
    bgx, bga, lam = row(b_gate_x), row(b_gate_a), row(lru_lambda)
    bg2, gfin = row(b_gla_g2), row(g_final)
    ggla = g_gla_norm.reshape(HEADS, DV)
    convw = conv_w.reshape(CONV_W, D)

    step_major = lambda a: jnp.swapaxes(a, 0, 1)
    xs = x_sample.reshape(ns, D)
    yp, hp, cp, sp, ys, ss, hs, cs = _prompt(
        x_prompt, c_prompt, c_sample, w_ada.reshape(D, 3 * D),
        row(b_ada), gnorm, w_in_t, w_out2d, convw, convb, wgx, wga, bgx, bga, lam, wg2, bg2,
        ggla, gfin, state_gla.reshape(ns, HEADS, DK, DV), xs, state_lru_h.reshape(ns, D),
        step_major(state_lru_conv.reshape(ns, CONV_W - 1, D)))

    return (yp, ys.reshape(ns, 1, D), hp[None], step_major(cp)[None], sp[None],
            hs[None], step_major(cs)[None], ss[None])
```

```python
import functools

import jax
import jax.numpy as jnp
from jax import lax
from jax.experimental import pallas as pl
from jax.experimental.pallas import tpu as pltpu

EPS = 1e-6
D = 1024
LRU_BLOCKS = 8
LRU_BW = D // LRU_BLOCKS
LRU_C = 8.0
CONV_W = 4
HEADS = 4
DV = D // HEADS
DK = DV // 2
RANK = 16
RANK_PAD = 128
TAU = 16.0
CHUNK = 64
D_QK = HEADS * DK
D_MIX = 2 * D
O_GLR = 4 * D
N_GROUPS = 5

SUBLANES = 8
TM = 256
STAGE_ROWS = 256
VMEM_LIMIT = 56 * 1024 * 1024
NEG_LOG2E = -1.4426950408889634

_BF = jnp.bfloat16
_F32 = jnp.float32

_NT = (((1,), (1,)), ((), ()))


def _sigmoid(x):
    return 1.0 / (1.0 + jnp.exp2(x * NEG_LOG2E))


def _silu(x):
    return x * _sigmoid(x)


def _log_sigmoid(x):
    return jnp.minimum(x, 0.0) - jnp.log(1.0 + jnp.exp2(jnp.abs(x) * NEG_LOG2E))


def _rms(x):
    return x * lax.rsqrt(jnp.mean(x * x, axis=-1, keepdims=True) + EPS)


def _dot(a, b):
    return jnp.dot(a, b, preferred_element_type=_F32)


def _group_scan(a3, u3):
    row = lax.broadcasted_iota(jnp.int32, a3.shape, 1)
    for s in (1, 2, 4):
        a_sh = pltpu.roll(a3, s, axis=1)
        u_sh = pltpu.roll(u3, s, axis=1)
        m = row >= s
        u3 = jnp.where(m, a3 * u_sh + u3, u3)
        a3 = jnp.where(m, a3 * a_sh, a3)
    return a3, u3


def _group_cumsum(x3):
    row = lax.broadcasted_iota(jnp.int32, x3.shape, 1)
    for s in (1, 2, 4):
        x3 = jnp.where(row >= s, x3 + pltpu.roll(x3, s, axis=1), x3)
    return x3


def _linear_scan(a, u, h0):
    r, l = a.shape
    g = r // SUBLANES
    a3, u3 = _group_scan(a.reshape(g, SUBLANES, l), u.reshape(g, SUBLANES, l))
    carry = h0
    out = []
    for i in range(g):
        hg = a3[i] * carry + u3[i]
        out.append(hg)
        carry = hg[SUBLANES - 1:SUBLANES]
    return jnp.concatenate(out, axis=0)


def _chunk_cumsum(x):
    r, l = x.shape
    g = r // SUBLANES
    per_chunk = CHUNK // SUBLANES
    x3 = _group_cumsum(x.reshape(g, SUBLANES, l))
    out = []
    carry = None
    for i in range(g):
        xg = x3[i] if i % per_chunk == 0 else x3[i] + carry
        out.append(xg)
        carry = xg[SUBLANES - 1:SUBLANES]
    return jnp.concatenate(out, axis=0)


def _lru_block(xc_n, pre, bgx, bga, c_logsig):
    gx = _sigmoid(pre[:, :LRU_BW] + bgx)
    ga = _sigmoid(pre[:, LRU_BW:] + bga)
    log_a = ga * c_logsig
    a = jnp.exp(log_a)
    om = 1.0 - a * a
    mult = jnp.where(om > 0.0, om * lax.rsqrt(om), 0.0)
    u = mult * (gx * xc_n)
    return a, u


def _gla_token_step(qs, k, eg, v, s0):
    eye = (lax.broadcasted_iota(jnp.int32, (DK, DK), 0)
           == lax.broadcasted_iota(jnp.int32, (DK, DK), 1))

    def to_col(row):
        return jnp.sum(jnp.where(eye, jnp.broadcast_to(row, (DK, DK)), 0.0),
                       axis=-1, keepdims=True)

    qk = jnp.sum(qs * k, axis=-1, keepdims=True)
    qg = jnp.broadcast_to(qs * eg, (SUBLANES, DK)).astype(_BF)
    o = _dot(qg, s0.astype(_BF))[0:1] + qk * v
    return o, to_col(eg) * s0 + to_col(k) * v


def _stage_blocks(n, src_block, store_block, stage, sem):
    def copy(i, slot):
        return pltpu.make_async_copy(src_block(i), stage.at[slot], sem.at[slot])

    copy(0, 0).start()

    def body(i, carry):
        slot = lax.rem(i, 2)

        @pl.when(i + 1 < n)
        def _():
            copy(i + 1, 1 - slot).start()

        copy(i, slot).wait()
        store_block(i, stage[slot].astype(_BF))
        return carry

    lax.fori_loop(0, n, body, 0)


def _stage_in_proj(wt_hbm, wmain_s, stage, sem):
    per_group = D // STAGE_ROWS

    def block(g, c):
        row0 = g * D + (g // (N_GROUPS - 1)) * RANK + c * STAGE_ROWS
        return wt_hbm.at[pl.ds(pl.multiple_of(row0, SUBLANES), STAGE_ROWS), :]

    def copy(g, c):
        return pltpu.make_async_copy(block(g, c), stage.at[c % 2], sem.at[c % 2])

    copy(0, 0).start()

    def body(g, carry):
        for c in range(per_group):
            if c + 1 < per_group:
                copy(g, c + 1).start()
            else:
                @pl.when(g + 1 < N_GROUPS)
                def _():
                    copy(g + 1, 0).start()
            copy(g, c).wait()
            wmain_s[g, :, c * STAGE_ROWS:(c + 1) * STAGE_ROWS] = stage[c % 2].T.astype(_BF)
        return carry

    lax.fori_loop(0, N_GROUPS, body, 0)


def _prompt_kernel(xn_ref, xp_ref, cp_ref, cs_ref, wada_hbm, bada_ref, gnorm_ref, wt_hbm, wout_hbm,
                   convw_ref, convb_ref, wgx_ref, wga_ref, bgx_ref, bga_ref, lam_ref, wg2_ref,
                   bg2_ref, ggla_ref, gfin_ref, ss0_ref, sx_ref, h0_ref, conv0_ref,
                   y_ref, hlast_ref, convout_ref, sout_ref, ys_ref, ssnew_ref, hnew_ref, cnew_ref,
                   wmain_s, wout_s, wglr_s, wg2_s, wgate_s, stage, sem,
                   hn_s, xa_s, za_s, xc_s, pre_s, q_s, k_s, v_s, zg_s, b_s, mix_s, yacc_s, ctail_s,
                   h_s, st_s, so_s, ada_s, sqs_ref, sk_ref, seg_ref, sv_ref, sgate_ref, szg_ref,
                   sylru_ref, *, tm, nt, n_tiles, s_per_step, s_blocks):
    i = pl.program_id(0)
    bp = n_tiles // nt
    adan_ref = ada_s.at[pl.ds(jnp.minimum(i + 1, n_tiles - 1) // nt, 1)]
    adap_ref = ada_s.at[pl.ds(jnp.maximum(i - 1, 0) // nt, 1)]
    t = lax.rem(jnp.minimum(i, n_tiles - 1), nt)
    t_next = lax.rem(jnp.minimum(i + 1, n_tiles - 1), nt)
    slot = lax.rem(i, 2)
    half = D // 2
    quarter = D // 4

    def front_norm(x_ref, ada_ref):
        x = x_ref[0]
        gain = gnorm_ref[...] * (1.0 + ada_ref[0, 1:2, :])
        hn_s[...] = (_rms(x) * gain + ada_ref[0, 0:1, :]).astype(_BF)

    def front_xa():
        xa_s[SUBLANES:SUBLANES + tm, :] = _dot(hn_s[...], wmain_s[0])

    def front_za():
        za_s[...] = _dot(hn_s[...], wmain_s[1])

    def front_conv():
        xe = xa_s[...]
        xc = convb_ref[...]
        for j in range(CONV_W):
            lag = CONV_W - 1 - j
            xj = xe if lag == 0 else pltpu.roll(xe, lag, axis=0)
            xc = xc + xj[SUBLANES:, :] * convw_ref[j:j + 1, :]
        xc_s[...] = xc

    def front_gates():
        for n in range(LRU_BLOCKS):
            pre_s[:, 2 * n * LRU_BW:2 * (n + 1) * LRU_BW] = _dot(
                xc_s[:, n * LRU_BW:(n + 1) * LRU_BW].astype(_BF), wgate_s[n])

    def decode_front():
        sc = jnp.concatenate([_silu(cp_ref[...]), _silu(cs_ref[...])], axis=0).astype(_BF)
        blocks = [(j, kb) for j in range(3) for kb in range(D // STAGE_ROWS)]

        def ada_copy(n):
            j, kb = blocks[n]
            return pltpu.make_async_copy(
                wada_hbm.at[pl.ds(kb * STAGE_ROWS, STAGE_ROWS), pl.ds(j * D, D)],
                stage.at[n % 2], sem.at[n % 2])

        ada_copy(0).start()
        ada = []
        acc = None
        for n, (j, kb) in enumerate(blocks):
            if n + 1 < len(blocks):
                ada_copy(n + 1).start()
            ada_copy(n).wait()
            part = _dot(sc[:, kb * STAGE_ROWS:(kb + 1) * STAGE_ROWS], stage[n % 2].astype(_BF))
            acc = part if kb == 0 else acc + part
            if kb == D // STAGE_ROWS - 1:
                ada.append(acc + bada_ref[:, j * D:(j + 1) * D])
        for j in range(3):
            for b in range(bp):
                ada_s[b, j:j + 1, :] = ada[j][b:b + 1, :]
        shift, scale = ada[0][bp:, :], ada[1][bp:, :]
        sgate_ref[...] = ada[2][bp:, :]

        hnb = (_rms(sx_ref[:, 0, :]) * (gnorm_ref[...] * (1.0 + scale)) + shift).astype(_BF)
        xa = _dot(hnb, wmain_s[0])
        za = _dot(hnb, wmain_s[1])
        qk = _dot(hnb, wmain_s[2])
        sqs_ref[...] = qk[:, :D_QK] * (DK ** -0.5)
        sk_ref[...] = qk[:, D_QK:]
        sv_ref[...] = _dot(hnb, wmain_s[3])
        szg_ref[...] = _dot(hnb, wmain_s[4])
        glr = lax.dot_general(hnb, wglr_s[...], _NT, preferred_element_type=_F32)
        la = _log_sigmoid(_dot(glr.astype(_BF), wg2_s[...]) + bg2_ref[...]) / TAU
        seg_ref[...] = jnp.exp(la)
        for j in range(CONV_W - 2):
            cnew_ref[j] = conv0_ref[j + 1]
        cnew_ref[CONV_W - 2] = xa

        xc = convb_ref[...]
        for j in range(CONV_W):
            cj = conv0_ref[j] if j < CONV_W - 1 else xa
            xc = xc + cj * convw_ref[j:j + 1, :]
        xcb = xc.astype(_BF)
        logsig = LRU_C * _log_sigmoid(lam_ref[...])
        for n in range(LRU_BLOCKS):
            ln = slice(n * LRU_BW, (n + 1) * LRU_BW)
            a, u = _lru_block(xc[:, ln], _dot(xcb[:, ln], wgate_s[n]), bgx_ref[:, ln],
                              bga_ref[:, ln], logsig[:, ln])
            h = a * h0_ref[:, ln] + u
            hnew_ref[:, ln] = h
            sylru_ref[:, ln] = h * _silu(za[:, ln])

    @pl.when(i == 0)
    def _():
        def out_rows(j):
            return pl.ds(pl.multiple_of(j * STAGE_ROWS, STAGE_ROWS), STAGE_ROWS)

        def store_out(j, val):
            wout_s[out_rows(j), :] = val

        _stage_in_proj(wt_hbm, wmain_s, stage, sem)
        _stage_blocks(D_MIX // STAGE_ROWS, lambda j: wout_hbm.at[out_rows(j), :], store_out,
                      stage, sem)
        glr_rows = pltpu.make_async_copy(wt_hbm.at[pl.ds(O_GLR, RANK), :],
                                         stage.at[0, pl.ds(0, RANK), :], sem.at[0])
        glr_rows.start()
        wglr_s[...] = jnp.zeros_like(wglr_s)
        wg2_s[...] = jnp.zeros_like(wg2_s)
        wg2_s[0:RANK, :] = wg2_ref[...].astype(_BF)
        wgate_s[:, :, 0:LRU_BW] = wgx_ref[...].astype(_BF)
        wgate_s[:, :, LRU_BW:2 * LRU_BW] = wga_ref[...].astype(_BF)
        glr_rows.wait()
        wglr_s[0:RANK, :] = stage[0, 0:RANK, :].astype(_BF)
        decode_front()
        so_s[...] = jnp.zeros_like(so_s)
        hlast_ref[...] = jnp.zeros_like(hlast_ref)
        convout_ref[...] = jnp.zeros_like(convout_ref)
        mix_s[...] = jnp.zeros_like(mix_s)
        xa_s[0:SUBLANES, :] = jnp.zeros((SUBLANES, D), _F32)
        front_norm(xp_ref, adap_ref)
        front_xa()
        front_za()
        front_conv()
        front_gates()

    @pl.when(t == 0)
    def _():
        h_s[...] = jnp.zeros_like(h_s)
        st_s[...] = jnp.zeros_like(st_s)

    def out_slab(j):
        def run():
            cols = slice(j * quarter, (j + 1) * quarter)
            o = _dot(mix_s[1 - slot], wout_s[:, cols])
            yacc_s[:, cols] = xp_ref[0, :, cols] + adap_ref[0, 2:3, cols] * o
        return run

    def out_norm():
        y_ref[0] = _rms(yacc_s[...]) * gfin_ref[...]

    @pl.when(i < n_tiles)
    def _():
        first = i * s_per_step
        grp = pl.ds(pl.multiple_of((first // SUBLANES) * SUBLANES, SUBLANES), SUBLANES)
        r0 = lax.rem(first, SUBLANES)
        sub = lax.broadcasted_iota(jnp.int32, (SUBLANES, 1), 0)
        g_qs, g_k, g_eg, g_v = sqs_ref[grp, :], sk_ref[grp, :], seg_ref[grp, :], sv_ref[grp, :]
        g_o = so_s[grp, :]
        for j in range(s_per_step):
            mine = sub == r0 + j
            pick = lambda a: jnp.sum(jnp.where(mine, a, 0.0), axis=0, keepdims=True)
            qs_j, k_j, eg_j, v_j = pick(g_qs), pick(g_k), pick(g_eg), pick(g_v)
            o_heads = []
            for h in range(HEADS):
                lk = slice(h * DK, (h + 1) * DK)
                lv = slice(h * DV, (h + 1) * DV)
                o, snew = _gla_token_step(qs_j[:, lk], k_j[:, lk], eg_j[:, lk], v_j[:, lv],
                                          ss0_ref[j, h])
                o_heads.append(o)
                ssnew_ref[j, h] = snew
            g_o = jnp.where(mine, jnp.concatenate(o_heads, axis=-1), g_o)
        so_s[grp, :] = g_o

        def proj(dst, dst_cols, group, w_cols):
            def run():
                dst[:, dst_cols] = _dot(hn_s[...], wmain_s[group, :, w_cols])
            return run

        def decay():
            glr = lax.dot_general(hn_s[...], wglr_s[...], _NT,
                                  preferred_element_type=_F32)
            la = _log_sigmoid(_dot(glr.astype(_BF), wg2_s[...]) + bg2_ref[...]) / TAU
            b_s[...] = _chunk_cumsum(la)

        lo, hi, full = slice(0, half), slice(half, D), slice(0, D_QK)
        tasks = [
            (out_slab(0), proj(q_s, full, 2, lo)),
            (out_slab(1), proj(k_s, full, 2, hi)),
            (out_slab(2), decay),
            (out_slab(3), out_norm),
            (proj(v_s, lo, 3, lo),),
            (proj(v_s, hi, 3, hi),),
            (proj(zg_s, lo, 4, lo),),
            (proj(zg_s, hi, 4, hi),),
        ]

        logsig = LRU_C * _log_sigmoid(lam_ref[...])
        for n in range(LRU_BLOCKS):
            ln = slice(n * LRU_BW, (n + 1) * LRU_BW)
            a, u = _lru_block(xc_s[:, ln], pre_s[:, 2 * n * LRU_BW:2 * (n + 1) * LRU_BW],
                              bgx_ref[:, ln], bga_ref[:, ln], logsig[:, ln])
            h = _linear_scan(a, u, h_s[:, ln])
            h_s[:, ln] = h[tm - 1:tm]
            mix_s[slot, :, ln] = (h * _silu(za_s[:, ln])).astype(_BF)
            for task in tasks[n]:
                task()

        tril = (lax.broadcasted_iota(jnp.int32, (CHUNK, CHUNK), 0)
                >= lax.broadcasted_iota(jnp.int32, (CHUNK, CHUNK), 1))

        def gla_chunk(c):
            rows = slice(c * CHUNK, (c + 1) * CHUNK)
            for h in range(HEADS):
                lk = slice(h * DK, (h + 1) * DK)
                lv = slice(h * DV, (h + 1) * DV)
                b = b_s[rows, lk]
                bl = b[CHUNK - 1:CHUNK]
                q = q_s[rows, lk] * (DK ** -0.5)
                k = k_s[rows, lk]
                v = v_s[rows, lv]
                qi = (q * jnp.exp(b)).astype(_BF)
                ki = (k * jnp.exp(-b)).astype(_BF)
                kd = (k * jnp.exp(bl - b)).astype(_BF)
                vb = v.astype(_BF)
                att = lax.dot_general(qi, ki, _NT, preferred_element_type=_F32)
                att = jnp.where(tril, att, 0.0).astype(_BF)
                st = st_s[h]
                o = _dot(att, vb) + lax.dot_general(qi, st.astype(_BF), _NT,
                                                    preferred_element_type=_F32)
                st_s[h] = st * jnp.exp(bl) + _dot(v.T.astype(_BF), kd)
                o = _rms(o) * ggla_ref[h:h + 1, :]
                mix_s[slot, rows, D + h * DV:D + (h + 1) * DV] = (
                    o * _silu(zg_s[rows, lv])).astype(_BF)

        def next_tail():
            tail = xa_s[tm:tm + SUBLANES, :]
            ctail_s[...] = tail
            xa_s[0:SUBLANES, :] = jnp.where(t_next != 0, tail, 0.0)

        next_tail()
        front_norm(xn_ref, adan_ref)
        gla_chunk(0)
        front_xa()
        gla_chunk(1)
        front_za()
        front_conv()
        gla_chunk(2)
        front_gates()
        gla_chunk(3)

    @pl.when(i == n_tiles)
    def _():
        for j in range(D // quarter):
            out_slab(j)()
        out_norm()
        parts = [sylru_ref[...].astype(_BF)]
        for h in range(HEADS):
            lv = slice(h * DV, (h + 1) * DV)
            o = _rms(so_s[:, lv]) * ggla_ref[h:h + 1, :]
            parts.append((o * _silu(szg_ref[:, lv])).astype(_BF))
        ys = sx_ref[:, 0, :] + sgate_ref[...] * _dot(jnp.concatenate(parts, axis=-1), wout_s[...])
        ys_ref[:, 0, :] = _rms(ys) * gfin_ref[...]

    @pl.when((t == nt - 1) & (i < n_tiles))
    def _():
        mine = lax.broadcasted_iota(jnp.int32, (bp, 1), 0) == jnp.minimum(i, n_tiles - 1) // nt
        hlast_ref[...] = jnp.where(mine, h_s[...], hlast_ref[...])
        for m in range(CONV_W - 1):
            row = SUBLANES - (CONV_W - 1) + m
            convout_ref[m] = jnp.where(mine, ctail_s[row:row + 1, :], convout_ref[m])
        for h in range(HEADS):
            sout_ref[0, h] = st_s[h].T


def _const_spec(shape):
    nd = len(shape)
    return pl.BlockSpec(shape, lambda i: (0,) * nd, pipeline_mode=pl.Buffered(1))


def _prompt(x, c_prompt, c_sample, w_ada, b_ada, gnorm, w_in_t, w_out2d, convw, convb, wgx, wga,
            bgx, bga, lam, wg2, bg2, ggla, gfin, ss0, sx, h0, conv0, tm=TM):
    bsz, seq, _ = x.shape
    assert bsz % SUBLANES == 0, "decode rows must start on a sublane-tile boundary of the adaLN rows"
    nt = seq // tm
    n_tiles = bsz * nt
    ns = ss0.shape[0]
    s_per_step = -(-ns // n_tiles)
    s_blocks = ns // s_per_step
    assert tm // CHUNK == 4, "the step body spreads the next tile's input side over 4 GLA chunks"
    assert s_blocks * s_per_step == ns and s_blocks <= n_tiles and SUBLANES % s_per_step == 0
    kern = functools.partial(_prompt_kernel, tm=tm, nt=nt, n_tiles=n_tiles,
                             s_per_step=s_per_step, s_blocks=s_blocks)
    out_shape = (
        jax.ShapeDtypeStruct((bsz, seq, D), _F32),
        jax.ShapeDtypeStruct((bsz, D), _F32),
        jax.ShapeDtypeStruct((CONV_W - 1, bsz, D), _F32),
        jax.ShapeDtypeStruct((bsz, HEADS, DK, DV), _F32),
        jax.ShapeDtypeStruct((ns, 1, D), _F32),
        jax.ShapeDtypeStruct(ss0.shape, _F32),
        jax.ShapeDtypeStruct((ns, D), _F32),
        jax.ShapeDtypeStruct((CONV_W - 1, ns, D), _F32),
    )
    s_block = lambda i: (jnp.minimum(i, s_blocks - 1), 0, 0, 0)
    cur = lambda i: jnp.minimum(i, n_tiles - 1)
    prev = lambda i: jnp.maximum(i - 1, 0)
    nxt = lambda i: jnp.minimum(i + 1, n_tiles - 1)
    tile = lambda f: (lambda i: (f(i) // nt, lax.rem(f(i), nt), 0))
    seq_of = lambda f: (lambda i: (f(i) // nt, 0, 0))
    hbm = pl.BlockSpec(memory_space=pl.ANY)
    in_specs = [
        pl.BlockSpec((1, tm, D), tile(nxt)),
        pl.BlockSpec((1, tm, D), tile(prev)),
        _const_spec((bsz, D)),
        _const_spec((ns, D)),
        hbm,
        _const_spec((1, 3 * D)),
        _const_spec((1, D)),
        hbm, hbm,
        _const_spec((CONV_W, D)),
        _const_spec((1, D)),
        _const_spec((LRU_BLOCKS, LRU_BW, LRU_BW)),
        _const_spec((LRU_BLOCKS, LRU_BW, LRU_BW)),
        _const_spec((1, D)),
        _const_spec((1, D)),
        _const_spec((1, D)),
        _const_spec((RANK, D_QK)),
        _const_spec((1, D_QK)),
        _const_spec((HEADS, DV)),
        _const_spec((1, D)),
        pl.BlockSpec((s_per_step, HEADS, DK, DV), s_block),
        _const_spec((ns, 1, D)),
        _const_spec((ns, D)),
        _const_spec((CONV_W - 1, ns, D)),
    ]
    out_specs = (
        pl.BlockSpec((1, tm, D), tile(prev)),
        pl.BlockSpec((bsz, D), lambda i: (0, 0)),
        pl.BlockSpec((CONV_W - 1, bsz, D), lambda i: (0, 0, 0)),
        pl.BlockSpec((1, HEADS, DK, DV), lambda i: (cur(i) // nt, 0, 0, 0)),
        pl.BlockSpec((ns, 1, D), lambda i: (0, 0, 0)),
        pl.BlockSpec((s_per_step, HEADS, DK, DV), s_block),
        pl.BlockSpec((ns, D), lambda i: (0, 0)),
        pl.BlockSpec((CONV_W - 1, ns, D), lambda i: (0, 0, 0)),
    )
    scratch = [
        pltpu.VMEM((N_GROUPS, D, D), _BF),
        pltpu.VMEM((D_MIX, D), _BF),
        pltpu.VMEM((RANK_PAD, D), _BF),
        pltpu.VMEM((RANK_PAD, D_QK), _BF),
        pltpu.VMEM((LRU_BLOCKS, LRU_BW, 2 * LRU_BW), _BF),
        pltpu.VMEM((2, STAGE_ROWS, D), _F32),
        pltpu.SemaphoreType.DMA((2,)),
        pltpu.VMEM((tm, D), _BF),
        pltpu.VMEM((tm + SUBLANES, D), _F32),
        pltpu.VMEM((tm, D), _F32),
        pltpu.VMEM((tm, D), _F32),
        pltpu.VMEM((tm, 2 * D), _F32),
        pltpu.VMEM((tm, D_QK), _F32),
        pltpu.VMEM((tm, D_QK), _F32),
        pltpu.VMEM((tm, D), _F32),
        pltpu.VMEM((tm, D), _F32),
        pltpu.VMEM((tm, D_QK), _F32),
        pltpu.VMEM((2, tm, D_MIX), _BF),
        pltpu.VMEM((tm, D), _F32),
        pltpu.VMEM((SUBLANES, D), _F32),
        pltpu.VMEM((1, D), _F32),
        pltpu.VMEM((HEADS, DV, DK), _F32),
        pltpu.VMEM((ns, D), _F32),
        pltpu.VMEM((bsz, SUBLANES, D), _F32),
        pltpu.VMEM((ns, D_QK), _F32),
        pltpu.VMEM((ns, D_QK), _F32),
        pltpu.VMEM((ns, D_QK), _F32),
        pltpu.VMEM((ns, D), _F32),
        pltpu.VMEM((ns, D), _F32),
        pltpu.VMEM((ns, D), _F32),
        pltpu.VMEM((ns, D), _F32),
    ]
    return pl.pallas_call(
        kern,
        out_shape=out_shape,
        grid=(n_tiles + 1,),
        in_specs=in_specs,
        out_specs=out_specs,
        scratch_shapes=scratch,
        compiler_params=pltpu.CompilerParams(
            dimension_semantics=("arbitrary",),
            vmem_limit_bytes=VMEM_LIMIT),
        name="prompt_layer",
    )(x, x, c_prompt, c_sample, w_ada, b_ada, gnorm, w_in_t, w_out2d, convw, convb, wgx, wga, bgx,
      bga, lam, wg2, bg2, ggla, gfin, ss0, sx, h0, conv0)


def kernel(x_prompt, x_sample, state_lru_h, state_lru_conv, state_gla, c_prompt, c_sample,
           g_norm, w_ada, b_ada, w_in, conv_w, conv_b, w_gate_x, b_gate_x, w_gate_a, b_gate_a,
           lru_lambda, w_gla_g2, b_gla_g2, g_gla_norm, w_out, g_final):
    depth = g_norm.shape[0]
    assert depth == 1, "single-layer trunk"
    ns = x_sample.shape[0]

    w_in_t = jnp.swapaxes(w_in, 1, 2).reshape(w_in.shape[-1], D)
    w_out2d = w_out.reshape(D_MIX, D)
    wg2 = w_gla_g2.reshape(RANK, D_QK)
    wgx = w_gate_x.reshape(LRU_BLOCKS, LRU_BW, LRU_BW)
    wga = w_gate_a.reshape(LRU_BLOCKS, LRU_BW, LRU_BW)
    row = lambda a: a.reshape(1, -1)
    gnorm, convb = row(g_norm), row(conv_b)
    bgx, bga, lam = row(b_gate_x), row(b_gate_a), row(lru_lambda)
    bg2, gfin = row(b_gla_g2), row(g_final)
    ggla = g_gla_norm.reshape(HEADS, DV)
    convw = conv_w.reshape(CONV_W, D)

    step_major = lambda a: jnp.swapaxes(a, 0, 1)
    yp, hp, cp, sp, ys, ss, hs, cs = _prompt(
        x_prompt, c_prompt, c_sample, w_ada.reshape(D, 3 * D),
        row(b_ada), gnorm, w_in_t, w_out2d, convw, convb, wgx, wga, bgx, bga, lam, wg2, bg2,
        ggla, gfin, state_gla.reshape(ns, HEADS, DK, DV), x_sample, state_lru_h.reshape(ns, D),
        step_major(state_lru_conv.reshape(ns, CONV_W - 1, D)))

    return (yp, ys, hp[None], step_major(cp)[None], sp[None],
            hs[None], step_major(cs)[None], ss[None])
```

```python
import functools

import jax
import jax.numpy as jnp
from jax import lax
from jax.experimental import pallas as pl
from jax.experimental.pallas import tpu as pltpu

EPS = 1e-6
D = 1024
LRU_BLOCKS = 8
LRU_BW = D // LRU_BLOCKS
LRU_C = 8.0
CONV_W = 4
HEADS = 4
DV = D // HEADS
DK = DV // 2
RANK = 16
RANK_PAD = 128
TAU = 16.0
CHUNK = 64
D_QK = HEADS * DK
D_MIX = 2 * D
O_GLR = 4 * D
N_GROUPS = 5

SUBLANES = 8
TM = 256
STAGE_ROWS = 256
VMEM_LIMIT = 56 * 1024 * 1024
NEG_LOG2E = -1.4426950408889634

_BF = jnp.bfloat16
_F32 = jnp.float32

_NT = (((1,), (1,)), ((), ()))


def _sigmoid(x):
    return 1.0 / (1.0 + jnp.exp2(x * NEG_LOG2E))


def _silu(x):
    return x * _sigmoid(x)


def _log_sigmoid(x):
    return jnp.minimum(x, 0.0) - jnp.log(1.0 + jnp.exp2(jnp.abs(x) * NEG_LOG2E))


def _rms(x):
    return x * lax.rsqrt(jnp.mean(x * x, axis=-1, keepdims=True) + EPS)


def _dot(a, b):
    return jnp.dot(a, b, preferred_element_type=_F32)


def _group_scan(a3, u3):
    row = lax.broadcasted_iota(jnp.int32, a3.shape, 1)
    for s in (1, 2, 4):
        a_sh = pltpu.roll(a3, s, axis=1)
        u_sh = pltpu.roll(u3, s, axis=1)
        m = row >= s
        u3 = jnp.where(m, a3 * u_sh + u3, u3)
        a3 = jnp.where(m, a3 * a_sh, a3)
    return a3, u3


def _group_cumsum(x3):
    row = lax.broadcasted_iota(jnp.int32, x3.shape, 1)
    for s in (1, 2, 4):
        x3 = jnp.where(row >= s, x3 + pltpu.roll(x3, s, axis=1), x3)
    return x3


def _linear_scan(a, u, h0):
    r, l = a.shape
    g = r // SUBLANES
    a3, u3 = _group_scan(a.reshape(g, SUBLANES, l), u.reshape(g, SUBLANES, l))
    carry = h0
    out = []
    for i in range(g):
        hg = a3[i] * carry + u3[i]
        out.append(hg)
        carry = hg[SUBLANES - 1:SUBLANES]
    return jnp.concatenate(out, axis=0)


def _chunk_cumsum(x):
    r, l = x.shape
    g = r // SUBLANES
    per_chunk = CHUNK // SUBLANES
    x3 = _group_cumsum(x.reshape(g, SUBLANES, l))
    out = []
    carry = None
    for i in range(g):
        xg = x3[i] if i % per_chunk == 0 else x3[i] + carry
        out.append(xg)
        carry = xg[SUBLANES - 1:SUBLANES]
    return jnp.concatenate(out, axis=0)


def _lru_block(xc_n, pre, bgx, bga, c_logsig):
    gx = _sigmoid(pre[:, :LRU_BW] + bgx)
    ga = _sigmoid(pre[:, LRU_BW:] + bga)
    log_a = ga * c_logsig
    a = jnp.exp(log_a)
    om = 1.0 - a * a
    mult = jnp.where(om > 0.0, om * lax.rsqrt(om), 0.0)
    u = mult * (gx * xc_n)
    return a, u


def _gla_token_step(qs, k, eg, v, s0):
    eye = (lax.broadcasted_iota(jnp.int32, (DK, DK), 0)
           == lax.broadcasted_iota(jnp.int32, (DK, DK), 1))

    def to_col(row):
        return jnp.sum(jnp.where(eye, jnp.broadcast_to(row, (DK, DK)), 0.0),
                       axis=-1, keepdims=True)

    qk = jnp.sum(qs * k, axis=-1, keepdims=True)
    qg = jnp.broadcast_to(qs * eg, (SUBLANES, DK)).astype(_BF)
    o = _dot(qg, s0.astype(_BF))[0:1] + qk * v
    return o, to_col(eg) * s0 + to_col(k) * v


def _stage_blocks(n, src_block, store_block, stage, sem):
    def copy(i, slot):
        return pltpu.make_async_copy(src_block(i), stage.at[slot], sem.at[slot])

    copy(0, 0).start()

    def body(i, carry):
        slot = lax.rem(i, 2)

        @pl.when(i + 1 < n)
        def _():
            copy(i + 1, 1 - slot).start()

        copy(i, slot).wait()
        store_block(i, stage[slot].astype(_BF))
        return carry

    lax.fori_loop(0, n, body, 0)


def _stage_in_proj(wt_hbm, wmain_s, stage, sem):
    per_group = D // STAGE_ROWS

    def block(g, c):
        row0 = g * D + (g // (N_GROUPS - 1)) * RANK + c * STAGE_ROWS
        return wt_hbm.at[pl.ds(pl.multiple_of(row0, SUBLANES), STAGE_ROWS), :]

    def copy(g, c):
        return pltpu.make_async_copy(block(g, c), stage.at[c % 2], sem.at[c % 2])

    copy(0, 0).start()

    def body(g, carry):
        for c in range(per_group):
            if c + 1 < per_group:
                copy(g, c + 1).start()
            else:
                @pl.when(g + 1 < N_GROUPS)
                def _():
                    copy(g + 1, 0).start()
            copy(g, c).wait()
            wmain_s[g, :, c * STAGE_ROWS:(c + 1) * STAGE_ROWS] = stage[c % 2].T.astype(_BF)
        return carry

    lax.fori_loop(0, N_GROUPS, body, 0)


def _prompt_kernel(xn_ref, xp_ref, cp_ref, cs_ref, wada_hbm, bada_ref, gnorm_ref, wt_hbm, wout_hbm,
                   convw_ref, convb_ref, wgx_ref, wga_ref, bgx_ref, bga_ref, lam_ref, wg2_ref,
                   bg2_ref, ggla_ref, gfin_ref, ss0_ref, sx_ref, h0_ref, conv0_ref,
                   y_ref, hlast_ref, convout_ref, sout_ref, ys_ref, ssnew_ref, hnew_ref, cnew_ref,
                   wmain_s, wout_s, wglr_s, wg2_s, wgate_s, stage, sem,
                   hn_s, xa_s, za_s, xc_s, pre_s, q_s, k_s, v_s, zg_s, b_s, mix_s, yacc_s, ctail_s,
                   h_s, st_s, so_s, ada_s, sqs_ref, sk_ref, seg_ref, sv_ref, sgate_ref, szg_ref,
                   sylru_ref, *, tm, nt, n_tiles, s_per_step, s_blocks):
    i = pl.program_id(0)
    bp = n_tiles // nt
    adan_ref = ada_s.at[pl.ds(jnp.minimum(i + 1, n_tiles - 1) // nt, 1)]
    adap_ref = ada_s.at[pl.ds(jnp.maximum(i - 1, 0) // nt, 1)]
    t = lax.rem(jnp.minimum(i, n_tiles - 1), nt)
    t_next = lax.rem(jnp.minimum(i + 1, n_tiles - 1), nt)
    slot = lax.rem(i, 2)
    half = D // 2
    quarter = D // 4

    def front_norm(x_ref, ada_ref):
        x = x_ref[0]
        gain = gnorm_ref[...] * (1.0 + ada_ref[0, 1:2, :])
        hn_s[...] = (_rms(x) * gain + ada_ref[0, 0:1, :]).astype(_BF)

    def front_xa():
        xa_s[SUBLANES:SUBLANES + tm, :] = _dot(hn_s[...], wmain_s[0])

    def front_za():
        za_s[...] = _dot(hn_s[...], wmain_s[1])

    def front_conv():
        xe = xa_s[...]
        xc = convb_ref[...]
        for j in range(CONV_W):
            lag = CONV_W - 1 - j
            xj = xe if lag == 0 else pltpu.roll(xe, lag, axis=0)
            xc = xc + xj[SUBLANES:, :] * convw_ref[j:j + 1, :]
        xc_s[...] = xc

    def front_gates():
        for n in range(LRU_BLOCKS):
            pre_s[:, 2 * n * LRU_BW:2 * (n + 1) * LRU_BW] = _dot(
                xc_s[:, n * LRU_BW:(n + 1) * LRU_BW].astype(_BF), wgate_s[n])

    def decode_front():
        sc = jnp.concatenate([_silu(cp_ref[...]), _silu(cs_ref[...])], axis=0).astype(_BF)
        blocks = [(j, kb) for j in range(3) for kb in range(D // STAGE_ROWS)]

        def ada_copy(n):
            j, kb = blocks[n]
            return pltpu.make_async_copy(
                wada_hbm.at[pl.ds(kb * STAGE_ROWS, STAGE_ROWS), pl.ds(j * D, D)],
                stage.at[n % 2], sem.at[n % 2])

        ada_copy(0).start()
        ada = []
        acc = None
        for n, (j, kb) in enumerate(blocks):
            if n + 1 < len(blocks):
                ada_copy(n + 1).start()
            ada_copy(n).wait()
            part = _dot(sc[:, kb * STAGE_ROWS:(kb + 1) * STAGE_ROWS], stage[n % 2].astype(_BF))
            acc = part if kb == 0 else acc + part
            if kb == D // STAGE_ROWS - 1:
                ada.append(acc + bada_ref[:, j * D:(j + 1) * D])
        for j in range(3):
            for b in range(bp):
                ada_s[b, j:j + 1, :] = ada[j][b:b + 1, :]
        shift, scale = ada[0][bp:, :], ada[1][bp:, :]
        sgate_ref[...] = ada[2][bp:, :]

        hnb = (_rms(sx_ref[:, 0, :]) * (gnorm_ref[...] * (1.0 + scale)) + shift).astype(_BF)
        xa = _dot(hnb, wmain_s[0])
        za = _dot(hnb, wmain_s[1])
        qk = _dot(hnb, wmain_s[2])
        sqs_ref[...] = qk[:, :D_QK] * (DK ** -0.5)
        sk_ref[...] = qk[:, D_QK:]
        sv_ref[...] = _dot(hnb, wmain_s[3])
        szg_ref[...] = _dot(hnb, wmain_s[4])
        glr = lax.dot_general(hnb, wglr_s[...], _NT, preferred_element_type=_F32)
        la = _log_sigmoid(_dot(glr.astype(_BF), wg2_s[...]) + bg2_ref[...]) / TAU
        seg_ref[...] = jnp.exp(la)
        for j in range(CONV_W - 2):
            cnew_ref[j] = conv0_ref[j + 1]
        cnew_ref[CONV_W - 2] = xa

        xc = convb_ref[...]
        for j in range(CONV_W):
            cj = conv0_ref[j] if j < CONV_W - 1 else xa
            xc = xc + cj * convw_ref[j:j + 1, :]
        xcb = xc.astype(_BF)
        logsig = LRU_C * _log_sigmoid(lam_ref[...])
        for n in range(LRU_BLOCKS):
            ln = slice(n * LRU_BW, (n + 1) * LRU_BW)
            a, u = _lru_block(xc[:, ln], _dot(xcb[:, ln], wgate_s[n]), bgx_ref[:, ln],
                              bga_ref[:, ln], logsig[:, ln])
            h = a * h0_ref[:, ln] + u
            hnew_ref[:, ln] = h
            sylru_ref[:, ln] = h * _silu(za[:, ln])

    @pl.when(i == 0)
    def _():
        def out_rows(j):
            return pl.ds(pl.multiple_of(j * STAGE_ROWS, STAGE_ROWS), STAGE_ROWS)

        def store_out(j, val):
            wout_s[out_rows(j), :] = val

        _stage_in_proj(wt_hbm, wmain_s, stage, sem)
        _stage_blocks(D_MIX // STAGE_ROWS, lambda j: wout_hbm.at[out_rows(j), :], store_out,
                      stage, sem)
        glr_rows = pltpu.make_async_copy(wt_hbm.at[pl.ds(O_GLR, RANK), :],
                                         stage.at[0, pl.ds(0, RANK), :], sem.at[0])
        glr_rows.start()
        wglr_s[...] = jnp.zeros_like(wglr_s)
        wg2_s[...] = jnp.zeros_like(wg2_s)
        wg2_s[0:RANK, :] = wg2_ref[...].astype(_BF)
        wgate_s[:, :, 0:LRU_BW] = wgx_ref[...].astype(_BF)
        wgate_s[:, :, LRU_BW:2 * LRU_BW] = wga_ref[...].astype(_BF)
        glr_rows.wait()
        wglr_s[0:RANK, :] = stage[0, 0:RANK, :].astype(_BF)
        decode_front()
        so_s[...] = jnp.zeros_like(so_s)
        hlast_ref[...] = jnp.zeros_like(hlast_ref)
        convout_ref[...] = jnp.zeros_like(convout_ref)
        mix_s[...] = jnp.zeros_like(mix_s)
        xa_s[0:SUBLANES, :] = jnp.zeros((SUBLANES, D), _F32)
        front_norm(xp_ref, adap_ref)
        front_xa()
        front_za()
        front_conv()
        front_gates()

    @pl.when(t == 0)
    def _():
        h_s[...] = jnp.zeros_like(h_s)
        st_s[...] = jnp.zeros_like(st_s)

    def out_slab(j):
        def run(_=None):
            cols = slice(j * quarter, (j + 1) * quarter)
            o = _dot(mix_s[1 - slot], wout_s[:, cols])
            yacc_s[:, cols] = xp_ref[0, :, cols] + adap_ref[0, 2:3, cols] * o
        return run

    def out_norm(_=None):
        y_ref[0] = _rms(yacc_s[...]) * gfin_ref[...]

    @pl.when(i < n_tiles)
    def _():
        def after(val, token):
            sixteen = jnp.int32(16)
            zero = lax.shift_right_logical(
                lax.shift_right_logical(token.astype(jnp.int32), sixteen), sixteen)
            return jnp.where(zero == 0, val, -val)

        def proj(dst, dst_cols, group, w_cols):
            def run(lru_rows):
                r = _dot(hn_s[...], wmain_s[group, :, w_cols])
                dst[:, dst_cols] = r
                c0 = dst_cols.start
                dst[0:SUBLANES, c0:c0 + LRU_BW] = after(r[0:SUBLANES, 0:LRU_BW], lru_rows)
            return run

        def decay(_=None):
            glr = lax.dot_general(hn_s[...], wglr_s[...], _NT,
                                  preferred_element_type=_F32)
            la = _log_sigmoid(_dot(glr.astype(_BF), wg2_s[...]) + bg2_ref[...]) / TAU
            b_s[...] = _chunk_cumsum(la)

        lo, hi, full = slice(0, half), slice(half, D), slice(0, D_QK)
        tasks = [
            (out_slab(0), proj(q_s, full, 2, lo)),
            (out_slab(1), proj(k_s, full, 2, hi)),
            (out_slab(2), decay),
            (out_slab(3), out_norm),
            (proj(v_s, lo, 3, lo),),
            (proj(v_s, hi, 3, hi),),
            (proj(zg_s, lo, 4, lo),),
            (proj(zg_s, hi, 4, hi),),
        ]

        logsig = LRU_C * _log_sigmoid(lam_ref[...])
        for n in range(LRU_BLOCKS):
            ln = slice(n * LRU_BW, (n + 1) * LRU_BW)
            a, u = _lru_block(xc_s[:, ln], pre_s[:, 2 * n * LRU_BW:2 * (n + 1) * LRU_BW],
                              bgx_ref[:, ln], bga_ref[:, ln], logsig[:, ln])
            h = _linear_scan(a, u, h_s[:, ln])
            h_s[:, ln] = h[tm - 1:tm]
            mix_s[slot, :, ln] = (h * _silu(za_s[:, ln])).astype(_BF)
            for task in tasks[n]:
                task(h[tm - SUBLANES:tm])

        tril = (lax.broadcasted_iota(jnp.int32, (CHUNK, CHUNK), 0)
                >= lax.broadcasted_iota(jnp.int32, (CHUNK, CHUNK), 1))

        def gla_chunk(c):
            rows = slice(c * CHUNK, (c + 1) * CHUNK)
            for h in range(HEADS):
                lk = slice(h * DK, (h + 1) * DK)
                lv = slice(h * DV, (h + 1) * DV)
                b = b_s[rows, lk]
                bl = b[CHUNK - 1:CHUNK]
                q = q_s[rows, lk] * (DK ** -0.5)
                k = k_s[rows, lk]
                v = v_s[rows, lv]
                qi = (q * jnp.exp(b)).astype(_BF)
                ki = (k * jnp.exp(-b)).astype(_BF)
                kd = (k * jnp.exp(bl - b)).astype(_BF)
                vb = v.astype(_BF)
                att = lax.dot_general(qi, ki, _NT, preferred_element_type=_F32)
                att = jnp.where(tril, att, 0.0).astype(_BF)
                st = st_s[h]
                o = _dot(att, vb) + lax.dot_general(qi, st.astype(_BF), _NT,
                                                    preferred_element_type=_F32)
                st_s[h] = st * jnp.exp(bl) + _dot(v.T.astype(_BF), kd)
                o = _rms(o) * ggla_ref[h:h + 1, :]
                mix_s[slot, rows, D + h * DV:D + (h + 1) * DV] = (
                    o * _silu(zg_s[rows, lv])).astype(_BF)

        def next_tail():
            tail = xa_s[tm:tm + SUBLANES, :]
            ctail_s[...] = tail
            xa_s[0:SUBLANES, :] = jnp.where(t_next != 0, tail, 0.0)

        next_tail()
        front_norm(xn_ref, adan_ref)
        gla_chunk(0)
        front_xa()
        gla_chunk(1)
        front_za()
        front_conv()
        gla_chunk(2)
        front_gates()
        gla_chunk(3)

        first = i * s_per_step
        grp = pl.ds(pl.multiple_of((first // SUBLANES) * SUBLANES, SUBLANES), SUBLANES)
        r0 = lax.rem(first, SUBLANES)
        sub = lax.broadcasted_iota(jnp.int32, (SUBLANES, 1), 0)
        g_qs, g_k, g_eg, g_v = sqs_ref[grp, :], sk_ref[grp, :], seg_ref[grp, :], sv_ref[grp, :]
        g_o = so_s[grp, :]
        for j in range(s_per_step):
            mine = sub == r0 + j
            pick = lambda a: jnp.sum(jnp.where(mine, a, 0.0), axis=0, keepdims=True)
            qs_j, k_j, eg_j, v_j = pick(g_qs), pick(g_k), pick(g_eg), pick(g_v)
            o_heads = []
            for h in range(HEADS):
                lk = slice(h * DK, (h + 1) * DK)
                lv = slice(h * DV, (h + 1) * DV)
                o, snew = _gla_token_step(qs_j[:, lk], k_j[:, lk], eg_j[:, lk], v_j[:, lv],
                                          ss0_ref[j, h])
                o_heads.append(o)
                ssnew_ref[j, h] = snew
            g_o = jnp.where(mine, jnp.concatenate(o_heads, axis=-1), g_o)
        so_s[grp, :] = g_o

    @pl.when(i == n_tiles)
    def _():
        for j in range(D // quarter):
            out_slab(j)()
        out_norm()
        parts = [sylru_ref[...].astype(_BF)]
        for h in range(HEADS):
            lv = slice(h * DV, (h + 1) * DV)
            o = _rms(so_s[:, lv]) * ggla_ref[h:h + 1, :]
            parts.append((o * _silu(szg_ref[:, lv])).astype(_BF))
        ys = sx_ref[:, 0, :] + sgate_ref[...] * _dot(jnp.concatenate(parts, axis=-1), wout_s[...])
        ys_ref[:, 0, :] = _rms(ys) * gfin_ref[...]

    @pl.when((t == nt - 1) & (i < n_tiles))
    def _():
        mine = lax.broadcasted_iota(jnp.int32, (bp, 1), 0) == jnp.minimum(i, n_tiles - 1) // nt
        hlast_ref[...] = jnp.where(mine, h_s[...], hlast_ref[...])
        for m in range(CONV_W - 1):
            row = SUBLANES - (CONV_W - 1) + m
            convout_ref[m] = jnp.where(mine, ctail_s[row:row + 1, :], convout_ref[m])
        for h in range(HEADS):
            sout_ref[0, h] = st_s[h].T


def _const_spec(shape):
    nd = len(shape)
    return pl.BlockSpec(shape, lambda i: (0,) * nd, pipeline_mode=pl.Buffered(1))


def _prompt(x, c_prompt, c_sample, w_ada, b_ada, gnorm, w_in_t, w_out2d, convw, convb, wgx, wga,
            bgx, bga, lam, wg2, bg2, ggla, gfin, ss0, sx, h0, conv0, tm=TM):
    bsz, seq, _ = x.shape
    assert bsz % SUBLANES == 0, "decode rows must start on a sublane-tile boundary of the adaLN rows"
    nt = seq // tm
    n_tiles = bsz * nt
    ns = ss0.shape[0]
    s_per_step = -(-ns // n_tiles)
    s_blocks = ns // s_per_step
    assert tm // CHUNK == 4, "the step body spreads the next tile's input side over 4 GLA chunks"
    assert s_blocks * s_per_step == ns and s_blocks <= n_tiles and SUBLANES % s_per_step == 0
    kern = functools.partial(_prompt_kernel, tm=tm, nt=nt, n_tiles=n_tiles,
                             s_per_step=s_per_step, s_blocks=s_blocks)
    out_shape = (
        jax.ShapeDtypeStruct((bsz, seq, D), _F32),
        jax.ShapeDtypeStruct((bsz, D), _F32),
        jax.ShapeDtypeStruct((CONV_W - 1, bsz, D), _F32),
        jax.ShapeDtypeStruct((bsz, HEADS, DK, DV), _F32),
        jax.ShapeDtypeStruct((ns, 1, D), _F32),
        jax.ShapeDtypeStruct(ss0.shape, _F32),
        jax.ShapeDtypeStruct((ns, D), _F32),
        jax.ShapeDtypeStruct((CONV_W - 1, ns, D), _F32),
    )
    s_block = lambda i: (jnp.minimum(i, s_blocks - 1), 0, 0, 0)
    cur = lambda i: jnp.minimum(i, n_tiles - 1)
    prev = lambda i: jnp.maximum(i - 1, 0)
    nxt = lambda i: jnp.minimum(i + 1, n_tiles - 1)
    tile = lambda f: (lambda i: (f(i) // nt, lax.rem(f(i), nt), 0))
    seq_of = lambda f: (lambda i: (f(i) // nt, 0, 0))
    hbm = pl.BlockSpec(memory_space=pl.ANY)
    in_specs = [
        pl.BlockSpec((1, tm, D), tile(nxt)),
        pl.BlockSpec((1, tm, D), tile(prev)),
        _const_spec((bsz, D)),
        _const_spec((ns, D)),
        hbm,
        _const_spec((1, 3 * D)),
        _const_spec((1, D)),
        hbm, hbm,
        _const_spec((CONV_W, D)),
        _const_spec((1, D)),
        _const_spec((LRU_BLOCKS, LRU_BW, LRU_BW)),
        _const_spec((LRU_BLOCKS, LRU_BW, LRU_BW)),
        _const_spec((1, D)),
        _const_spec((1, D)),
        _const_spec((1, D)),
        _const_spec((RANK, D_QK)),
        _const_spec((1, D_QK)),
        _const_spec((HEADS, DV)),
        _const_spec((1, D)),
        pl.BlockSpec((s_per_step, HEADS, DK, DV), s_block),
        _const_spec((ns, 1, D)),
        _const_spec((ns, D)),
        _const_spec((CONV_W - 1, ns, D)),
    ]
    out_specs = (
        pl.BlockSpec((1, tm, D), tile(prev)),
        pl.BlockSpec((bsz, D), lambda i: (0, 0)),
        pl.BlockSpec((CONV_W - 1, bsz, D), lambda i: (0, 0, 0)),
        pl.BlockSpec((1, HEADS, DK, DV), lambda i: (cur(i) // nt, 0, 0, 0)),
        pl.BlockSpec((ns, 1, D), lambda i: (0, 0, 0)),
        pl.BlockSpec((s_per_step, HEADS, DK, DV), s_block),
        pl.BlockSpec((ns, D), lambda i: (0, 0)),
        pl.BlockSpec((CONV_W - 1, ns, D), lambda i: (0, 0, 0)),
    )
    scratch = [
        pltpu.VMEM((N_GROUPS, D, D), _BF),
        pltpu.VMEM((D_MIX, D), _BF),
        pltpu.VMEM((RANK_PAD, D), _BF),
        pltpu.VMEM((RANK_PAD, D_QK), _BF),
        pltpu.VMEM((LRU_BLOCKS, LRU_BW, 2 * LRU_BW), _BF),
        pltpu.VMEM((2, STAGE_ROWS, D), _F32),
        pltpu.SemaphoreType.DMA((2,)),
        pltpu.VMEM((tm, D), _BF),
        pltpu.VMEM((tm + SUBLANES, D), _F32),
        pltpu.VMEM((tm, D), _F32),
        pltpu.VMEM((tm, D), _F32),
        pltpu.VMEM((tm, 2 * D), _F32),
        pltpu.VMEM((tm, D_QK), _F32),
        pltpu.VMEM((tm, D_QK), _F32),
        pltpu.VMEM((tm, D), _F32),
        pltpu.VMEM((tm, D), _F32),
        pltpu.VMEM((tm, D_QK), _F32),
        pltpu.VMEM((2, tm, D_MIX), _BF),
        pltpu.VMEM((tm, D), _F32),
        pltpu.VMEM((SUBLANES, D), _F32),
        pltpu.VMEM((1, D), _F32),
        pltpu.VMEM((HEADS, DV, DK), _F32),
        pltpu.VMEM((ns, D), _F32),
        pltpu.VMEM((bsz, SUBLANES, D), _F32),
        pltpu.VMEM((ns, D_QK), _F32),
        pltpu.VMEM((ns, D_QK), _F32),
        pltpu.VMEM((ns, D_QK), _F32),
        pltpu.VMEM((ns, D), _F32),
        pltpu.VMEM((ns, D), _F32),
        pltpu.VMEM((ns, D), _F32),
        pltpu.VMEM((ns, D), _F32),
    ]
    return pl.pallas_call(
        kern,
        out_shape=out_shape,
        grid=(n_tiles + 1,),
        in_specs=in_specs,
        out_specs=out_specs,
        scratch_shapes=scratch,
        compiler_params=pltpu.CompilerParams(
            dimension_semantics=("arbitrary",),
            vmem_limit_bytes=VMEM_LIMIT),
        name="prompt_layer",
    )(x, x, c_prompt, c_sample, w_ada, b_ada, gnorm, w_in_t, w_out2d, convw, convb, wgx, wga, bgx,
      bga, lam, wg2, bg2, ggla, gfin, ss0, sx, h0, conv0)


def kernel(x_prompt, x_sample, state_lru_h, state_lru_conv, state_gla, c_prompt, c_sample,
           g_norm, w_ada, b_ada, w_in, conv_w, conv_b, w_gate_x, b_gate_x, w_gate_a, b_gate_a,
           lru_lambda, w_gla_g2, b_gla_g2, g_gla_norm, w_out, g_final):
    depth = g_norm.shape[0]
    assert depth == 1, "single-layer trunk"
    ns = x_sample.shape[0]

    w_in_t = jnp.swapaxes(w_in, 1, 2).reshape(w_in.shape[-1], D)
    w_out2d = w_out.reshape(D_MIX, D)
    wg2 = w_gla_g2.reshape(RANK, D_QK)
    wgx = w_gate_x.reshape(LRU_BLOCKS, LRU_BW, LRU_BW)
    wga = w_gate_a.reshape(LRU_BLOCKS, LRU_BW, LRU_BW)
    row = lambda a: a.reshape(1, -1)
    gnorm, convb = row(g_norm), row(conv_b)
    bgx, bga, lam = row(b_gate_x), row(b_gate_a), row(lru_lambda)
    bg2, gfin = row(b_gla_g2), row(g_final)
    ggla = g_gla_norm.reshape(HEADS, DV)
    convw = conv_w.reshape(CONV_W, D)

    step_major = lambda a: jnp.swapaxes(a, 0, 1)
    yp, hp, cp, sp, ys, ss, hs, cs = _prompt(
        x_prompt, c_prompt, c_sample, w_ada.reshape(D, 3 * D),
        row(b_ada), gnorm, w_in_t, w_out2d, convw, convb, wgx, wga, bgx, bga, lam, wg2, bg2,
        ggla, gfin, state_gla.reshape(ns, HEADS, DK, DV), x_sample, state_lru_h.reshape(ns, D),
        step_major(state_lru_conv.reshape(ns, CONV_W - 1, D)))

    return (yp, ys, hp[None], step_major(cp)[None], sp[None],
            hs[None], step_major(cs)[None], ss[None])
```

```python
import functools

import jax
import jax.numpy as jnp
from jax import lax
from jax.experimental import pallas as pl
from jax.experimental.pallas import tpu as pltpu

EPS = 1e-6
D = 1024
LRU_BLOCKS = 8
LRU_BW = D // LRU_BLOCKS
LRU_C = 8.0
CONV_W = 4
HEADS = 4
DV = D // HEADS
DK = DV // 2
RANK = 16
RANK_PAD = 128
TAU = 16.0
CHUNK = 64
D_QK = HEADS * DK
D_MIX = 2 * D
O_GLR = 4 * D
N_GROUPS = 5

SUBLANES = 8
TM = 256
STAGE_ROWS = 256
VMEM_LIMIT = 56 * 1024 * 1024
NEG_LOG2E = -1.4426950408889634

_BF = jnp.bfloat16
_F32 = jnp.float32

_NT = (((1,), (1,)), ((), ()))


def _sigmoid(x):
    return 1.0 / (1.0 + jnp.exp2(x * NEG_LOG2E))


def _silu(x):
    return x * _sigmoid(x)


def _log_sigmoid(x):
    return jnp.minimum(x, 0.0) - jnp.log(1.0 + jnp.exp2(jnp.abs(x) * NEG_LOG2E))


def _rms(x):
    return x * lax.rsqrt(jnp.mean(x * x, axis=-1, keepdims=True) + EPS)


def _dot(a, b):
    return jnp.dot(a, b, preferred_element_type=_F32)


def _group_scan(a3, u3):
    row = lax.broadcasted_iota(jnp.int32, a3.shape, 1)
    for s in (1, 2, 4):
        a_sh = pltpu.roll(a3, s, axis=1)
        u_sh = pltpu.roll(u3, s, axis=1)
        m = row >= s
        u3 = jnp.where(m, a3 * u_sh + u3, u3)
        a3 = jnp.where(m, a3 * a_sh, a3)
    return a3, u3


def _group_cumsum(x3):
    row = lax.broadcasted_iota(jnp.int32, x3.shape, 1)
    for s in (1, 2, 4):
        x3 = jnp.where(row >= s, x3 + pltpu.roll(x3, s, axis=1), x3)
    return x3


def _linear_scan(a, u, h0):
    r, l = a.shape
    g = r // SUBLANES
    a3, u3 = _group_scan(a.reshape(g, SUBLANES, l), u.reshape(g, SUBLANES, l))
    carry = h0
    out = []
    for i in range(g):
        hg = a3[i] * carry + u3[i]
        out.append(hg)
        carry = hg[SUBLANES - 1:SUBLANES]
    return jnp.concatenate(out, axis=0)


def _chunk_cumsum(x):
    r, l = x.shape
    g = r // SUBLANES
    per_chunk = CHUNK // SUBLANES
    x3 = _group_cumsum(x.reshape(g, SUBLANES, l))
    out = []
    carry = None
    for i in range(g):
        xg = x3[i] if i % per_chunk == 0 else x3[i] + carry
        out.append(xg)
        carry = xg[SUBLANES - 1:SUBLANES]
    return jnp.concatenate(out, axis=0)


def _lru_block(xc_n, pre, bgx, bga, c_logsig):
    gx = _sigmoid(pre[:, :LRU_BW] + bgx)
    ga = _sigmoid(pre[:, LRU_BW:] + bga)
    log_a = ga * c_logsig
    a = jnp.exp(log_a)
    om = 1.0 - a * a
    mult = jnp.where(om > 0.0, om * lax.rsqrt(om), 0.0)
    u = mult * (gx * xc_n)
    return a, u


def _gla_token_step(qs, k, eg, v, s0):
    eye = (lax.broadcasted_iota(jnp.int32, (DK, DK), 0)
           == lax.broadcasted_iota(jnp.int32, (DK, DK), 1))

    def to_col(row):
        return jnp.sum(jnp.where(eye, jnp.broadcast_to(row, (DK, DK)), 0.0),
                       axis=-1, keepdims=True)

    qk = jnp.sum(qs * k, axis=-1, keepdims=True)
    qg = jnp.broadcast_to(qs * eg, (SUBLANES, DK)).astype(_BF)
    o = _dot(qg, s0.astype(_BF))[0:1] + qk * v
    return o, to_col(eg) * s0 + to_col(k) * v


def _stage_blocks(n, src_block, store_block, stage, sem):
    def copy(i, slot):
        return pltpu.make_async_copy(src_block(i), stage.at[slot], sem.at[slot])

    copy(0, 0).start()

    def body(i, carry):
        slot = lax.rem(i, 2)

        @pl.when(i + 1 < n)
        def _():
            copy(i + 1, 1 - slot).start()

        copy(i, slot).wait()
        store_block(i, stage[slot].astype(_BF))
        return carry

    lax.fori_loop(0, n, body, 0)


def _stage_in_proj(wt_hbm, wmain_s, stage, sem):
    per_group = D // STAGE_ROWS

    def block(g, c):
        row0 = g * D + (g // (N_GROUPS - 1)) * RANK + c * STAGE_ROWS
        return wt_hbm.at[pl.ds(pl.multiple_of(row0, SUBLANES), STAGE_ROWS), :]

    def copy(g, c):
        return pltpu.make_async_copy(block(g, c), stage.at[c % 2], sem.at[c % 2])

    copy(0, 0).start()

    def body(g, carry):
        for c in range(per_group):
            if c + 1 < per_group:
                copy(g, c + 1).start()
            else:
                @pl.when(g + 1 < N_GROUPS)
                def _():
                    copy(g + 1, 0).start()
            copy(g, c).wait()
            wmain_s[g, :, c * STAGE_ROWS:(c + 1) * STAGE_ROWS] = stage[c % 2].T.astype(_BF)
        return carry

    lax.fori_loop(0, N_GROUPS, body, 0)


def _prompt_kernel(xn_ref, xp_ref, cp_ref, cs_ref, wada_hbm, bada_ref, gnorm_ref, wt_hbm, wout_hbm,
                   convw_ref, convb_ref, wgx_ref, wga_ref, bgx_ref, bga_ref, lam_ref, wg2_ref,
                   bg2_ref, ggla_ref, gfin_ref, ss0_ref, sx_ref, h0_ref, conv0_ref,
                   y_ref, hlast_ref, convout_ref, sout_ref, ys_ref, ssnew_ref, hnew_ref, cnew_ref,
                   wmain_s, wout_s, wglr_s, wg2_s, wgate_s, stage, sem,
                   hn_s, xa_s, za_s, xc_s, pre_s, q_s, k_s, v_s, zg_s, b_s, mix_s, yacc_s, ctail_s,
                   h_s, st_s, so_s, ada_s, sqs_ref, sk_ref, seg_ref, sv_ref, sgate_ref, szg_ref,
                   sylru_ref, *, tm, nt, n_tiles, s_per_step, s_blocks):
    i = pl.program_id(0)
    bp = n_tiles // nt
    adan_ref = ada_s.at[pl.ds(jnp.minimum(i + 1, n_tiles - 1) // nt, 1)]
    adap_ref = ada_s.at[pl.ds(jnp.maximum(i - 1, 0) // nt, 1)]
    t = lax.rem(jnp.minimum(i, n_tiles - 1), nt)
    t_next = lax.rem(jnp.minimum(i + 1, n_tiles - 1), nt)
    slot = lax.rem(i, 2)
    half = D // 2
    quarter = D // 4

    def front_norm(x_ref, ada_ref):
        x = x_ref[0]
        gain = gnorm_ref[...] * (1.0 + ada_ref[0, 1:2, :])
        hn_s[...] = (_rms(x) * gain + ada_ref[0, 0:1, :]).astype(_BF)

    def front_xa():
        xa_s[SUBLANES:SUBLANES + tm, :] = _dot(hn_s[...], wmain_s[0])

    def front_za():
        za_s[...] = _dot(hn_s[...], wmain_s[1])

    def front_conv():
        xe = xa_s[...]
        xc = convb_ref[...]
        for j in range(CONV_W):
            lag = CONV_W - 1 - j
            xj = xe if lag == 0 else pltpu.roll(xe, lag, axis=0)
            xc = xc + xj[SUBLANES:, :] * convw_ref[j:j + 1, :]
        xc_s[...] = xc

    def front_gates():
        for n in range(LRU_BLOCKS):
            pre_s[:, 2 * n * LRU_BW:2 * (n + 1) * LRU_BW] = _dot(
                xc_s[:, n * LRU_BW:(n + 1) * LRU_BW].astype(_BF), wgate_s[n])

    def decode_front():
        sc = jnp.concatenate([_silu(cp_ref[...]), _silu(cs_ref[...])], axis=0).astype(_BF)
        blocks = [(j, kb) for j in range(3) for kb in range(D // STAGE_ROWS)]

        def ada_copy(n):
            j, kb = blocks[n]
            return pltpu.make_async_copy(
                wada_hbm.at[pl.ds(kb * STAGE_ROWS, STAGE_ROWS), pl.ds(j * D, D)],
                stage.at[n % 2], sem.at[n % 2])

        ada_copy(0).start()
        ada = []
        acc = None
        for n, (j, kb) in enumerate(blocks):
            if n + 1 < len(blocks):
                ada_copy(n + 1).start()
            ada_copy(n).wait()
            part = _dot(sc[:, kb * STAGE_ROWS:(kb + 1) * STAGE_ROWS], stage[n % 2].astype(_BF))
            acc = part if kb == 0 else acc + part
            if kb == D // STAGE_ROWS - 1:
                ada.append(acc + bada_ref[:, j * D:(j + 1) * D])
        for j in range(3):
            for b in range(bp):
                ada_s[b, j:j + 1, :] = ada[j][b:b + 1, :]
        shift, scale = ada[0][bp:, :], ada[1][bp:, :]
        sgate_ref[...] = ada[2][bp:, :]

        hnb = (_rms(sx_ref[:, 0, :]) * (gnorm_ref[...] * (1.0 + scale)) + shift).astype(_BF)
        xa = _dot(hnb, wmain_s[0])
        za = _dot(hnb, wmain_s[1])
        qk = _dot(hnb, wmain_s[2])
        sqs_ref[...] = qk[:, :D_QK] * (DK ** -0.5)
        sk_ref[...] = qk[:, D_QK:]
        sv_ref[...] = _dot(hnb, wmain_s[3])
        szg_ref[...] = _dot(hnb, wmain_s[4])
        glr = lax.dot_general(hnb, wglr_s[...], _NT, preferred_element_type=_F32)
        la = _log_sigmoid(_dot(glr.astype(_BF), wg2_s[...]) + bg2_ref[...]) / TAU
        seg_ref[...] = jnp.exp(la)
        for j in range(CONV_W - 2):
            cnew_ref[j] = conv0_ref[j + 1]
        cnew_ref[CONV_W - 2] = xa

        xc = convb_ref[...]
        for j in range(CONV_W):
            cj = conv0_ref[j] if j < CONV_W - 1 else xa
            xc = xc + cj * convw_ref[j:j + 1, :]
        xcb = xc.astype(_BF)
        logsig = LRU_C * _log_sigmoid(lam_ref[...])
        for n in range(LRU_BLOCKS):
            ln = slice(n * LRU_BW, (n + 1) * LRU_BW)
            a, u = _lru_block(xc[:, ln], _dot(xcb[:, ln], wgate_s[n]), bgx_ref[:, ln],
                              bga_ref[:, ln], logsig[:, ln])
            h = a * h0_ref[:, ln] + u
            hnew_ref[:, ln] = h
            sylru_ref[:, ln] = h * _silu(za[:, ln])

    @pl.when(i == 0)
    def _():
        def out_rows(j):
            return pl.ds(pl.multiple_of(j * STAGE_ROWS, STAGE_ROWS), STAGE_ROWS)

        def store_out(j, val):
            wout_s[out_rows(j), :] = val

        _stage_in_proj(wt_hbm, wmain_s, stage, sem)
        _stage_blocks(D_MIX // STAGE_ROWS, lambda j: wout_hbm.at[out_rows(j), :], store_out,
                      stage, sem)
        glr_rows = pltpu.make_async_copy(wt_hbm.at[pl.ds(O_GLR, RANK), :],
                                         stage.at[0, pl.ds(0, RANK), :], sem.at[0])
        glr_rows.start()
        wglr_s[...] = jnp.zeros_like(wglr_s)
        wg2_s[...] = jnp.zeros_like(wg2_s)
        wg2_s[0:RANK, :] = wg2_ref[...].astype(_BF)
        wgate_s[:, :, 0:LRU_BW] = wgx_ref[...].astype(_BF)
        wgate_s[:, :, LRU_BW:2 * LRU_BW] = wga_ref[...].astype(_BF)
        glr_rows.wait()
        wglr_s[0:RANK, :] = stage[0, 0:RANK, :].astype(_BF)
        decode_front()
        so_s[...] = jnp.zeros_like(so_s)
        hlast_ref[...] = jnp.zeros_like(hlast_ref)
        convout_ref[...] = jnp.zeros_like(convout_ref)
        mix_s[...] = jnp.zeros_like(mix_s)
        xa_s[0:SUBLANES, :] = jnp.zeros((SUBLANES, D), _F32)
        front_norm(xp_ref, adap_ref)
        front_xa()
        front_za()
        front_conv()
        front_gates()

    @pl.when(t == 0)
    def _():
        h_s[...] = jnp.zeros_like(h_s)
        st_s[...] = jnp.zeros_like(st_s)

    def out_slab(j):
        def run(_=None):
            cols = slice(j * quarter, (j + 1) * quarter)
            o = _dot(mix_s[1 - slot], wout_s[:, cols])
            yacc_s[:, cols] = xp_ref[0, :, cols] + adap_ref[0, 2:3, cols] * o
        return run

    def out_norm(_=None):
        y_ref[0] = _rms(yacc_s[...]) * gfin_ref[...]

    @pl.when(i < n_tiles)
    def _():
        def after(val, token):
            sixteen = jnp.int32(16)
            zero = lax.shift_right_logical(
                lax.shift_right_logical(token.astype(jnp.int32), sixteen), sixteen)
            return jnp.where(zero == 0, val, -val)

        def proj(dst, dst_cols, group, w_cols):
            def run(lru_rows):
                r = _dot(hn_s[...], wmain_s[group, :, w_cols])
                dst[:, dst_cols] = r
                c0 = dst_cols.start
                dst[0:SUBLANES, c0:c0 + LRU_BW] = after(r[0:SUBLANES, 0:LRU_BW], lru_rows)
            return run

        def decay(_=None):
            glr = lax.dot_general(hn_s[...], wglr_s[...], _NT,
                                  preferred_element_type=_F32)
            la = _log_sigmoid(_dot(glr.astype(_BF), wg2_s[...]) + bg2_ref[...]) / TAU
            b_s[...] = _chunk_cumsum(la)

        lo, hi, full = slice(0, half), slice(half, D), slice(0, D_QK)
        tasks = [
            (out_slab(0), proj(q_s, full, 2, lo)),
            (out_slab(1), proj(k_s, full, 2, hi)),
            (out_slab(2), decay),
            (out_slab(3), out_norm),
            (proj(v_s, lo, 3, lo),),
            (proj(v_s, hi, 3, hi),),
            (proj(zg_s, lo, 4, lo),),
            (proj(zg_s, hi, 4, hi),),
        ]

        logsig = LRU_C * _log_sigmoid(lam_ref[...])
        for n in range(LRU_BLOCKS):
            ln = slice(n * LRU_BW, (n + 1) * LRU_BW)
            a, u = _lru_block(xc_s[:, ln], pre_s[:, 2 * n * LRU_BW:2 * (n + 1) * LRU_BW],
                              bgx_ref[:, ln], bga_ref[:, ln], logsig[:, ln])
            h = _linear_scan(a, u, h_s[:, ln])
            h_s[:, ln] = h[tm - 1:tm]
            for task in tasks[n]:
                task(h[tm - SUBLANES:tm])
            mix_s[slot, :, ln] = (h * _silu(za_s[:, ln])).astype(_BF)

        tril = (lax.broadcasted_iota(jnp.int32, (CHUNK, CHUNK), 0)
                >= lax.broadcasted_iota(jnp.int32, (CHUNK, CHUNK), 1))

        def gla_chunk(c):
            rows = slice(c * CHUNK, (c + 1) * CHUNK)
            for h in range(HEADS):
                lk = slice(h * DK, (h + 1) * DK)
                lv = slice(h * DV, (h + 1) * DV)
                b = b_s[rows, lk]
                bl = b[CHUNK - 1:CHUNK]
                q = q_s[rows, lk] * (DK ** -0.5)
                k = k_s[rows, lk]
                v = v_s[rows, lv]
                qi = (q * jnp.exp(b)).astype(_BF)
                ki = (k * jnp.exp(-b)).astype(_BF)
                kd = (k * jnp.exp(bl - b)).astype(_BF)
                vb = v.astype(_BF)
                att = lax.dot_general(qi, ki, _NT, preferred_element_type=_F32)
                att = jnp.where(tril, att, 0.0).astype(_BF)
                st = st_s[h]
                o = _dot(att, vb) + lax.dot_general(qi, st.astype(_BF), _NT,
                                                    preferred_element_type=_F32)
                st_s[h] = st * jnp.exp(bl) + _dot(v.T.astype(_BF), kd)
                o = _rms(o) * ggla_ref[h:h + 1, :]
                mix_s[slot, rows, D + h * DV:D + (h + 1) * DV] = (
                    o * _silu(zg_s[rows, lv])).astype(_BF)

        def next_tail():
            tail = xa_s[tm:tm + SUBLANES, :]
            ctail_s[...] = tail
            xa_s[0:SUBLANES, :] = jnp.where(t_next != 0, tail, 0.0)

        next_tail()
        front_norm(xn_ref, adan_ref)
        gla_chunk(0)
        front_xa()
        gla_chunk(1)
        front_za()
        front_conv()
        gla_chunk(2)
        front_gates()
        gla_chunk(3)

        first = i * s_per_step
        grp = pl.ds(pl.multiple_of((first // SUBLANES) * SUBLANES, SUBLANES), SUBLANES)
        r0 = lax.rem(first, SUBLANES)
        sub = lax.broadcasted_iota(jnp.int32, (SUBLANES, 1), 0)
        g_qs, g_k, g_eg, g_v = sqs_ref[grp, :], sk_ref[grp, :], seg_ref[grp, :], sv_ref[grp, :]
        g_o = so_s[grp, :]
        for j in range(s_per_step):
            mine = sub == r0 + j
            pick = lambda a: jnp.sum(jnp.where(mine, a, 0.0), axis=0, keepdims=True)
            qs_j, k_j, eg_j, v_j = pick(g_qs), pick(g_k), pick(g_eg), pick(g_v)
            o_heads = []
            for h in range(HEADS):
                lk = slice(h * DK, (h + 1) * DK)
                lv = slice(h * DV, (h + 1) * DV)
                o, snew = _gla_token_step(qs_j[:, lk], k_j[:, lk], eg_j[:, lk], v_j[:, lv],
                                          ss0_ref[j, h])
                o_heads.append(o)
                ssnew_ref[j, h] = snew
            g_o = jnp.where(mine, jnp.concatenate(o_heads, axis=-1), g_o)
        so_s[grp, :] = g_o

    @pl.when(i == n_tiles)
    def _():
        for j in range(D // quarter):
            out_slab(j)()
        out_norm()
        parts = [sylru_ref[...].astype(_BF)]
        for h in range(HEADS):
            lv = slice(h * DV, (h + 1) * DV)
            o = _rms(so_s[:, lv]) * ggla_ref[h:h + 1, :]
            parts.append((o * _silu(szg_ref[:, lv])).astype(_BF))
        ys = sx_ref[:, 0, :] + sgate_ref[...] * _dot(jnp.concatenate(parts, axis=-1), wout_s[...])
        ys_ref[:, 0, :] = _rms(ys) * gfin_ref[...]

    @pl.when((t == nt - 1) & (i < n_tiles))
    def _():
        mine = lax.broadcasted_iota(jnp.int32, (bp, 1), 0) == jnp.minimum(i, n_tiles - 1) // nt
        hlast_ref[...] = jnp.where(mine, h_s[...], hlast_ref[...])
        for m in range(CONV_W - 1):
            row = SUBLANES - (CONV_W - 1) + m
            convout_ref[m] = jnp.where(mine, ctail_s[row:row + 1, :], convout_ref[m])
        for h in range(HEADS):
            sout_ref[0, h] = st_s[h].T


def _const_spec(shape):
    nd = len(shape)
    return pl.BlockSpec(shape, lambda i: (0,) * nd, pipeline_mode=pl.Buffered(1))


def _prompt(x, c_prompt, c_sample, w_ada, b_ada, gnorm, w_in_t, w_out2d, convw, convb, wgx, wga,
            bgx, bga, lam, wg2, bg2, ggla, gfin, ss0, sx, h0, conv0, tm=TM):
    bsz, seq, _ = x.shape
    assert bsz % SUBLANES == 0, "decode rows must start on a sublane-tile boundary of the adaLN rows"
    nt = seq // tm
    n_tiles = bsz * nt
    ns = ss0.shape[0]
    s_per_step = -(-ns // n_tiles)
    s_blocks = ns // s_per_step
    assert tm // CHUNK == 4, "the step body spreads the next tile's input side over 4 GLA chunks"
    assert s_blocks * s_per_step == ns and s_blocks <= n_tiles and SUBLANES % s_per_step == 0
    kern = functools.partial(_prompt_kernel, tm=tm, nt=nt, n_tiles=n_tiles,
                             s_per_step=s_per_step, s_blocks=s_blocks)
    out_shape = (
        jax.ShapeDtypeStruct((bsz, seq, D), _F32),
        jax.ShapeDtypeStruct((bsz, D), _F32),
        jax.ShapeDtypeStruct((CONV_W - 1, bsz, D), _F32),
        jax.ShapeDtypeStruct((bsz, HEADS, DK, DV), _F32),
        jax.ShapeDtypeStruct((ns, 1, D), _F32),
        jax.ShapeDtypeStruct(ss0.shape, _F32),
        jax.ShapeDtypeStruct((ns, D), _F32),
        jax.ShapeDtypeStruct((CONV_W - 1, ns, D), _F32),
    )
    s_block = lambda i: (jnp.minimum(i, s_blocks - 1), 0, 0, 0)
    cur = lambda i: jnp.minimum(i, n_tiles - 1)
    prev = lambda i: jnp.maximum(i - 1, 0)
    nxt = lambda i: jnp.minimum(i + 1, n_tiles - 1)
    tile = lambda f: (lambda i: (f(i) // nt, lax.rem(f(i), nt), 0))
    seq_of = lambda f: (lambda i: (f(i) // nt, 0, 0))
    hbm = pl.BlockSpec(memory_space=pl.ANY)
    in_specs = [
        pl.BlockSpec((1, tm, D), tile(nxt)),
        pl.BlockSpec((1, tm, D), tile(prev)),
        _const_spec((bsz, D)),
        _const_spec((ns, D)),
        hbm,
        _const_spec((1, 3 * D)),
        _const_spec((1, D)),
        hbm, hbm,
        _const_spec((CONV_W, D)),
        _const_spec((1, D)),
        _const_spec((LRU_BLOCKS, LRU_BW, LRU_BW)),
        _const_spec((LRU_BLOCKS, LRU_BW, LRU_BW)),
        _const_spec((1, D)),
        _const_spec((1, D)),
        _const_spec((1, D)),
        _const_spec((RANK, D_QK)),
        _const_spec((1, D_QK)),
        _const_spec((HEADS, DV)),
        _const_spec((1, D)),
        pl.BlockSpec((s_per_step, HEADS, DK, DV), s_block),
        _const_spec((ns, 1, D)),
        _const_spec((ns, D)),
        _const_spec((CONV_W - 1, ns, D)),
    ]
    out_specs = (
        pl.BlockSpec((1, tm, D), tile(prev)),
        pl.BlockSpec((bsz, D), lambda i: (0, 0)),
        pl.BlockSpec((CONV_W - 1, bsz, D), lambda i: (0, 0, 0)),
        pl.BlockSpec((1, HEADS, DK, DV), lambda i: (cur(i) // nt, 0, 0, 0)),
        pl.BlockSpec((ns, 1, D), lambda i: (0, 0, 0)),
        pl.BlockSpec((s_per_step, HEADS, DK, DV), s_block),
        pl.BlockSpec((ns, D), lambda i: (0, 0)),
        pl.BlockSpec((CONV_W - 1, ns, D), lambda i: (0, 0, 0)),
    )
    scratch = [
        pltpu.VMEM((N_GROUPS, D, D), _BF),
        pltpu.VMEM((D_MIX, D), _BF),
        pltpu.VMEM((RANK_PAD, D), _BF),
        pltpu.VMEM((RANK_PAD, D_QK), _BF),
        pltpu.VMEM((LRU_BLOCKS, LRU_BW, 2 * LRU_BW), _BF),
        pltpu.VMEM((2, STAGE_ROWS, D), _F32),
        pltpu.SemaphoreType.DMA((2,)),
        pltpu.VMEM((tm, D), _BF),
        pltpu.VMEM((tm + SUBLANES, D), _F32),
        pltpu.VMEM((tm, D), _F32),
        pltpu.VMEM((tm, D), _F32),
        pltpu.VMEM((tm, 2 * D), _F32),
        pltpu.VMEM((tm, D_QK), _F32),
        pltpu.VMEM((tm, D_QK), _F32),
        pltpu.VMEM((tm, D), _F32),
        pltpu.VMEM((tm, D), _F32),
        pltpu.VMEM((tm, D_QK), _F32),
        pltpu.VMEM((2, tm, D_MIX), _BF),
        pltpu.VMEM((tm, D), _F32),
        pltpu.VMEM((SUBLANES, D), _F32),
        pltpu.VMEM((1, D), _F32),
        pltpu.VMEM((HEADS, DV, DK), _F32),
        pltpu.VMEM((ns, D), _F32),
        pltpu.VMEM((bsz, SUBLANES, D), _F32),
        pltpu.VMEM((ns, D_QK), _F32),
        pltpu.VMEM((ns, D_QK), _F32),
        pltpu.VMEM((ns, D_QK), _F32),
        pltpu.VMEM((ns, D), _F32),
        pltpu.VMEM((ns, D), _F32),
        pltpu.VMEM((ns, D), _F32),
        pltpu.VMEM((ns, D), _F32),
    ]
    return pl.pallas_call(
        kern,
        out_shape=out_shape,
        grid=(n_tiles + 1,),
        in_specs=in_specs,
        out_specs=out_specs,
        scratch_shapes=scratch,
        compiler_params=pltpu.CompilerParams(
            dimension_semantics=("arbitrary",),
            vmem_limit_bytes=VMEM_LIMIT),
        name="prompt_layer",
    )(x, x, c_prompt, c_sample, w_ada, b_ada, gnorm, w_in_t, w_out2d, convw, convb, wgx, wga, bgx,
      bga, lam, wg2, bg2, ggla, gfin, ss0, sx, h0, conv0)


def kernel(x_prompt, x_sample, state_lru_h, state_lru_conv, state_gla, c_prompt, c_sample,
           g_norm, w_ada, b_ada, w_in, conv_w, conv_b, w_gate_x, b_gate_x, w_gate_a, b_gate_a,
           lru_lambda, w_gla_g2, b_gla_g2, g_gla_norm, w_out, g_final):
    depth = g_norm.shape[0]
    assert depth == 1, "single-layer trunk"
    ns = x_sample.shape[0]

    w_in_t = jnp.swapaxes(w_in, 1, 2).reshape(w_in.shape[-1], D)
    w_out2d = w_out.reshape(D_MIX, D)
    wg2 = w_gla_g2.reshape(RANK, D_QK)
    wgx = w_gate_x.reshape(LRU_BLOCKS, LRU_BW, LRU_BW)
    wga = w_gate_a.reshape(LRU_BLOCKS, LRU_BW, LRU_BW)
    row = lambda a: a.reshape(1, -1)
    gnorm, convb = row(g_norm), row(conv_b)
    bgx, bga, lam = row(b_gate_x), row(b_gate_a), row(lru_lambda)
    bg2, gfin = row(b_gla_g2), row(g_final)
    ggla = g_gla_norm.reshape(HEADS, DV)
    convw = conv_w.reshape(CONV_W, D)

    step_major = lambda a: jnp.swapaxes(a, 0, 1)
    yp, hp, cp, sp, ys, ss, hs, cs = _prompt(
        x_prompt, c_prompt, c_sample, w_ada.reshape(D, 3 * D),
        row(b_ada), gnorm, w_in_t, w_out2d, convw, convb, wgx, wga, bgx, bga, lam, wg2, bg2,
        ggla, gfin, state_gla.reshape(ns, HEADS, DK, DV), x_sample, state_lru_h.reshape(ns, D),
        step_major(state_lru_conv.reshape(ns, CONV_W - 1, D)))

    return (yp, ys, hp[None], step_major(cp)[None], sp[None],
            hs[None], step_major(cs)[None], ss[None])
```
